```python
import math
import jax
import jax.numpy as jnp
from jax import lax
import numpy as np

D_MODEL = 1024
BATCH = 8
SEQ = 8192
DEPTH = 2

RMS_EPS = 1e-6
GN_EPS = 1e-5
ROPE_THETA = 10000.0

RET_HEADS = D_MODEL // 256
RET_QK_DIM = 128
RET_V_DIM = 2 * RET_QK_DIM
RET_CHUNK = 128

MLA_HEADS = D_MODEL // 128
MLA_Q_LORA = 3 * D_MODEL // 8
MLA_KV_LORA = D_MODEL // 4
MLA_NOPE = 128
MLA_ROPE = 64
MLA_V = 128
ATTN_BLOCK = 128

S5_WIDTH = D_MODEL
S5_GROUP = 16
S5_GROUPS = S5_WIDTH // S5_GROUP
S5_STATE = 64

N_BRANCH = 3
BRANCH_WIDTH = D_MODEL
FFN_HIDDEN = ((8 * D_MODEL + 3 * 256 - 1) // (3 * 256)) * 256

IN_SPLITS = (
    RET_HEADS * RET_QK_DIM,
    RET_HEADS * RET_QK_DIM,
    RET_HEADS * RET_V_DIM,
    RET_HEADS * RET_V_DIM,
    MLA_Q_LORA,
    MLA_KV_LORA,
    MLA_ROPE,
    S5_WIDTH,
    N_BRANCH * D_MODEL,
)
IN_WIDTH = sum(IN_SPLITS)

kernel_name = "hybrid_retention_mla_s5_gated_encoder"


def _rmsnorm(x, g):
    xf = x.astype(jnp.float32)
    y = xf * lax.rsqrt(jnp.mean(xf * xf, axis=-1, keepdims=True) + RMS_EPS)
    return y.astype(x.dtype) * g


def _rope_tables(seq, dim):
    inv = 1.0 / (ROPE_THETA ** (jnp.arange(0, dim, 2, dtype=jnp.float32) / dim))
    ang = jnp.arange(seq, dtype=jnp.float32)[:, None] * inv[None, :]
    return jnp.cos(ang), jnp.sin(ang)


def _apply_rope(x, cos, sin):
    half = x.shape[-1] // 2
    x1, x2 = x[..., :half], x[..., half:]
    return jnp.concatenate([x1 * cos - x2 * sin, x2 * cos + x1 * sin], axis=-1).astype(x.dtype)


def _retention_dir(q, k, v, log_g, strict):
    b, h, s, dk = q.shape
    dv = v.shape[-1]
    c = RET_CHUNK
    n = s // c
    q = q.reshape(b, h, n, c, dk)
    k = k.reshape(b, h, n, c, dk)
    v = v.reshape(b, h, n, c, dv)
    pos = jnp.arange(c, dtype=jnp.float32)
    diff = pos[:, None] - pos[None, :]
    mask = (diff > 0) if strict else (diff >= 0)
    decay_in = jnp.where(mask, jnp.exp(jnp.where(mask, diff, 0.0)[None] * log_g[:, None, None]), 0.0)
    scores = jnp.einsum('bhnid,bhnjd->bhnij', q, k) * decay_in[None, :, None]
    inner = jnp.einsum('bhnij,bhnje->bhnie', scores, v)
    k_w = jnp.exp((c - 1 - pos)[None, :] * log_g[:, None])
    chunk_kv = jnp.einsum('bhncd,bhnce->nbhde', k * k_w[None, :, None, :, None], v)
    chunk_decay = jnp.exp(c * log_g)[None, :, None, None]

    def step(state, kv):
        return chunk_decay * state + kv, state

    init = jnp.zeros(chunk_kv.shape[1:], chunk_kv.dtype)
    _, states = lax.scan(step, init, chunk_kv)
    q_w = jnp.exp((pos + 1)[None, :] * log_g[:, None])
    cross = jnp.einsum('bhncd,nbhde->bhnce', q, states) * q_w[None, :, None, :, None]
    return (inner + cross).reshape(b, h, s, dv)


def _retention_branch(q_flat, k_flat, v_flat, g_flat, ret_decay, gn_g, cos, sin):
    b, s, _ = q_flat.shape
    q = q_flat.reshape(b, s, RET_HEADS, RET_QK_DIM).transpose(0, 2, 1, 3)
    k = k_flat.reshape(b, s, RET_HEADS, RET_QK_DIM).transpose(0, 2, 1, 3)
    v = v_flat.reshape(b, s, RET_HEADS, RET_V_DIM).transpose(0, 2, 1, 3)
    q = _apply_rope(q, cos, sin) * (RET_QK_DIM ** -0.5)
    k = _apply_rope(k, cos, sin)
    log_g = jax.nn.log_sigmoid(ret_decay.astype(jnp.float32))
    fwd = _retention_dir(q, k, v, log_g[0], strict=False)
    bwd = _retention_dir(q[:, :, ::-1], k[:, :, ::-1], v[:, :, ::-1], log_g[1], strict=True)[:, :, ::-1]
    y = (fwd + bwd).astype(jnp.float32)
    mu = jnp.mean(y, axis=-1, keepdims=True)
    var = jnp.mean(jnp.square(y - mu), axis=-1, keepdims=True)
    y = (y - mu) * lax.rsqrt(var + GN_EPS)
    y = y.transpose(0, 2, 1, 3).reshape(b, s, RET_HEADS * RET_V_DIM).astype(q_flat.dtype) * gn_g
    return jax.nn.silu(g_flat) * y


def _mla_branch(c_q, c_kv, k_rope_flat, q_norm_g, w_uq, kv_norm_g, w_ukv, cos, sin):
    b, s, _ = c_q.shape
    q = (_rmsnorm(c_q, q_norm_g) @ w_uq).reshape(b, s, MLA_HEADS, MLA_NOPE + MLA_ROPE).transpose(0, 2, 1, 3)
    scale = (MLA_NOPE + MLA_ROPE) ** -0.5
    q_nope = q[..., :MLA_NOPE] * scale
    q_rope = _apply_rope(q[..., MLA_NOPE:], cos, sin) * scale
    kv = (_rmsnorm(c_kv, kv_norm_g) @ w_ukv).reshape(b, s, MLA_HEADS, MLA_NOPE + MLA_V).transpose(0, 2, 1, 3)
    k_nope, v = kv[..., :MLA_NOPE], kv[..., MLA_NOPE:]
    k_rope = _apply_rope(k_rope_flat, cos, sin)
    nb = s // ATTN_BLOCK
    qn_blocks = jnp.moveaxis(q_nope.reshape(b, MLA_HEADS, nb, ATTN_BLOCK, MLA_NOPE), 2, 0)
    qr_blocks = jnp.moveaxis(q_rope.reshape(b, MLA_HEADS, nb, ATTN_BLOCK, MLA_ROPE), 2, 0)

    def attend(blk):
        qn, qr = blk
        sc = (jnp.einsum('bhqd,bhkd->bhqk', qn, k_nope)
              + jnp.einsum('bhqd,bkd->bhqk', qr, k_rope))
        p = jax.nn.softmax(sc.astype(jnp.float32), axis=-1)
        return jnp.einsum('bhqk,bhkd->bhqd', p.astype(v.dtype), v)

    out = lax.map(attend, (qn_blocks, qr_blocks))
    return out.transpose(1, 0, 3, 2, 4).reshape(b, s, MLA_HEADS * MLA_V)


def _s5_direction(u_g, a_re, a_im, log_dt, b_re, b_im, c_re, c_im, reverse):
    dt = jnp.exp(log_dt)[:, None]
    ar = jnp.minimum(a_re, -1e-4)
    mag = jnp.exp(dt * ar)
    abar_re = mag * jnp.cos(dt * a_im)
    abar_im = mag * jnp.sin(dt * a_im)
    den = ar * ar + a_im * a_im
    nr = abar_re - 1.0
    ni = abar_im
    coef_re = (nr * ar + ni * a_im) / den
    coef_im = (ni * ar - nr * a_im) / den
    bb_re = coef_re[..., None] * b_re - coef_im[..., None] * b_im
    bb_im = coef_re[..., None] * b_im + coef_im[..., None] * b_re
    bu_re = jnp.einsum('bsgc,gpc->bsgp', u_g, bb_re)
    bu_im = jnp.einsum('bsgc,gpc->bsgp', u_g, bb_im)
    s = u_g.shape[1]
    a_re_t = jnp.broadcast_to(abar_re, (1, s) + abar_re.shape)
    a_im_t = jnp.broadcast_to(abar_im, (1, s) + abar_im.shape)

    def combine(e1, e2):
        a1r, a1i, b1r, b1i = e1
        a2r, a2i, b2r, b2i = e2
        return (a2r * a1r - a2i * a1i,
                a2r * a1i + a2i * a1r,
                a2r * b1r - a2i * b1i + b2r,
                a2r * b1i + a2i * b1r + b2i)

    _, _, xr, xi = lax.associative_scan(combine, (a_re_t, a_im_t, bu_re, bu_im), reverse=reverse, axis=1)
    return jnp.einsum('bsgp,gcp->bsgc', xr, c_re) - jnp.einsum('bsgp,gcp->bsgc', xi, c_im)


def _s5_branch(u, a_re, a_im, log_dt, b_re, b_im, c_re, c_im, d, w_glu):
    b, s, w = u.shape
    u_g = u.reshape(b, s, S5_GROUPS, S5_GROUP)
    y = (_s5_direction(u_g, a_re[0], a_im[0], log_dt[0], b_re[0], b_im[0], c_re[0], c_im[0], reverse=False)
         + _s5_direction(u_g, a_re[1], a_im[1], log_dt[1], b_re[1], b_im[1], c_re[1], c_im[1], reverse=True))
    y = y.reshape(b, s, w).astype(u.dtype) + d * u
    g = jax.nn.gelu(y)
    ga, gb = jnp.split(g @ w_glu, 2, axis=-1)
    return ga * jax.nn.sigmoid(gb)


def _fwd_setup_inputs(seed: int = 0) -> dict:
    key = jax.random.key(seed)
    ks = jax.random.split(key, 24)
    f32 = jnp.float32
    L, D, G, P, C = DEPTH, D_MODEL, S5_GROUPS, S5_STATE, S5_GROUP

    def nrm(k, shape, scale):
        return jax.random.normal(k, shape, f32) * scale

    def gain(k, shape):
        return 1.0 + 0.02 * jax.random.normal(k, shape, f32)

    ret_logit = jnp.log(2.0 ** (5.0 + jnp.arange(RET_HEADS, dtype=f32)) - 1.0)
    ret_decay = ret_logit[None, None, :] + 0.05 * jax.random.normal(ks[3], (L, 2, RET_HEADS), f32)
    a_im_init = math.pi * jnp.arange(P, dtype=f32)

    return {
        "x": jax.random.normal(ks[0], (BATCH, SEQ, D), f32),
        "norm1_g": gain(ks[1], (L, D)),
        "w_in": nrm(ks[2], (L, D, IN_WIDTH), D ** -0.5),
        "ret_decay": ret_decay,
        "ret_gn_g": gain(ks[4], (L, RET_HEADS * RET_V_DIM)),
        "mla_q_norm_g": gain(ks[5], (L, MLA_Q_LORA)),
        "mla_w_uq": nrm(ks[6], (L, MLA_Q_LORA, MLA_HEADS * (MLA_NOPE + MLA_ROPE)), MLA_Q_LORA ** -0.5),
        "mla_kv_norm_g": gain(ks[7], (L, MLA_KV_LORA)),
        "mla_w_ukv": nrm(ks[8], (L, MLA_KV_LORA, MLA_HEADS * (MLA_NOPE + MLA_V)), MLA_KV_LORA ** -0.5),
        "s5_a_re": -0.5 + 0.01 * jax.random.normal(ks[9], (L, 2, G, P), f32),
        "s5_a_im": a_im_init + 0.01 * jax.random.normal(ks[10], (L, 2, G, P), f32),
        "s5_log_dt": jax.random.uniform(ks[11], (L, 2, G), f32, math.log(0.001), math.log(0.1)),
        "s5_b_re": nrm(ks[12], (L, 2, G, P, C), (2 * C) ** -0.5),
        "s5_b_im": nrm(ks[13], (L, 2, G, P, C), (2 * C) ** -0.5),
        "s5_c_re": nrm(ks[14], (L, 2, G, C, P), (2 * P) ** -0.5),
        "s5_c_im": nrm(ks[15], (L, 2, G, C, P), (2 * P) ** -0.5),
        "s5_d": nrm(ks[16], (L, S5_WIDTH), 1.0),
        "s5_w_glu": nrm(ks[17], (L, S5_WIDTH, 2 * S5_WIDTH), S5_WIDTH ** -0.5),
        "w_branch": nrm(ks[18], (L, N_BRANCH, BRANCH_WIDTH, D), BRANCH_WIDTH ** -0.5),
        "w_out": nrm(ks[19], (L, D, D), D ** -0.5),
        "norm2_g": gain(ks[20], (L, D)),
        "ffn_w_gu": nrm(ks[21], (L, D, 2 * FFN_HIDDEN), D ** -0.5),
        "ffn_w_down": nrm(ks[22], (L, FFN_HIDDEN, D), FFN_HIDDEN ** -0.5),
        "final_g": gain(ks[23], (D,)),
    }


def _fwd_reference(x, norm1_g, w_in, ret_decay, ret_gn_g, mla_q_norm_g, mla_w_uq, mla_kv_norm_g, mla_w_ukv,
              s5_a_re, s5_a_im, s5_log_dt, s5_b_re, s5_b_im, s5_c_re, s5_c_im, s5_d, s5_w_glu,
              w_branch, w_out, norm2_g, ffn_w_gu, ffn_w_down, final_g):
    b, s, d = x.shape
    cos_r, sin_r = _rope_tables(s, RET_QK_DIM)
    cos_m, sin_m = _rope_tables(s, MLA_ROPE)
    split_at = list(np.cumsum(IN_SPLITS)[:-1])
    for l in range(DEPTH):
        h = _rmsnorm(x, norm1_g[l])
        (rq, rk, rv, rg, c_q, c_kv, k_rope, s5_u, gate_logits) = jnp.split(h @ w_in[l], split_at, axis=-1)
        y_ret = _retention_branch(rq, rk, rv, rg, ret_decay[l], ret_gn_g[l], cos_r, sin_r)
        y_mla = _mla_branch(c_q, c_kv, k_rope, mla_q_norm_g[l], mla_w_uq[l], mla_kv_norm_g[l], mla_w_ukv[l],
                            cos_m, sin_m)
        y_s5 = _s5_branch(s5_u, s5_a_re[l], s5_a_im[l], s5_log_dt[l], s5_b_re[l], s5_b_im[l],
                          s5_c_re[l], s5_c_im[l], s5_d[l], s5_w_glu[l])
        branches = jnp.stack([y_ret, y_mla, y_s5], axis=2)
        proj = jnp.einsum('bsnk,nkd->bsnd', branches, w_branch[l])
        gates = jax.nn.sigmoid(gate_logits.reshape(b, s, N_BRANCH, d))
        x = x + jnp.sum(gates * proj, axis=2) @ w_out[l]
        h2 = _rmsnorm(x, norm2_g[l])
        f_gate, f_up = jnp.split(h2 @ ffn_w_gu[l], 2, axis=-1)
        x = x + (jax.nn.silu(f_gate) * f_up) @ ffn_w_down[l]
    return _rmsnorm(x, final_g)


import jax as _jax
import jax.numpy as _jnp

TWIN_FORMAT = 'train_step'
FWD_PARAMS = ['x', 'norm1_g', 'w_in', 'ret_decay', 'ret_gn_g', 'mla_q_norm_g', 'mla_w_uq', 'mla_kv_norm_g', 'mla_w_ukv', 's5_a_re', 's5_a_im', 's5_log_dt', 's5_b_re', 's5_b_im', 's5_c_re', 's5_c_im', 's5_d', 's5_w_glu', 'w_branch', 'w_out', 'norm2_g', 'ffn_w_gu', 'ffn_w_down', 'final_g']
TWIN_WEIGHTS = ['norm1_g', 'w_in', 'ret_decay', 'ret_gn_g', 'mla_q_norm_g', 'mla_w_uq', 'mla_kv_norm_g', 'mla_w_ukv', 's5_a_re', 's5_a_im', 's5_log_dt', 's5_b_re', 's5_b_im', 's5_c_re', 's5_c_im', 's5_d', 's5_w_glu', 'w_branch', 'w_out', 'norm2_g', 'ffn_w_gu', 'ffn_w_down', 'final_g']
TWIN_DIFF_INPUT = 'x'
TWIN_INPUTS = ['x', 'norm1_g', 'w_in', 'ret_decay', 'ret_gn_g', 'mla_q_norm_g', 'mla_w_uq', 'mla_kv_norm_g', 'mla_w_ukv', 's5_a_re', 's5_a_im', 's5_log_dt', 's5_b_re', 's5_b_im', 's5_c_re', 's5_c_im', 's5_d', 's5_w_glu', 'w_branch', 'w_out', 'norm2_g', 'ffn_w_gu', 'ffn_w_down', 'final_g', 'loss_target', 'm_norm1_g', 'm_w_in', 'm_ret_decay', 'm_ret_gn_g', 'm_mla_q_norm_g', 'm_mla_w_uq', 'm_mla_kv_norm_g', 'm_mla_w_ukv', 'm_s5_a_re', 'm_s5_a_im', 'm_s5_log_dt', 'm_s5_b_re', 'm_s5_b_im', 'm_s5_c_re', 'm_s5_c_im', 'm_s5_d', 'm_s5_w_glu', 'm_w_branch', 'm_w_out', 'm_norm2_g', 'm_ffn_w_gu', 'm_ffn_w_down', 'm_final_g', 'v_norm1_g', 'v_w_in', 'v_ret_decay', 'v_ret_gn_g', 'v_mla_q_norm_g', 'v_mla_w_uq', 'v_mla_kv_norm_g', 'v_mla_w_ukv', 'v_s5_a_re', 'v_s5_a_im', 'v_s5_log_dt', 'v_s5_b_re', 'v_s5_b_im', 'v_s5_c_re', 'v_s5_c_im', 'v_s5_d', 'v_s5_w_glu', 'v_w_branch', 'v_w_out', 'v_norm2_g', 'v_ffn_w_gu', 'v_ffn_w_down', 'v_final_g']
TWIN_OUTPUTS = ['loss', 'grad_x', 'grad_norm1_g', 'grad_w_in', 'grad_ret_decay', 'grad_ret_gn_g', 'grad_mla_q_norm_g', 'grad_mla_w_uq', 'grad_mla_kv_norm_g', 'grad_mla_w_ukv', 'grad_s5_a_re', 'grad_s5_a_im', 'grad_s5_log_dt', 'grad_s5_b_re', 'grad_s5_b_im', 'grad_s5_c_re', 'grad_s5_c_im', 'grad_s5_d', 'grad_s5_w_glu', 'grad_w_branch', 'grad_w_out', 'grad_norm2_g', 'grad_ffn_w_gu', 'grad_ffn_w_down', 'grad_final_g', 'delta_norm1_g', 'delta_w_in', 'delta_ret_decay', 'delta_ret_gn_g', 'delta_mla_q_norm_g', 'delta_mla_w_uq', 'delta_mla_kv_norm_g', 'delta_mla_w_ukv', 'delta_s5_a_re', 'delta_s5_a_im', 'delta_s5_log_dt', 'delta_s5_b_re', 'delta_s5_b_im', 'delta_s5_c_re', 'delta_s5_c_im', 'delta_s5_d', 'delta_s5_w_glu', 'delta_w_branch', 'delta_w_out', 'delta_norm2_g', 'delta_ffn_w_gu', 'delta_ffn_w_down', 'delta_final_g', 'new_m_norm1_g', 'new_m_w_in', 'new_m_ret_decay', 'new_m_ret_gn_g', 'new_m_mla_q_norm_g', 'new_m_mla_w_uq', 'new_m_mla_kv_norm_g', 'new_m_mla_w_ukv', 'new_m_s5_a_re', 'new_m_s5_a_im', 'new_m_s5_log_dt', 'new_m_s5_b_re', 'new_m_s5_b_im', 'new_m_s5_c_re', 'new_m_s5_c_im', 'new_m_s5_d', 'new_m_s5_w_glu', 'new_m_w_branch', 'new_m_w_out', 'new_m_norm2_g', 'new_m_ffn_w_gu', 'new_m_ffn_w_down', 'new_m_final_g', 'new_v_norm1_g', 'new_v_w_in', 'new_v_ret_decay', 'new_v_ret_gn_g', 'new_v_mla_q_norm_g', 'new_v_mla_w_uq', 'new_v_mla_kv_norm_g', 'new_v_mla_w_ukv', 'new_v_s5_a_re', 'new_v_s5_a_im', 'new_v_s5_log_dt', 'new_v_s5_b_re', 'new_v_s5_b_im', 'new_v_s5_c_re', 'new_v_s5_c_im', 'new_v_s5_d', 'new_v_s5_w_glu', 'new_v_w_branch', 'new_v_w_out', 'new_v_norm2_g', 'new_v_ffn_w_gu', 'new_v_ffn_w_down', 'new_v_final_g']
TWIN_LEAF_KINDS = {'loss': 'loss', 'grad_x': 'grad_x', 'grad_norm1_g': 'grad_w', 'grad_w_in': 'grad_w', 'grad_ret_decay': 'grad_w', 'grad_ret_gn_g': 'grad_w', 'grad_mla_q_norm_g': 'grad_w', 'grad_mla_w_uq': 'grad_w', 'grad_mla_kv_norm_g': 'grad_w', 'grad_mla_w_ukv': 'grad_w', 'grad_s5_a_re': 'grad_w', 'grad_s5_a_im': 'grad_w', 'grad_s5_log_dt': 'grad_w', 'grad_s5_b_re': 'grad_w', 'grad_s5_b_im': 'grad_w', 'grad_s5_c_re': 'grad_w', 'grad_s5_c_im': 'grad_w', 'grad_s5_d': 'grad_w', 'grad_s5_w_glu': 'grad_w', 'grad_w_branch': 'grad_w', 'grad_w_out': 'grad_w', 'grad_norm2_g': 'grad_w', 'grad_ffn_w_gu': 'grad_w', 'grad_ffn_w_down': 'grad_w', 'grad_final_g': 'grad_w', 'delta_norm1_g': 'delta_w', 'delta_w_in': 'delta_w', 'delta_ret_decay': 'delta_w', 'delta_ret_gn_g': 'delta_w', 'delta_mla_q_norm_g': 'delta_w', 'delta_mla_w_uq': 'delta_w', 'delta_mla_kv_norm_g': 'delta_w', 'delta_mla_w_ukv': 'delta_w', 'delta_s5_a_re': 'delta_w', 'delta_s5_a_im': 'delta_w', 'delta_s5_log_dt': 'delta_w', 'delta_s5_b_re': 'delta_w', 'delta_s5_b_im': 'delta_w', 'delta_s5_c_re': 'delta_w', 'delta_s5_c_im': 'delta_w', 'delta_s5_d': 'delta_w', 'delta_s5_w_glu': 'delta_w', 'delta_w_branch': 'delta_w', 'delta_w_out': 'delta_w', 'delta_norm2_g': 'delta_w', 'delta_ffn_w_gu': 'delta_w', 'delta_ffn_w_down': 'delta_w', 'delta_final_g': 'delta_w', 'new_m_norm1_g': 'new_m', 'new_m_w_in': 'new_m', 'new_m_ret_decay': 'new_m', 'new_m_ret_gn_g': 'new_m', 'new_m_mla_q_norm_g': 'new_m', 'new_m_mla_w_uq': 'new_m', 'new_m_mla_kv_norm_g': 'new_m', 'new_m_mla_w_ukv': 'new_m', 'new_m_s5_a_re': 'new_m', 'new_m_s5_a_im': 'new_m', 'new_m_s5_log_dt': 'new_m', 'new_m_s5_b_re': 'new_m', 'new_m_s5_b_im': 'new_m', 'new_m_s5_c_re': 'new_m', 'new_m_s5_c_im': 'new_m', 'new_m_s5_d': 'new_m', 'new_m_s5_w_glu': 'new_m', 'new_m_w_branch': 'new_m', 'new_m_w_out': 'new_m', 'new_m_norm2_g': 'new_m', 'new_m_ffn_w_gu': 'new_m', 'new_m_ffn_w_down': 'new_m', 'new_m_final_g': 'new_m', 'new_v_norm1_g': 'new_v', 'new_v_w_in': 'new_v', 'new_v_ret_decay': 'new_v', 'new_v_ret_gn_g': 'new_v', 'new_v_mla_q_norm_g': 'new_v', 'new_v_mla_w_uq': 'new_v', 'new_v_mla_kv_norm_g': 'new_v', 'new_v_mla_w_ukv': 'new_v', 'new_v_s5_a_re': 'new_v', 'new_v_s5_a_im': 'new_v', 'new_v_s5_log_dt': 'new_v', 'new_v_s5_b_re': 'new_v', 'new_v_s5_b_im': 'new_v', 'new_v_s5_c_re': 'new_v', 'new_v_s5_c_im': 'new_v', 'new_v_s5_d': 'new_v', 'new_v_s5_w_glu': 'new_v', 'new_v_w_branch': 'new_v', 'new_v_w_out': 'new_v', 'new_v_norm2_g': 'new_v', 'new_v_ffn_w_gu': 'new_v', 'new_v_ffn_w_down': 'new_v', 'new_v_final_g': 'new_v'}


def _forward(args):
    return _fwd_reference(*[args[k] for k in FWD_PARAMS])


def _output_shape():
    def fwd():
        inp = _fwd_setup_inputs(0)
        return _fwd_reference(*[inp[k] for k in FWD_PARAMS])
    out = _jax.eval_shape(fwd)
    return out.shape, out.dtype

N_MICROBATCH = 1
ADAM_LR = 0.001
ADAM_B1 = 0.9
ADAM_B2 = 0.999
ADAM_EPS = 1e-08
ADAM_WD = 0.01
ADAM_STEP = 10
PER_EXAMPLE_BATCH_AXIS = {'x': 0, 'loss_target': 0}
SHARED_INPUTS = []
_WEIGHT_DTYPES = {'norm1_g': _jnp.float32, 'w_in': _jnp.float32, 'ret_decay': _jnp.float32, 'ret_gn_g': _jnp.float32, 'mla_q_norm_g': _jnp.float32, 'mla_w_uq': _jnp.float32, 'mla_kv_norm_g': _jnp.float32, 'mla_w_ukv': _jnp.float32, 's5_a_re': _jnp.float32, 's5_a_im': _jnp.float32, 's5_log_dt': _jnp.float32, 's5_b_re': _jnp.float32, 's5_b_im': _jnp.float32, 's5_c_re': _jnp.float32, 's5_c_im': _jnp.float32, 's5_d': _jnp.float32, 's5_w_glu': _jnp.float32, 'w_branch': _jnp.float32, 'w_out': _jnp.float32, 'norm2_g': _jnp.float32, 'ffn_w_gu': _jnp.float32, 'ffn_w_down': _jnp.float32, 'final_g': _jnp.float32}
MOMENT_SCALE = {'norm1_g': 1.982134e-01, 'w_in': 6.845055e-02, 'ret_decay': 8.068805e-01, 'ret_gn_g': 9.386594e-02, 'mla_q_norm_g': 2.707743e-02, 'mla_w_uq': 1.346234e-02, 'mla_kv_norm_g': 4.806167e-02, 'mla_w_ukv': 1.626608e-02, 's5_a_re': 2.940741e-03, 's5_a_im': 3.120256e-03, 's5_log_dt': 2.844788e+00, 's5_b_re': 1.823209e-03, 's5_b_im': 1.821734e-03, 's5_c_re': 3.624459e-03, 's5_c_im': 3.627721e-03, 's5_d': 5.631709e-02, 's5_w_glu': 3.845882e-02, 'w_branch': 5.875154e-02, 'w_out': 1.017171e-01, 'norm2_g': 1.905722e-01, 'ffn_w_gu': 7.810115e-02, 'ffn_w_down': 1.276847e-01, 'final_g': 6.392276e+01}


def _to_microbatches(a, axis):
    t = _jnp.moveaxis(a, axis, 0)
    t = t.reshape((N_MICROBATCH, t.shape[0] // N_MICROBATCH) + t.shape[1:])
    return _jnp.moveaxis(t, 1, axis + 1)


def setup_inputs(seed: int = 0) -> dict:
    inp = _fwd_setup_inputs(seed)
    key = _jax.random.fold_in(_jax.random.key(seed), 7919)
    shape, _ = _output_shape()
    out = dict(inp)
    out["loss_target"] = _jax.random.normal(_jax.random.fold_in(key, 0), shape, _jnp.float32)
    for i, name in enumerate(TWIN_WEIGHTS):
        w = inp[name].astype(_jnp.float32)
        if MOMENT_SCALE is None:
            s = _jnp.sqrt(_jnp.mean(_jnp.square(w)) + 1e-30)
        else:
            s = MOMENT_SCALE[name]
        km, kv = _jax.random.split(_jax.random.fold_in(key, i + 1))
        out[name] = w
        out["m_" + name] = s * _jax.random.normal(km, w.shape, _jnp.float32)
        out["v_" + name] = (s * s) * _jax.random.uniform(kv, w.shape, _jnp.float32, 0.5, 1.5)
    if N_MICROBATCH > 1:
        for name, axis in PER_EXAMPLE_BATCH_AXIS.items():
            out[name] = _to_microbatches(out[name], axis)
    return {'x': out['x'], 'norm1_g': out['norm1_g'], 'w_in': out['w_in'], 'ret_decay': out['ret_decay'], 'ret_gn_g': out['ret_gn_g'], 'mla_q_norm_g': out['mla_q_norm_g'], 'mla_w_uq': out['mla_w_uq'], 'mla_kv_norm_g': out['mla_kv_norm_g'], 'mla_w_ukv': out['mla_w_ukv'], 's5_a_re': out['s5_a_re'], 's5_a_im': out['s5_a_im'], 's5_log_dt': out['s5_log_dt'], 's5_b_re': out['s5_b_re'], 's5_b_im': out['s5_b_im'], 's5_c_re': out['s5_c_re'], 's5_c_im': out['s5_c_im'], 's5_d': out['s5_d'], 's5_w_glu': out['s5_w_glu'], 'w_branch': out['w_branch'], 'w_out': out['w_out'], 'norm2_g': out['norm2_g'], 'ffn_w_gu': out['ffn_w_gu'], 'ffn_w_down': out['ffn_w_down'], 'final_g': out['final_g'], 'loss_target': out['loss_target'], 'm_norm1_g': out['m_norm1_g'], 'm_w_in': out['m_w_in'], 'm_ret_decay': out['m_ret_decay'], 'm_ret_gn_g': out['m_ret_gn_g'], 'm_mla_q_norm_g': out['m_mla_q_norm_g'], 'm_mla_w_uq': out['m_mla_w_uq'], 'm_mla_kv_norm_g': out['m_mla_kv_norm_g'], 'm_mla_w_ukv': out['m_mla_w_ukv'], 'm_s5_a_re': out['m_s5_a_re'], 'm_s5_a_im': out['m_s5_a_im'], 'm_s5_log_dt': out['m_s5_log_dt'], 'm_s5_b_re': out['m_s5_b_re'], 'm_s5_b_im': out['m_s5_b_im'], 'm_s5_c_re': out['m_s5_c_re'], 'm_s5_c_im': out['m_s5_c_im'], 'm_s5_d': out['m_s5_d'], 'm_s5_w_glu': out['m_s5_w_glu'], 'm_w_branch': out['m_w_branch'], 'm_w_out': out['m_w_out'], 'm_norm2_g': out['m_norm2_g'], 'm_ffn_w_gu': out['m_ffn_w_gu'], 'm_ffn_w_down': out['m_ffn_w_down'], 'm_final_g': out['m_final_g'], 'v_norm1_g': out['v_norm1_g'], 'v_w_in': out['v_w_in'], 'v_ret_decay': out['v_ret_decay'], 'v_ret_gn_g': out['v_ret_gn_g'], 'v_mla_q_norm_g': out['v_mla_q_norm_g'], 'v_mla_w_uq': out['v_mla_w_uq'], 'v_mla_kv_norm_g': out['v_mla_kv_norm_g'], 'v_mla_w_ukv': out['v_mla_w_ukv'], 'v_s5_a_re': out['v_s5_a_re'], 'v_s5_a_im': out['v_s5_a_im'], 'v_s5_log_dt': out['v_s5_log_dt'], 'v_s5_b_re': out['v_s5_b_re'], 'v_s5_b_im': out['v_s5_b_im'], 'v_s5_c_re': out['v_s5_c_re'], 'v_s5_c_im': out['v_s5_c_im'], 'v_s5_d': out['v_s5_d'], 'v_s5_w_glu': out['v_s5_w_glu'], 'v_w_branch': out['v_w_branch'], 'v_w_out': out['v_w_out'], 'v_norm2_g': out['v_norm2_g'], 'v_ffn_w_gu': out['v_ffn_w_gu'], 'v_ffn_w_down': out['v_ffn_w_down'], 'v_final_g': out['v_final_g']}


def _loss(weights, diff, rest, loss_target):
    with _jax.named_scope("forward"):
        args = {**rest, TWIN_DIFF_INPUT: diff, **{k: w.astype(_WEIGHT_DTYPES[k]) for k, w in weights.items()}}
        y = _forward(args)
    with _jax.named_scope("loss_head"):
        err = _jnp.square(y.astype(_jnp.float32) - loss_target)
        return 0.5 * _jnp.sum(_jnp.mean(err, axis=-1)) if err.ndim else 0.5 * err


def _adamw(w, g, m, v):
    m = ADAM_B1 * m + (1.0 - ADAM_B1) * g
    v = ADAM_B2 * v + (1.0 - ADAM_B2) * _jnp.square(g)
    m_hat = m / (1.0 - ADAM_B1 ** ADAM_STEP)
    v_hat = v / (1.0 - ADAM_B2 ** ADAM_STEP)
    delta = -ADAM_LR * (m_hat / (_jnp.sqrt(v_hat) + ADAM_EPS) + ADAM_WD * w)
    return delta, m, v


def reference(x, norm1_g, w_in, ret_decay, ret_gn_g, mla_q_norm_g, mla_w_uq, mla_kv_norm_g, mla_w_ukv, s5_a_re, s5_a_im, s5_log_dt, s5_b_re, s5_b_im, s5_c_re, s5_c_im, s5_d, s5_w_glu, w_branch, w_out, norm2_g, ffn_w_gu, ffn_w_down, final_g, loss_target, m_norm1_g, m_w_in, m_ret_decay, m_ret_gn_g, m_mla_q_norm_g, m_mla_w_uq, m_mla_kv_norm_g, m_mla_w_ukv, m_s5_a_re, m_s5_a_im, m_s5_log_dt, m_s5_b_re, m_s5_b_im, m_s5_c_re, m_s5_c_im, m_s5_d, m_s5_w_glu, m_w_branch, m_w_out, m_norm2_g, m_ffn_w_gu, m_ffn_w_down, m_final_g, v_norm1_g, v_w_in, v_ret_decay, v_ret_gn_g, v_mla_q_norm_g, v_mla_w_uq, v_mla_kv_norm_g, v_mla_w_ukv, v_s5_a_re, v_s5_a_im, v_s5_log_dt, v_s5_b_re, v_s5_b_im, v_s5_c_re, v_s5_c_im, v_s5_d, v_s5_w_glu, v_w_branch, v_w_out, v_norm2_g, v_ffn_w_gu, v_ffn_w_down, v_final_g):
    given = dict(x=x, norm1_g=norm1_g, w_in=w_in, ret_decay=ret_decay, ret_gn_g=ret_gn_g, mla_q_norm_g=mla_q_norm_g, mla_w_uq=mla_w_uq, mla_kv_norm_g=mla_kv_norm_g, mla_w_ukv=mla_w_ukv, s5_a_re=s5_a_re, s5_a_im=s5_a_im, s5_log_dt=s5_log_dt, s5_b_re=s5_b_re, s5_b_im=s5_b_im, s5_c_re=s5_c_re, s5_c_im=s5_c_im, s5_d=s5_d, s5_w_glu=s5_w_glu, w_branch=w_branch, w_out=w_out, norm2_g=norm2_g, ffn_w_gu=ffn_w_gu, ffn_w_down=ffn_w_down, final_g=final_g, loss_target=loss_target, m_norm1_g=m_norm1_g, m_w_in=m_w_in, m_ret_decay=m_ret_decay, m_ret_gn_g=m_ret_gn_g, m_mla_q_norm_g=m_mla_q_norm_g, m_mla_w_uq=m_mla_w_uq, m_mla_kv_norm_g=m_mla_kv_norm_g, m_mla_w_ukv=m_mla_w_ukv, m_s5_a_re=m_s5_a_re, m_s5_a_im=m_s5_a_im, m_s5_log_dt=m_s5_log_dt, m_s5_b_re=m_s5_b_re, m_s5_b_im=m_s5_b_im, m_s5_c_re=m_s5_c_re, m_s5_c_im=m_s5_c_im, m_s5_d=m_s5_d, m_s5_w_glu=m_s5_w_glu, m_w_branch=m_w_branch, m_w_out=m_w_out, m_norm2_g=m_norm2_g, m_ffn_w_gu=m_ffn_w_gu, m_ffn_w_down=m_ffn_w_down, m_final_g=m_final_g, v_norm1_g=v_norm1_g, v_w_in=v_w_in, v_ret_decay=v_ret_decay, v_ret_gn_g=v_ret_gn_g, v_mla_q_norm_g=v_mla_q_norm_g, v_mla_w_uq=v_mla_w_uq, v_mla_kv_norm_g=v_mla_kv_norm_g, v_mla_w_ukv=v_mla_w_ukv, v_s5_a_re=v_s5_a_re, v_s5_a_im=v_s5_a_im, v_s5_log_dt=v_s5_log_dt, v_s5_b_re=v_s5_b_re, v_s5_b_im=v_s5_b_im, v_s5_c_re=v_s5_c_re, v_s5_c_im=v_s5_c_im, v_s5_d=v_s5_d, v_s5_w_glu=v_s5_w_glu, v_w_branch=v_w_branch, v_w_out=v_w_out, v_norm2_g=v_norm2_g, v_ffn_w_gu=v_ffn_w_gu, v_ffn_w_down=v_ffn_w_down, v_final_g=v_final_g)
    weights = {n: given[n] for n in TWIN_WEIGHTS}
    shared = {n: given[n] for n in SHARED_INPUTS}
    per_example = {n: given[n] for n in ['x']}
    grad_fn = _jax.value_and_grad(_loss, argnums=(0, 1))

    def one_microbatch(ex, loss_target):
        ex = dict(ex)
        diff = ex.pop(TWIN_DIFF_INPUT)
        return grad_fn(weights, diff, {**shared, **ex}, loss_target)

    if N_MICROBATCH == 1:
        loss, (grad_w, grad_x) = one_microbatch(per_example, given["loss_target"])
    else:
        def body(carry, xs):
            loss_sum, grad_sum = carry
            l_k, (gw_k, gx_k) = one_microbatch(xs[0], xs[1])
            with _jax.named_scope("update"):
                return (loss_sum + l_k, _jax.tree.map(_jnp.add, grad_sum, gw_k)), gx_k

        init = (_jnp.zeros((), _jnp.float32), _jax.tree.map(_jnp.zeros_like, weights))
        (loss, grad_w), grad_x = _jax.lax.scan(body, init, (per_example, given["loss_target"]))
    with _jax.named_scope("update"):
        delta_w, new_m, new_v = {}, {}, {}
        for n in TWIN_WEIGHTS:
            delta_w[n], new_m[n], new_v[n] = _adamw(weights[n], grad_w[n], given["m_" + n], given["v_" + n])
    return (loss, grad_x, *[grad_w[n] for n in TWIN_WEIGHTS], *[delta_w[n] for n in TWIN_WEIGHTS],
            *[new_m[n] for n in TWIN_WEIGHTS], *[new_v[n] for n in TWIN_WEIGHTS])
```

```python
import functools
import math

import jax
import jax.numpy as jnp
from jax import lax
from jax.experimental import pallas as pl
from jax.experimental.pallas import tpu as pltpu

F32 = jnp.float32
BF16 = jnp.bfloat16

D_MODEL = 1024
DEPTH = 2
RMS_EPS = 1e-6
GN_EPS = 1e-5
ROPE_THETA = 10000.0
RET_HEADS, RET_QK, RET_V, RET_CHUNK = 4, 128, 256, 128
MLA_HEADS, MLA_Q_LORA, MLA_KV_LORA, MLA_NOPE, MLA_ROPE, MLA_V = 8, 384, 256, 128, 64, 128
S5_GROUPS, S5_GROUP, S5_STATE = 64, 16, 64
S5_BLOCKS = 8
FFN_HIDDEN = 2816
N_DEV = 8
ADAM_LR, ADAM_B1, ADAM_B2, ADAM_EPS, ADAM_WD, ADAM_STEP = 0.001, 0.9, 0.999, 1e-08, 0.01, 10

LANES = 128
SUBLANES = 8
VMEM_LIMIT = 48 * 1024 * 1024

ZW = 8192
Z_RET = 0
Z_RG = 2048
Z_U = 3072
Z_GATE = 4096
Z_CQ = 7168
Z_CKV = 7680
Z_KR = 7936
IN_SPLITS = (512, 512, 1024, 1024, 384, 256, 64, 1024, 3072)

SHARDED = ("w_in", "mla_w_uq", "mla_w_ukv", "s5_w_glu", "w_branch", "w_out", "ffn_w_gu", "ffn_w_down")
SHARD_AXIS = {"w_in": 2, "mla_w_uq": 2, "mla_w_ukv": 2, "s5_w_glu": 2, "w_branch": 2, "w_out": 1,
              "ffn_w_gu": 2, "ffn_w_down": 1}
SMALL = ("norm1_g", "ret_decay", "ret_gn_g", "mla_q_norm_g", "mla_kv_norm_g", "s5_a_re", "s5_a_im", "s5_log_dt",
         "s5_b_re", "s5_b_im", "s5_c_re", "s5_c_im", "s5_d", "norm2_g", "final_g")
WEIGHTS = ("norm1_g", "w_in", "ret_decay", "ret_gn_g", "mla_q_norm_g", "mla_w_uq", "mla_kv_norm_g", "mla_w_ukv",
           "s5_a_re", "s5_a_im", "s5_log_dt", "s5_b_re", "s5_b_im", "s5_c_re", "s5_c_im", "s5_d", "s5_w_glu",
           "w_branch", "w_out", "norm2_g", "ffn_w_gu", "ffn_w_down", "final_g")


def _params(sem=None):
    return pltpu.CompilerParams(dimension_semantics=sem, vmem_limit_bytes=VMEM_LIMIT)


def _pick(n, cap):
    if n <= cap:
        return n
    t = cap - cap % LANES
    while t >= LANES:
        if n % t == 0:
            return t
        t -= LANES
    return n


@functools.partial(jax.custom_vjp, nondiff_argnums=(2, 3))
def _bdot(a, b, ca, cb):
    return lax.dot_general(a.astype(BF16), b.astype(BF16), (((ca,), (cb,)), ((), ())), preferred_element_type=F32)


def _bdot_fwd(a, b, ca, cb):
    return _bdot(a, b, ca, cb), (a, b)


def _bdot_bwd(ca, cb, res, g):
    a, b = res
    da = _bdot(g, b, 1, 1 - cb) if ca == 1 else _bdot(b, g, 1 - cb, 1)
    db = _bdot(a, g, 1 - ca, 0) if cb == 0 else _bdot(g, a, 0, 1 - ca)
    return da, db


_bdot.defvjp(_bdot_fwd, _bdot_bwd)


@jax.custom_vjp
def _swap_halves(x):
    return pltpu.roll(x, LANES // 2, 1)


_swap_halves.defvjp(lambda x: (_swap_halves(x), None), lambda _, g: (_swap_halves(g),))


def _rope(x, cosf, sinf):
    return x * cosf + _swap_halves(x) * sinf


def _f_rms(x, g):
    return x * lax.rsqrt(jnp.mean(x * x, axis=-1, keepdims=True) + RMS_EPS) * g


def _rope_t(g, cosf, sinf):
    return g * cosf + _swap_halves(g * sinf)


def _f_gn_gate(yh, rgh, gh):
    mu = jnp.mean(yh, axis=-1, keepdims=True)
    var = jnp.mean(jnp.square(yh - mu), axis=-1, keepdims=True)
    return jax.nn.silu(rgh) * ((yh - mu) * lax.rsqrt(var + GN_EPS) * gh)


MLA_SCALE = (MLA_NOPE + MLA_ROPE) ** -0.5


def _mla_q(qraw, cosf, sinf, transpose):
    parts = []
    for h in range(MLA_HEADS):
        parts.append(qraw[:, 256 * h:256 * h + 128] * MLA_SCALE)
        r = qraw[:, 256 * h + 128:256 * h + 256]
        parts.append(_rope_t(r * MLA_SCALE, cosf, sinf) if transpose else _rope(r, cosf, sinf) * MLA_SCALE)
    return jnp.concatenate(parts, axis=1)


def _f_s5_act(ysum, u, d):
    return jax.nn.gelu(ysum + d * u)


def _f_glu(ga, gb):
    return ga * jax.nn.sigmoid(gb)


def _f_gate(zg, proj):
    return jax.nn.sigmoid(zg) * proj


def _f_swiglu(gp, up):
    return jax.nn.silu(gp) * up


def _f_loss(x, g, target):
    y = _f_rms(x, g)
    err = jnp.square(y - target)
    return 0.5 * jnp.sum(jnp.mean(err, axis=-1))


def _tile_call(name, fn, n_rows, tm, row_ins, consts, row_outs, acc_outs=(), alias=None):
    n_row_in, n_const = len(row_ins), len(consts)
    args = [a for a, _, _ in row_ins] + list(consts)
    in_specs = [pl.BlockSpec((tm, w), lambda i, cb=cb: (i, cb)) for _, w, cb in row_ins]
    in_specs += [pl.BlockSpec(c.shape, lambda i: (0, 0)) for c in consts]
    out_shape, out_specs, aliases = [], [], {}
    if alias is not None:
        arr, w, cb = alias
        in_specs.append(pl.BlockSpec((tm, w), lambda i, cb=cb: (i, cb)))
        aliases[len(args)] = 0
        args.append(arr)
        out_shape.append(jax.ShapeDtypeStruct(arr.shape, arr.dtype))
        out_specs.append(pl.BlockSpec((tm, w), lambda i, cb=cb: (i, cb)))
    for w, dt in row_outs:
        out_shape.append(jax.ShapeDtypeStruct((n_rows, w), dt))
        out_specs.append(pl.BlockSpec((tm, w), lambda i: (i, 0)))
    n_row_out = len(out_shape)
    for r, w in acc_outs:
        out_shape.append(jax.ShapeDtypeStruct((r, w), F32))
        out_specs.append(pl.BlockSpec((r, w), lambda i: (0, 0)))
    n_in = len(args)

    def body(*refs):
        ins, outs = refs[:n_in], refs[n_in:]
        rows = [r[...] for r in ins[:n_row_in]]
        cons = [r[...] for r in ins[n_row_in:n_row_in + n_const]]
        prev = ins[-1][...] if alias is not None else None
        res_rows, res_accs = fn(rows, cons, prev)
        for o, r in zip(outs[:n_row_out], res_rows):
            o[...] = r.astype(o.dtype)
        if acc_outs:
            @pl.when(pl.program_id(0) == 0)
            def _():
                for o in outs[n_row_out:]:
                    o[...] = jnp.zeros(o.shape, F32)
            for o, r in zip(outs[n_row_out:], res_accs):
                o[...] += r

    return pl.pallas_call(
        body, name=name, grid=(n_rows // tm,), in_specs=in_specs, out_specs=out_specs, out_shape=out_shape,
        input_output_aliases=aliases, compiler_params=_params(("arbitrary",)))(*args)


def _mm(name, a, b, *, ta=False, tb=False, add=None, out_dtype=F32):
    (K, M) = a.shape if ta else a.shape[::-1]
    (N, K2) = b.shape if tb else b.shape[::-1]
    assert K == K2, (name, a.shape, b.shape)
    tm, tn, tk = _pick(M, 512), _pick(N, 1536), _pick(K, 1024)
    nk = K // tk
    a_spec = pl.BlockSpec((tk, tm), lambda i, j, k: (k, i)) if ta else pl.BlockSpec((tm, tk), lambda i, j, k: (i, k))
    b_spec = pl.BlockSpec((tn, tk), lambda i, j, k: (j, k)) if tb else pl.BlockSpec((tk, tn), lambda i, j, k: (k, j))
    o_spec = pl.BlockSpec((tm, tn), lambda i, j, k: (i, j))
    dn = (((0 if ta else 1,), (1 if tb else 0,)), ((), ()))
    has_add = add is not None

    def body(*refs):
        if has_add:
            a_ref, b_ref, add_ref, o_ref, acc = refs
        else:
            a_ref, b_ref, o_ref, acc = refs
        k = pl.program_id(2)

        @pl.when(k == 0)
        def _():
            acc[...] = jnp.zeros(acc.shape, F32)

        acc[...] += lax.dot_general(a_ref[...].astype(BF16), b_ref[...].astype(BF16), dn, preferred_element_type=F32)

        @pl.when(k == nk - 1)
        def _():
            r = acc[...]
            if has_add:
                r = r + add_ref[...]
            o_ref[...] = r.astype(out_dtype)

    args, specs = [a, b], [a_spec, b_spec]
    if has_add:
        args.append(add)
        specs.append(o_spec)
    return pl.pallas_call(
        body, name=name, grid=(M // tm, N // tn, nk), in_specs=specs, out_specs=o_spec,
        out_shape=jax.ShapeDtypeStruct((M, N), out_dtype), scratch_shapes=[pltpu.VMEM((tm, tn), F32)],
        compiler_params=_params(("parallel", "parallel", "arbitrary")))(*args)


def _bd_mm(name, pairs, n_rows, *, add=None, alias=None):
    kk, nn = pairs[0][2].shape[1:]
    tm = _pick(n_rows, 512)
    args, specs = [], []
    for x, off, w in pairs:
        args += [x, w]
        specs += [pl.BlockSpec((tm, kk), lambda i, j, off=off: (i, off + j)),
                  pl.BlockSpec((None, kk, nn), lambda i, j: (j, 0, 0))]
    n_pairs = len(pairs)
    has_add = add is not None
    if has_add:
        args.append(add)
        specs.append(pl.BlockSpec((tm, nn), lambda i, j: (i, j)))
    aliases = {}
    if alias is not None:
        arr, ooff = alias
        aliases[len(args)] = 0
        args.append(arr)
        specs.append(pl.BlockSpec(memory_space=pl.ANY))
        out_shape = jax.ShapeDtypeStruct(arr.shape, arr.dtype)
    else:
        ooff = 0
        out_shape = jax.ShapeDtypeStruct((n_rows, S5_BLOCKS * nn), F32)
    o_spec = pl.BlockSpec((tm, nn), lambda i, j: (i, ooff + j))

    def body(*refs):
        o_ref = refs[-1]
        r = None
        for p in range(n_pairs):
            t = jnp.dot(refs[2 * p][...].astype(BF16), refs[2 * p + 1][...].astype(BF16), preferred_element_type=F32)
            r = t if r is None else r + t
        if has_add:
            r = r + refs[2 * n_pairs][...]
        o_ref[...] = r

    return pl.pallas_call(
        body, name=name, grid=(n_rows // tm, S5_BLOCKS), in_specs=specs, out_specs=o_spec, out_shape=out_shape,
        input_output_aliases=aliases, compiler_params=_params(("parallel", "parallel")))(*args)


def _bd_mm_tn(name, x, xoff, kk, g, goff, nn, n_rows):
    tm = _pick(n_rows, 512)

    def body(x_ref, g_ref, o_ref):
        @pl.when(pl.program_id(1) == 0)
        def _():
            o_ref[...] = jnp.zeros(o_ref.shape, F32)

        o_ref[...] += lax.dot_general(x_ref[...].astype(BF16), g_ref[...].astype(BF16), (((0,), (0,)), ((), ())),
                                      preferred_element_type=F32)

    return pl.pallas_call(
        body, name=name, grid=(S5_BLOCKS, n_rows // tm),
        in_specs=[pl.BlockSpec((tm, kk), lambda j, i: (i, xoff + j)), pl.BlockSpec((tm, nn), lambda j, i: (i, goff + j))],
        out_specs=pl.BlockSpec((None, kk, nn), lambda j, i: (j, 0, 0)),
        out_shape=jax.ShapeDtypeStruct((S5_BLOCKS, kk, nn), F32),
        compiler_params=_params(("parallel", "arbitrary")))(x, g)


def _scan(name, b, ak, apow, *, reverse, xs=None):
    n_rows, width = b.shape
    bw = width // S5_BLOCKS
    hw = bw // 2
    tt = _pick(n_rows, 256)
    nt = n_rows // tt
    ng = tt // SUBLANES
    with_da = xs is not None
    tmap = (lambda j, t: (nt - 1 - t, j)) if reverse else (lambda j, t: (t, j))

    def body(*refs):
        if with_da:
            b_ref, ak_ref, ap_ref, x_ref, o_ref, da_ref, carry = refs
        else:
            b_ref, ak_ref, ap_ref, o_ref, carry = refs
        t = pl.program_id(1)

        @pl.when(t == 0)
        def _():
            carry[...] = jnp.zeros(carry.shape, F32)
            if with_da:
                da_ref[...] = jnp.zeros(da_ref.shape, F32)

        rowid = lax.broadcasted_iota(jnp.int32, (SUBLANES, hw), 0)
        steps = []
        for n, k in enumerate((1, 2, 4)):
            valid = (rowid < SUBLANES - k) if reverse else (rowid >= k)
            steps.append((SUBLANES - k if reverse else k, valid, ak_ref[n:n + 1, :hw], ak_ref[n:n + 1, hw:]))
        apr, api = ap_ref[:, :hw], ap_ref[:, hw:]
        first = (rowid == SUBLANES - 1) if reverse else (rowid == 0)

        def group(gi, c):
            cr, ci = c
            r0 = pl.multiple_of(((ng - 1 - gi) if reverse else gi) * SUBLANES, SUBLANES)
            xr, xi = b_ref[pl.ds(r0, SUBLANES), :hw], b_ref[pl.ds(r0, SUBLANES), hw:]
            for sh, valid, kr, ki in steps:
                sr = jnp.where(valid, pltpu.roll(xr, sh, 0), 0.0)
                si = jnp.where(valid, pltpu.roll(xi, sh, 0), 0.0)
                xr, xi = xr + kr * sr - ki * si, xi + kr * si + ki * sr
            xr, xi = xr + apr * cr - api * ci, xi + apr * ci + api * cr
            o_ref[pl.ds(r0, SUBLANES), :hw] = xr
            o_ref[pl.ds(r0, SUBLANES), hw:] = xi
            if with_da:
                sh1 = SUBLANES - 1 if reverse else 1
                pr = jnp.where(first, cr, pltpu.roll(xr, sh1, 0))
                pi = jnp.where(first, ci, pltpu.roll(xi, sh1, 0))
                sr, si = x_ref[pl.ds(r0, SUBLANES), :hw], x_ref[pl.ds(r0, SUBLANES), hw:]
                da_ref[:, :hw] += pr * sr + pi * si
                da_ref[:, hw:] += pi * sr - pr * si
            last = 0 if reverse else SUBLANES - 1
            return (jnp.broadcast_to(xr[last:last + 1], (SUBLANES, hw)), jnp.broadcast_to(xi[last:last + 1], (SUBLANES, hw)))

        cr, ci = lax.fori_loop(0, ng, group, (carry[:, :hw], carry[:, hw:]))
        carry[:, :hw] = cr
        carry[:, hw:] = ci

        if with_da:
            @pl.when(t == nt - 1)
            def _():
                da_ref[...] = jnp.broadcast_to(jnp.sum(da_ref[...], axis=0, keepdims=True), da_ref.shape)

    row_spec = pl.BlockSpec((tt, bw), tmap)
    par_spec = pl.BlockSpec((SUBLANES, bw), lambda j, t: (0, j))
    args, specs = [b, ak, apow], [row_spec, par_spec, par_spec]
    out_shape, out_specs = [jax.ShapeDtypeStruct(b.shape, F32)], [row_spec]
    if with_da:
        args.append(xs)
        specs.append(row_spec)
        out_shape.append(jax.ShapeDtypeStruct((SUBLANES, width), F32))
        out_specs.append(par_spec)
    res = pl.pallas_call(
        body, name=name, grid=(S5_BLOCKS, nt), in_specs=specs, out_specs=out_specs, out_shape=out_shape,
        scratch_shapes=[pltpu.VMEM((SUBLANES, bw), F32)], compiler_params=_params(("parallel", "arbitrary")))(*args)
    return res if with_da else res[0]


def _ret_chunk(zq, zk, v, cosf, sinf, state, rd, reverse):
    c = RET_CHUNK
    lg = jax.nn.log_sigmoid(rd)
    lg1 = jnp.max(lg, axis=1, keepdims=True)
    q = _rope(zq, cosf, sinf) * (RET_QK ** -0.5)
    k = _rope(zk, cosf, sinf)
    pi = lax.broadcasted_iota(jnp.int32, (c, c), 0).astype(F32)
    pj = lax.broadcasted_iota(jnp.int32, (c, c), 1).astype(F32)
    pcol = lax.broadcasted_iota(jnp.int32, (c, 1), 0).astype(F32)
    if reverse:
        diff, mask, pos = pj - pi, pj > pi, (c - 1) - pcol
    else:
        diff, mask, pos = pi - pj, pi >= pj, pcol
    decay_in = jnp.where(mask, jnp.exp(jnp.where(mask, diff, 0.0) * lg), 0.0)
    scores = _bdot(q, k, 1, 1) * decay_in
    inner = _bdot(scores, v, 1, 0)
    k_w = jnp.exp((c - 1 - pos) * lg1)
    kv = _bdot(k * k_w, v, 0, 0)
    q_w = jnp.exp((pos + 1) * lg1)
    cross = _bdot(q, state, 1, 0) * q_w
    new_state = jnp.exp(c * lg1) * state + kv
    return inner + cross, new_state


def _ret_specs(n_chunks, reverse_order):
    cmap = (lambda h, n: n_chunks - 1 - n) if reverse_order else (lambda h, n: n)
    z_spec = pl.BlockSpec((RET_CHUNK, 512), lambda h, n: (cmap(h, n), h))
    t_spec = pl.BlockSpec((RET_CHUNK, LANES), lambda h, n: (cmap(h, n), 0))
    rd_spec = pl.BlockSpec((None, 1, LANES), lambda h, n: (h, 0, 0))
    o_spec = pl.BlockSpec((RET_CHUNK, RET_V), lambda h, n: (cmap(h, n), h))
    st_spec = pl.BlockSpec((None, None, RET_QK, RET_V), lambda h, n: (h, cmap(h, n), 0, 0))
    return z_spec, t_spec, rd_spec, o_spec, st_spec


def _ret_fwd(name, z, cosf, sinf, rd, *, reverse):
    n_rows = z.shape[0]
    n_chunks = n_rows // RET_CHUNK
    z_spec, t_spec, rd_spec, o_spec, st_spec = _ret_specs(n_chunks, reverse)

    def body(z_ref, cos_ref, sin_ref, rd_ref, o_ref, st_ref, state):
        @pl.when(pl.program_id(1) == 0)
        def _():
            state[...] = jnp.zeros(state.shape, F32)

        st = state[...]
        st_ref[...] = st
        zt = z_ref[...]
        out, new = _ret_chunk(zt[:, :128], zt[:, 128:256], zt[:, 256:], cos_ref[...], sin_ref[...], st, rd_ref[...], reverse)
        o_ref[...] = out
        state[...] = new

    return pl.pallas_call(
        body, name=name, grid=(RET_HEADS, n_chunks), in_specs=[z_spec, t_spec, t_spec, rd_spec],
        out_specs=[o_spec, st_spec],
        out_shape=[jax.ShapeDtypeStruct((n_rows, RET_HEADS * RET_V), F32),
                   jax.ShapeDtypeStruct((RET_HEADS, n_chunks, RET_QK, RET_V), F32)],
        scratch_shapes=[pltpu.VMEM((RET_QK, RET_V), F32)], compiler_params=_params(("parallel", "arbitrary")))(z, cosf, sinf, rd)


def _ret_bwd(name, z, cosf, sinf, rd, states, dout, dz, *, reverse):
    n_rows = z.shape[0]
    n_chunks = n_rows // RET_CHUNK
    z_spec, t_spec, rd_spec, o_spec, st_spec = _ret_specs(n_chunks, not reverse)

    def body(z_ref, cos_ref, sin_ref, rd_ref, st_ref, do_ref, dzin_ref, dz_ref, drd_ref, dstate):
        @pl.when(pl.program_id(1) == 0)
        def _():
            dstate[...] = jnp.zeros(dstate.shape, F32)
            drd_ref[...] = jnp.zeros(drd_ref.shape, F32)

        zt = z_ref[...]
        cosv, sinv = cos_ref[...], sin_ref[...]
        _, vjp = jax.vjp(lambda a, b, c, s, r: _ret_chunk(a, b, c, cosv, sinv, s, r, reverse),
                         zt[:, :128], zt[:, 128:256], zt[:, 256:], st_ref[...], rd_ref[...])
        dq, dk, dv, dst, drd = vjp((do_ref[...], dstate[...]))
        dz_ref[...] = dzin_ref[...] + jnp.concatenate([dq, dk, dv], axis=1)
        dstate[...] = dst
        drd_ref[...] += jnp.sum(drd, axis=1, keepdims=True)

    return pl.pallas_call(
        body, name=name, grid=(RET_HEADS, n_chunks),
        in_specs=[z_spec, t_spec, t_spec, rd_spec, st_spec, o_spec, z_spec],
        out_specs=[z_spec, rd_spec],
        out_shape=[jax.ShapeDtypeStruct(dz.shape, F32), jax.ShapeDtypeStruct((RET_HEADS, 1, LANES), F32)],
        input_output_aliases={6: 0}, scratch_shapes=[pltpu.VMEM((RET_QK, RET_V), F32)],
        compiler_params=_params(("parallel", "arbitrary")))(z, cosf, sinf, rd, states, dout, dz)


_NT = (((1,), (1,)), ((), ()))
_TN = (((0,), (0,)), ((), ()))


def _attn_tiles(n_rows):
    return _pick(n_rows, 512), _pick(n_rows, 512)


def _attn_fwd(name, q, kv, kr):
    n_rows = q.shape[0]
    tq, tk = _attn_tiles(n_rows)
    nk = n_rows // tk

    def body(q_ref, kn_ref, v_ref, kr_ref, o_ref, lse_ref, m_sc, l_sc, acc):
        j = pl.program_id(2)

        @pl.when(j == 0)
        def _():
            m_sc[...] = jnp.full(m_sc.shape, -jnp.inf, F32)
            l_sc[...] = jnp.zeros(l_sc.shape, F32)
            acc[...] = jnp.zeros(acc.shape, F32)

        k = jnp.concatenate([kn_ref[...], kr_ref[...]], axis=1)
        s = lax.dot_general(q_ref[...], k, _NT, preferred_element_type=F32)
        m_prev = m_sc[...]
        m_new = jnp.maximum(m_prev, jnp.max(s, axis=1, keepdims=True))
        p = jnp.exp(s - m_new)
        alpha = jnp.exp(m_prev - m_new)
        l_sc[...] = alpha * l_sc[...] + jnp.sum(p, axis=1, keepdims=True)
        acc[...] = alpha * acc[...] + jnp.dot(p.astype(BF16), v_ref[...], preferred_element_type=F32)
        m_sc[...] = m_new

        @pl.when(j == nk - 1)
        def _():
            o_ref[...] = acc[...] / l_sc[...]
            lse_ref[...] = jnp.broadcast_to(m_sc[...] + jnp.log(l_sc[...]), lse_ref.shape)

    return pl.pallas_call(
        body, name=name, grid=(MLA_HEADS, n_rows // tq, nk),
        in_specs=[pl.BlockSpec((tq, 256), lambda h, i, j: (i, h)),
                  pl.BlockSpec((tk, 128), lambda h, i, j: (j, 2 * h)),
                  pl.BlockSpec((tk, 128), lambda h, i, j: (j, 2 * h + 1)),
                  pl.BlockSpec((tk, 128), lambda h, i, j: (j, 0))],
        out_specs=[pl.BlockSpec((tq, 128), lambda h, i, j: (i, h)),
                   pl.BlockSpec((None, tq, 128), lambda h, i, j: (h, i, 0))],
        out_shape=[jax.ShapeDtypeStruct((n_rows, MLA_HEADS * MLA_V), F32),
                   jax.ShapeDtypeStruct((MLA_HEADS, n_rows, LANES), F32)],
        scratch_shapes=[pltpu.VMEM((tq, 1), F32), pltpu.VMEM((tq, 1), F32), pltpu.VMEM((tq, 128), F32)],
        compiler_params=_params(("parallel", "parallel", "arbitrary")))(q, kv, kv, kr)


def _attn_probs(q_ref, kn_ref, kr_ref, v_ref, o_ref, do_ref, lse_ref):
    k = jnp.concatenate([kn_ref[...], kr_ref[...]], axis=1)
    q = q_ref[...]
    s = lax.dot_general(q, k, _NT, preferred_element_type=F32)
    p = jnp.exp(s - lse_ref[...][:, :1])
    do = do_ref[...]
    dp = lax.dot_general(do.astype(BF16), v_ref[...], _NT, preferred_element_type=F32)
    delta = jnp.sum(do * o_ref[...], axis=1, keepdims=True)
    ds = p * (dp - delta)
    return q, k, p, ds, do


def _attn_bwd_dq(name, q, kv, kr, o, lse, do):
    n_rows = q.shape[0]
    tq, tk = _attn_tiles(n_rows)
    nk = n_rows // tk

    def body(q_ref, kn_ref, v_ref, kr_ref, o_ref, lse_ref, do_ref, dq_ref, acc):
        j = pl.program_id(2)

        @pl.when(j == 0)
        def _():
            acc[...] = jnp.zeros(acc.shape, F32)

        _, k, _, ds, _ = _attn_probs(q_ref, kn_ref, kr_ref, v_ref, o_ref, do_ref, lse_ref)
        acc[...] += jnp.dot(ds.astype(BF16), k, preferred_element_type=F32)

        @pl.when(j == nk - 1)
        def _():
            dq_ref[...] = acc[...]

    return pl.pallas_call(
        body, name=name, grid=(MLA_HEADS, n_rows // tq, nk),
        in_specs=[pl.BlockSpec((tq, 256), lambda h, i, j: (i, h)),
                  pl.BlockSpec((tk, 128), lambda h, i, j: (j, 2 * h)),
                  pl.BlockSpec((tk, 128), lambda h, i, j: (j, 2 * h + 1)),
                  pl.BlockSpec((tk, 128), lambda h, i, j: (j, 0)),
                  pl.BlockSpec((tq, 128), lambda h, i, j: (i, h)),
                  pl.BlockSpec((None, tq, 128), lambda h, i, j: (h, i, 0)),
                  pl.BlockSpec((tq, 128), lambda h, i, j: (i, h))],
        out_specs=pl.BlockSpec((tq, 256), lambda h, i, j: (i, h)),
        out_shape=jax.ShapeDtypeStruct((n_rows, MLA_HEADS * 256), F32),
        scratch_shapes=[pltpu.VMEM((tq, 256), F32)],
        compiler_params=_params(("parallel", "parallel", "arbitrary")))(q, kv, kv, kr, o, lse, do)


def _attn_bwd_dkv(name, q, kv, kr, o, lse, do):
    n_rows = q.shape[0]
    tq, tk = _attn_tiles(n_rows)
    nq = n_rows // tq

    def body(q_ref, kn_ref, v_ref, kr_ref, o_ref, lse_ref, do_ref, dkv_ref, dkr_ref, dk_acc, dv_acc):
        h, i = pl.program_id(1), pl.program_id(2)

        @pl.when(i == 0)
        def _():
            dk_acc[...] = jnp.zeros(dk_acc.shape, F32)
            dv_acc[...] = jnp.zeros(dv_acc.shape, F32)

        @pl.when((i == 0) & (h == 0))
        def _():
            dkr_ref[...] = jnp.zeros(dkr_ref.shape, F32)

        q, _, p, ds, do = _attn_probs(q_ref, kn_ref, kr_ref, v_ref, o_ref, do_ref, lse_ref)
        dv_acc[...] += lax.dot_general(p.astype(BF16), do.astype(BF16), _TN, preferred_element_type=F32)
        dk_acc[...] += lax.dot_general(ds.astype(BF16), q, _TN, preferred_element_type=F32)

        @pl.when(i == nq - 1)
        def _():
            dkv_ref[...] = jnp.concatenate([dk_acc[:, :128], dv_acc[...]], axis=1)
            dkr_ref[...] += dk_acc[:, 128:]

    return pl.pallas_call(
        body, name=name, grid=(n_rows // tk, MLA_HEADS, nq),
        in_specs=[pl.BlockSpec((tq, 256), lambda j, h, i: (i, h)),
                  pl.BlockSpec((tk, 128), lambda j, h, i: (j, 2 * h)),
                  pl.BlockSpec((tk, 128), lambda j, h, i: (j, 2 * h + 1)),
                  pl.BlockSpec((tk, 128), lambda j, h, i: (j, 0)),
                  pl.BlockSpec((tq, 128), lambda j, h, i: (i, h)),
                  pl.BlockSpec((None, tq, 128), lambda j, h, i: (h, i, 0)),
                  pl.BlockSpec((tq, 128), lambda j, h, i: (i, h))],
        out_specs=[pl.BlockSpec((tk, 256), lambda j, h, i: (j, h)),
                   pl.BlockSpec((tk, 128), lambda j, h, i: (j, 0))],
        out_shape=[jax.ShapeDtypeStruct((n_rows, MLA_HEADS * 256), F32), jax.ShapeDtypeStruct((n_rows, LANES), F32)],
        scratch_shapes=[pltpu.VMEM((tk, 256), F32), pltpu.VMEM((tk, 128), F32)],
        compiler_params=_params(("parallel", "arbitrary", "arbitrary")))(q, kv, kv, kr, o, lse, do)


_MESH = pl.DeviceIdType.MESH


def _all_gather(name, x):
    def body(x_ref, out_ref, send_sems, recv_sems, local_sem):
        mx, my, mc = lax.axis_index("x"), lax.axis_index("y"), lax.axis_index("c")
        me, sibling = (mx, my, mc), (mx, my, 1 - mc)
        chips = [(1 - mx, my), (mx, 1 - my), (1 - mx, 1 - my)]

        def slot(px, py, pc):
            return out_ref.at[4 * px + 2 * py + pc]

        def copy(k, block, to, src=None):
            return pltpu.make_async_remote_copy(
                src_ref=slot(*block) if src is None else src, dst_ref=slot(*block),
                send_sem=send_sems.at[k], recv_sem=recv_sems.at[k], device_id=to, device_id_type=_MESH)

        mine = pltpu.make_async_copy(x_ref, slot(*me), local_sem)
        mine.start()
        first = [copy(0, me, sibling, src=x_ref)]
        first += [copy(1 + j, me, (*chip, mc), src=x_ref) for j, chip in enumerate(chips)]
        for cp in first:
            cp.start()
        passed = [copy(4 + j, (*chip, mc), sibling) for j, chip in enumerate(chips)]
        for j, chip in enumerate(chips):
            copy(1 + j, (*chip, mc), me).wait_recv()
            passed[j].start()
        copy(0, sibling, me).wait_recv()
        for j, chip in enumerate(chips):
            copy(4 + j, (*chip, 1 - mc), me).wait_recv()
        for cp in first + passed:
            cp.wait_send()
        mine.wait()

    return pl.pallas_call(
        body, name=name, out_shape=jax.ShapeDtypeStruct((N_DEV,) + x.shape, x.dtype),
        in_specs=[pl.BlockSpec(memory_space=pl.ANY)], out_specs=pl.BlockSpec(memory_space=pl.ANY),
        scratch_shapes=[pltpu.SemaphoreType.DMA((7,)), pltpu.SemaphoreType.DMA((7,)), pltpu.SemaphoreType.DMA(())],
    )(x)


def _all_to_all(name, g):
    def body(g_ref, land_ref, send_sems, recv_sems, local_sem):
        mx, my, mc = lax.axis_index("x"), lax.axis_index("y"), lax.axis_index("c")
        me = 4 * mx + 2 * my + mc
        mine = pltpu.make_async_copy(g_ref.at[me], land_ref.at[me], local_sem)
        mine.start()
        copies = []
        for k in range(1, N_DEV):
            px = 1 - mx if k & 4 else mx
            py = 1 - my if k & 2 else my
            pc = 1 - mc if k & 1 else mc
            peer = 4 * px + 2 * py + pc
            send = pltpu.make_async_remote_copy(
                src_ref=g_ref.at[peer], dst_ref=land_ref.at[me], send_sem=send_sems.at[k - 1],
                recv_sem=recv_sems.at[k - 1], device_id=(px, py, pc), device_id_type=_MESH)
            recv = pltpu.make_async_remote_copy(
                src_ref=g_ref.at[peer], dst_ref=land_ref.at[peer], send_sem=send_sems.at[k - 1],
                recv_sem=recv_sems.at[k - 1], device_id=(px, py, pc), device_id_type=_MESH)
            send.start()
            copies.append((send, recv))
        for _, recv in copies:
            recv.wait_recv()
        for send, _ in copies:
            send.wait_send()
        mine.wait()

    return pl.pallas_call(
        body, name=name, out_shape=jax.ShapeDtypeStruct(g.shape, g.dtype),
        in_specs=[pl.BlockSpec(memory_space=pl.ANY)], out_specs=pl.BlockSpec(memory_space=pl.ANY),
        scratch_shapes=[pltpu.SemaphoreType.DMA((7,)), pltpu.SemaphoreType.DMA((7,)), pltpu.SemaphoreType.DMA(())],
    )(g)


def _adamw(name, parts, w, m, v, tr):
    rows = w.shape[0]

    def body(p_ref, w_ref, m_ref, v_ref, g_ref, d_ref, nm_ref, nv_ref):
        g = p_ref[0].astype(F32)
        for d in range(1, N_DEV):
            g = g + p_ref[d].astype(F32)
        nm = ADAM_B1 * m_ref[...] + (1.0 - ADAM_B1) * g
        nv = ADAM_B2 * v_ref[...] + (1.0 - ADAM_B2) * jnp.square(g)
        m_hat = nm / (1.0 - ADAM_B1 ** ADAM_STEP)
        v_hat = nv / (1.0 - ADAM_B2 ** ADAM_STEP)
        g_ref[...] = g
        d_ref[...] = -ADAM_LR * (m_hat / (jnp.sqrt(v_hat) + ADAM_EPS) + ADAM_WD * w_ref[...])
        nm_ref[...] = nm
        nv_ref[...] = nv

    spec = pl.BlockSpec((tr, LANES), lambda i: (i, 0))
    return pl.pallas_call(
        body, name=name, grid=(rows // tr,),
        in_specs=[pl.BlockSpec((N_DEV, tr, LANES), lambda i: (0, i, 0)), spec, spec, spec],
        out_specs=[spec] * 4, out_shape=[jax.ShapeDtypeStruct((rows, LANES), F32)] * 4,
        compiler_params=_params(("parallel",)))(parts, w, m, v)


def _in_pieces():
    p = []
    for h in range(RET_HEADS):
        p += [(128 * h, 128 * h + 128), (512 + 128 * h, 512 + 128 * h + 128), (1024 + 256 * h, 1024 + 256 * h + 256)]
    p += [(2048, 3072), (3776, 4800), (4800, 7872), (3072, 3456), 128, (3456, 3712),
          (3712, 3744), 32, (3744, 3776), 32, 128]
    return p


def _uq_pieces():
    p = []
    for h in range(MLA_HEADS):
        b = 192 * h
        p += [(b, b + 128), (b + 128, b + 160), 32, (b + 160, b + 192), 32]
    return p


def _perm(w, pieces):
    cols = [jnp.zeros(w.shape[:-1] + (p,), w.dtype) if isinstance(p, int) else w[..., p[0]:p[1]] for p in pieces]
    return jnp.concatenate(cols, axis=-1)


def _unperm(dw, pieces):
    found, off = [], 0
    for p in pieces:
        if isinstance(p, int):
            off += p
        else:
            found.append((p[0], dw[..., off:off + p[1] - p[0]]))
            off += p[1] - p[0]
    return jnp.concatenate([t for _, t in sorted(found, key=lambda s: s[0])], axis=-1)


def _unshard(blocks, local_shape, axis):
    t = jnp.moveaxis(blocks.reshape((N_DEV,) + tuple(local_shape)), 0, axis)
    shape = list(local_shape)
    shape[axis] *= N_DEV
    return t.reshape(shape)


def _shard_split(full, axis):
    shape = list(full.shape)
    t = full.reshape(shape[:axis] + [N_DEV, shape[axis] // N_DEV] + shape[axis + 1:])
    return jnp.moveaxis(t, axis, 0).reshape(N_DEV, -1)


def _rope_tables(seq):
    pos = jnp.arange(seq, dtype=F32)[:, None]

    def table(dim):
        inv = 1.0 / (ROPE_THETA ** (jnp.arange(0, dim, 2, dtype=F32) / dim))
        ang = pos * inv[None, :]
        return jnp.cos(ang), jnp.sin(ang)

    cr, sr = table(RET_QK)
    cm, sm = table(MLA_ROPE)
    z = jnp.zeros_like(cm)
    return (jnp.concatenate([cr, cr], 1), jnp.concatenate([-sr, sr], 1),
            jnp.concatenate([cm, z, cm, z], 1), jnp.concatenate([-sm, z, sm, z], 1))


def _s5_maps(a_re, a_im, log_dt, b_re, b_im, c_re, c_im):
    dt = jnp.exp(log_dt)[:, None]
    ar = jnp.minimum(a_re, -1e-4)
    mag = jnp.exp(dt * ar)
    abar_re = mag * jnp.cos(dt * a_im)
    abar_im = mag * jnp.sin(dt * a_im)
    den = ar * ar + a_im * a_im
    nr = abar_re - 1.0
    ni = abar_im
    coef_re = (nr * ar + ni * a_im) / den
    coef_im = (ni * ar - nr * a_im) / den
    bb_re = coef_re[..., None] * b_re - coef_im[..., None] * b_im
    bb_im = coef_re[..., None] * b_im + coef_im[..., None] * b_re
    eye = jnp.eye(S5_BLOCKS, dtype=F32)

    def in_blocks(bb):
        t = bb.transpose(0, 2, 1).reshape(S5_BLOCKS, 8, S5_GROUP, S5_STATE)
        return jnp.einsum('jgcp,gh->jgchp', t, eye).reshape(S5_BLOCKS, 128, 512)

    def out_blocks(cc):
        t = cc.transpose(0, 2, 1).reshape(S5_BLOCKS, 8, S5_STATE, S5_GROUP)
        return jnp.einsum('jgpc,gh->jgphc', t, eye).reshape(S5_BLOCKS, 512, 128)

    arow = jnp.concatenate([abar_re.reshape(S5_BLOCKS, 512), abar_im.reshape(S5_BLOCKS, 512)], axis=1).reshape(1, -1)
    b_map = jnp.concatenate([in_blocks(bb_re), in_blocks(bb_im)], axis=2)
    c_map = jnp.concatenate([out_blocks(c_re), -out_blocks(c_im)], axis=1)
    return arow, b_map, c_map


def _power_tables(arow, conj, reverse):
    a = arow.reshape(S5_BLOCKS, 2, 512)
    ar, ai = a[:, 0], (-a[:, 1] if conj else a[:, 1])
    pw = [(ar, ai)]
    for _ in range(SUBLANES - 1):
        pr, pi = pw[-1]
        pw.append((pr * ar - pi * ai, pr * ai + pi * ar))

    def rows(sel):
        return jnp.stack([jnp.stack(list(pw[i]), axis=1) for i in sel], axis=0).reshape(len(sel), -1)

    ak = jnp.concatenate([rows([0, 1, 3]), jnp.zeros((SUBLANES - 3, arow.shape[1]), F32)], axis=0)
    order = list(range(SUBLANES))
    apow = rows(order[::-1] if reverse else order)
    return ak, apow


def _rows(arr):
    return (arr, arr.shape[1], 0)


def _vjp_rows(f, n_prim):
    def fn(r, c, _):
        _, vjp = jax.vjp(f, *r[:n_prim])
        return list(vjp(r[n_prim])), []
    return fn


def _norm_bwd(r, c, _):
    _, vjp = jax.vjp(_f_rms, r[0], c[0])
    dx, dg = vjp(r[1])
    return [dx + r[2]], [dg]


def _layer_fwd(l, x, W, P, T):
    n = x.shape[0]
    tm, tmw = _pick(n, 256), _pick(n, 128)
    cos_r, sin_r, cos_m, sin_m = T

    def nm(s):
        return f"l{l}_{s}"

    def one(name, f, rows, consts, width, dtype=BF16, tile=tm):
        return _tile_call(nm(name), lambda r, c, _: ([f(r, c)], []), n, tile, rows, consts, [(width, dtype)])[0]

    h = one("norm1", lambda r, c: _f_rms(r[0], c[0]), [_rows(x)], [P["norm1_g"]], D_MODEL)
    z = _mm(nm("in_proj"), h, W["in"])
    of, stf = _ret_fwd(nm("ret_f"), z, cos_r, sin_r, P["rd"][0], reverse=False)
    ob, stb = _ret_fwd(nm("ret_b"), z, cos_r, sin_r, P["rd"][1], reverse=True)

    def gn(r, c, _):
        yraw = r[0] + r[1]
        ys = [_f_gn_gate(yraw[:, RET_V * i:RET_V * (i + 1)], r[2][:, RET_V * i:RET_V * (i + 1)],
                         c[0][:, RET_V * i:RET_V * (i + 1)]) for i in range(RET_HEADS)]
        return [yraw, jnp.concatenate(ys, axis=1)], []

    yraw, yret = _tile_call(nm("ret_gn"), gn, n, tm, [_rows(of), _rows(ob), (z, 1024, Z_RG // 1024)], [P["ret_gn_g"]],
                            [(1024, F32), (1024, BF16)])

    cqn = one("q_norm", lambda r, c: _f_rms(r[0][:, :MLA_Q_LORA], c[0]), [(z, 512, Z_CQ // 512)], [P["mla_q_norm_g"]], MLA_Q_LORA)
    ckvn = one("kv_norm", lambda r, c: _f_rms(r[0], c[0]), [(z, 256, Z_CKV // 256)], [P["mla_kv_norm_g"]], MLA_KV_LORA)
    qraw = _mm(nm("q_up"), cqn, W["uq"])
    kv = _mm(nm("kv_up"), ckvn, W["ukv"], out_dtype=BF16)
    q = one("q_rope", lambda r, c: _mla_q(r[0], r[1], r[2], False), [_rows(qraw), _rows(cos_m), _rows(sin_m)], [], 2048)
    kr = one("k_rope", lambda r, c: _rope(r[0], r[1], r[2]), [(z, 128, Z_KR // 128), _rows(cos_m), _rows(sin_m)], [], 128)
    o, lse = _attn_fwd(nm("attn"), q, kv, kr)

    xs = []
    for d in range(2):
        bu = _bd_mm(nm(f"s5_in{d}"), [(z, Z_U // 128, P["s5"][d]["b_map"])], n)
        xs.append(_scan(nm(f"s5_scan{d}"), bu, P["s5"][d]["ak"], P["s5"][d]["apow"], reverse=(d == 1)))
    ysum = _bd_mm(nm("s5_out"), [(xs[0], 0, P["s5"][0]["c_map"]), (xs[1], 0, P["s5"][1]["c_map"])], n)
    gact = one("s5_act", lambda r, c: _f_s5_act(r[0], r[1], c[0]), [_rows(ysum), (z, 1024, Z_U // 1024)], [P["s5_d"]], 1024)
    ga = _mm(nm("glu_a"), gact, W["glu_a"])
    gb = _mm(nm("glu_b"), gact, W["glu_b"])
    ys5 = one("glu", lambda r, c: _f_glu(r[0], r[1]), [_rows(ga), _rows(gb)], [], 1024)

    ys = (yret, o, ys5)
    ps = [_mm(nm(f"branch{i}"), ys[i], W["branch"][i]) for i in range(3)]
    mix = one("mix", lambda r, c: _f_gate(r[0], r[3]) + _f_gate(r[1], r[4]) + _f_gate(r[2], r[5]),
              [(z, 1024, Z_GATE // 1024 + i) for i in range(3)] + [_rows(p) for p in ps], [], 1024)
    x1 = _mm(nm("out_proj"), mix, W["out"], add=x)
    h2 = one("norm2", lambda r, c: _f_rms(r[0], c[0]), [_rows(x1)], [P["norm2_g"]], D_MODEL)
    gp = _mm(nm("ffn_g"), h2, W["ffn_g"])
    up = _mm(nm("ffn_u"), h2, W["ffn_u"])
    act = one("swiglu", lambda r, c: _f_swiglu(r[0], r[1]), [_rows(gp), _rows(up)], [], FFN_HIDDEN, tile=tmw)
    x2 = _mm(nm("ffn_down"), act, W["ffn_down"], add=x1)
    saved = dict(x=x, h=h, z=z, stf=stf, stb=stb, yraw=yraw, ys=ys, cqn=cqn, ckvn=ckvn, qraw=qraw, q=q, kv=kv, kr=kr,
                 lse=lse, xs=xs, ysum=ysum, gact=gact, ga=ga, gb=gb, ps=ps, mix=mix, x1=x1, h2=h2, gp=gp, up=up, act=act)
    return x2, saved


def _layer_bwd(l, dx2, sv, W, P, T):
    n = dx2.shape[0]
    tm, tmw = _pick(n, 256), _pick(n, 128)
    cos_r, sin_r, cos_m, sin_m = T
    z = sv["z"]
    g = {}

    def nm(s):
        return f"l{l}_{s}"

    dact = _mm(nm("d_act"), dx2, W["ffn_down"], tb=True)
    g["ffn_down"] = _mm(nm("dw_ffn_down"), sv["act"], dx2, ta=True)
    dgp, dup = _tile_call(nm("d_swiglu"), _vjp_rows(_f_swiglu, 2), n, tmw, [_rows(sv["gp"]), _rows(sv["up"]), _rows(dact)], [],
                          [(FFN_HIDDEN, F32)] * 2)
    dh2 = _mm(nm("d_h2_g"), dgp, W["ffn_g"], tb=True)
    dh2 = _mm(nm("d_h2_u"), dup, W["ffn_u"], tb=True, add=dh2)
    g["ffn_g"] = _mm(nm("dw_ffn_g"), sv["h2"], dgp, ta=True)
    g["ffn_u"] = _mm(nm("dw_ffn_u"), sv["h2"], dup, ta=True)
    dx1, g["norm2_g"] = _tile_call(nm("d_norm2"), _norm_bwd, n, tm, [_rows(sv["x1"]), _rows(dh2), _rows(dx2)], [P["norm2_g"]],
                                   [(D_MODEL, F32)], acc_outs=[(1, D_MODEL)])

    dmix = _mm(nm("d_mix"), dx1, W["out"], tb=True)
    g["out"] = _mm(nm("dw_out"), sv["mix"], dx1, ta=True)
    dz = jnp.zeros((n, ZW), F32)
    dys, g["branch"] = [], []
    for i in range(3):
        dz, dp = _tile_call(nm(f"d_gate{i}"), _vjp_rows(_f_gate, 2), n, tm,
                            [(z, 1024, Z_GATE // 1024 + i), _rows(sv["ps"][i]), _rows(dmix)], [], [(1024, F32)],
                            alias=(dz, 1024, Z_GATE // 1024 + i))
        dys.append(_mm(nm(f"d_branch{i}"), dp, W["branch"][i], tb=True))
        g["branch"].append(_mm(nm(f"dw_branch{i}"), sv["ys"][i], dp, ta=True))

    dga, dgb = _tile_call(nm("d_glu"), _vjp_rows(_f_glu, 2), n, tm, [_rows(sv["ga"]), _rows(sv["gb"]), _rows(dys[2])], [],
                          [(1024, F32)] * 2)
    dgact = _mm(nm("d_gact_a"), dga, W["glu_a"], tb=True)
    dgact = _mm(nm("d_gact_b"), dgb, W["glu_b"], tb=True, add=dgact)
    g["glu_a"] = _mm(nm("dw_glu_a"), sv["gact"], dga, ta=True)
    g["glu_b"] = _mm(nm("dw_glu_b"), sv["gact"], dgb, ta=True)

    def act_bwd(r, c, _):
        _, vjp = jax.vjp(_f_s5_act, r[0], r[1], c[0])
        dy, du, dd = vjp(r[2])
        return [dy, du], [dd]

    dysum, du_part, g["s5_d"] = _tile_call(nm("d_s5_act"), act_bwd, n, tm,
                                           [_rows(sv["ysum"]), (z, 1024, Z_U // 1024), _rows(dgact)], [P["s5_d"]],
                                           [(1024, F32)] * 2, acc_outs=[(1, 1024)])
    lams, g["s5"] = [], []
    for d in range(2):
        m = P["s5"][d]
        dxs = _bd_mm(nm(f"d_s5_out{d}"), [(dysum, 0, m["c_map_t"])], n)
        g_c = _bd_mm_tn(nm(f"dw_s5_c{d}"), sv["xs"][d], 0, 1024, dysum, 0, 128, n)
        lam, da = _scan(nm(f"d_s5_scan{d}"), dxs, m["ak_adj"], m["apow_adj"], reverse=(d == 0), xs=sv["xs"][d])
        g_b = _bd_mm_tn(nm(f"dw_s5_b{d}"), z, Z_U // 128, 128, lam, 0, 1024, n)
        lams.append(lam)
        g["s5"].append((da[:1], g_b, g_c))
    dz = _bd_mm(nm("d_s5_in"), [(lams[0], 0, P["s5"][0]["b_map_t"]), (lams[1], 0, P["s5"][1]["b_map_t"])], n,
                add=du_part, alias=(dz, Z_U // 128))

    o = sv["ys"][1]
    dq = _attn_bwd_dq(nm("d_attn_q"), sv["q"], sv["kv"], sv["kr"], o, sv["lse"], dys[1])
    dkv, dkr = _attn_bwd_dkv(nm("d_attn_kv"), sv["q"], sv["kv"], sv["kr"], o, sv["lse"], dys[1])
    dqraw, = _tile_call(nm("d_q_rope"), lambda r, c, _: ([_mla_q(r[0], r[1], r[2], True)], []), n, tm,
                        [_rows(dq), _rows(cos_m), _rows(sin_m)], [], [(2048, F32)])
    dz, = _tile_call(nm("d_k_rope"), lambda r, c, _: ([_rope_t(r[0], r[1], r[2])], []), n, tm,
                     [_rows(dkr), _rows(cos_m), _rows(sin_m)], [], [], alias=(dz, 128, Z_KR // 128))
    dcqn = _mm(nm("d_cqn"), dqraw, W["uq"], tb=True)
    g["uq"] = _mm(nm("dw_uq"), sv["cqn"], dqraw, ta=True)
    dckvn = _mm(nm("d_ckvn"), dkv, W["ukv"], tb=True)
    g["ukv"] = _mm(nm("dw_ukv"), sv["ckvn"], dkv, ta=True)

    def qn_bwd(r, c, _):
        _, vjp = jax.vjp(_f_rms, r[0][:, :MLA_Q_LORA], c[0])
        da, dg = vjp(r[1])
        return [jnp.concatenate([da, jnp.zeros((da.shape[0], 512 - MLA_Q_LORA), F32)], axis=1)], [dg]

    dz, g["mla_q_norm_g"] = _tile_call(nm("d_q_norm"), qn_bwd, n, tm, [(z, 512, Z_CQ // 512), _rows(dcqn)], [P["mla_q_norm_g"]],
                                       [], acc_outs=[(1, MLA_Q_LORA)], alias=(dz, 512, Z_CQ // 512))

    def kvn_bwd(r, c, _):
        _, vjp = jax.vjp(_f_rms, r[0], c[0])
        da, dg = vjp(r[1])
        return [da], [dg]

    dz, g["mla_kv_norm_g"] = _tile_call(nm("d_kv_norm"), kvn_bwd, n, tm, [(z, 256, Z_CKV // 256), _rows(dckvn)],
                                        [P["mla_kv_norm_g"]], [], acc_outs=[(1, MLA_KV_LORA)], alias=(dz, 256, Z_CKV // 256))

    def gn_bwd(r, c, _):
        drg, dy, dg = [], [], []
        for i in range(RET_HEADS):
            sl = slice(RET_V * i, RET_V * (i + 1))
            _, vjp = jax.vjp(_f_gn_gate, r[0][:, sl], r[1][:, sl], c[0][:, sl])
            a, b, e = vjp(r[2][:, sl])
            dy.append(a)
            drg.append(b)
            dg.append(e)
        return [jnp.concatenate(drg, axis=1), jnp.concatenate(dy, axis=1)], [jnp.concatenate(dg, axis=1)]

    dz, dyraw, g["ret_gn_g"] = _tile_call(nm("d_ret_gn"), gn_bwd, n, tm,
                                          [_rows(sv["yraw"]), (z, 1024, Z_RG // 1024), _rows(dys[0])], [P["ret_gn_g"]],
                                          [(1024, F32)], acc_outs=[(1, 1024)], alias=(dz, 1024, Z_RG // 1024))
    dz, drd_f = _ret_bwd(nm("d_ret_f"), z, cos_r, sin_r, P["rd"][0], sv["stf"], dyraw, dz, reverse=False)
    dz, drd_b = _ret_bwd(nm("d_ret_b"), z, cos_r, sin_r, P["rd"][1], sv["stb"], dyraw, dz, reverse=True)
    g["ret_decay"] = jnp.stack([drd_f[:, 0, 0], drd_b[:, 0, 0]], axis=0)

    dh = _mm(nm("d_h"), dz, W["in"], tb=True)
    g["in"] = _mm(nm("dw_in"), sv["h"], dz, ta=True)
    dx, g["norm1_g"] = _tile_call(nm("d_norm1"), _norm_bwd, n, tm, [_rows(sv["x"]), _rows(dh), _rows(dx1)], [P["norm1_g"]],
                                  [(D_MODEL, F32)], acc_outs=[(1, D_MODEL)])
    return dx, g


INPUT_NAMES = ("x",) + WEIGHTS + ("loss_target",) + tuple("m_" + n for n in WEIGHTS) + tuple("v_" + n for n in WEIGHTS)
S5_NAMES = ("s5_a_re", "s5_a_im", "s5_log_dt", "s5_b_re", "s5_b_im", "s5_c_re", "s5_c_im")


def _pack_rows(arrays, tile_rows):
    flat = jnp.concatenate([a.reshape(-1) for a in arrays])
    pad = -flat.shape[0] % (tile_rows * LANES)
    if pad:
        flat = jnp.concatenate([flat, jnp.zeros((pad,), flat.dtype)])
    return flat.reshape(-1, LANES)


def _unpack_rows(packed, shapes):
    flat, out, off = packed.reshape(-1), [], 0
    for s in shapes:
        size = math.prod(s)
        out.append(flat[off:off + size].reshape(s))
        off += size
    return out


def _local_step(inp, full, x, target):
    n = x.shape[0]
    tables = _rope_tables(n)
    in_pieces, uq_pieces = _in_pieces(), _uq_pieces()
    Ws, Ps, s5_vjps = [], [], []
    for l in range(DEPTH):
        Ws.append(dict(
            **{"in": _perm(full["w_in"][l], in_pieces)}, uq=_perm(full["mla_w_uq"][l], uq_pieces), ukv=full["mla_w_ukv"][l],
            glu_a=full["s5_w_glu"][l][:, :1024], glu_b=full["s5_w_glu"][l][:, 1024:],
            branch=[full["w_branch"][l, i] for i in range(3)], out=full["w_out"][l],
            ffn_g=full["ffn_w_gu"][l][:, :FFN_HIDDEN], ffn_u=full["ffn_w_gu"][l][:, FFN_HIDDEN:],
            ffn_down=full["ffn_w_down"][l]))
        s5, vjps = [], []
        for d in range(2):
            (arow, b_map, c_map), vjp = jax.vjp(_s5_maps, *[inp[k][l, d] for k in S5_NAMES])
            arow = lax.stop_gradient(arow)
            ak, apow = _power_tables(arow, False, d == 1)
            ak_adj, apow_adj = _power_tables(arow, True, d == 0)
            s5.append(dict(b_map=b_map, c_map=c_map, b_map_t=b_map.transpose(0, 2, 1), c_map_t=c_map.transpose(0, 2, 1),
                           ak=ak, apow=apow, ak_adj=ak_adj, apow_adj=apow_adj))
            vjps.append(vjp)
        s5_vjps.append(vjps)
        Ps.append(dict(
            norm1_g=inp["norm1_g"][l][None], norm2_g=inp["norm2_g"][l][None], ret_gn_g=inp["ret_gn_g"][l][None],
            mla_q_norm_g=inp["mla_q_norm_g"][l][None], mla_kv_norm_g=inp["mla_kv_norm_g"][l][None], s5_d=inp["s5_d"][l][None],
            rd=[jnp.broadcast_to(inp["ret_decay"][l, d][:, None, None], (RET_HEADS, 1, LANES)) for d in range(2)], s5=s5))

    h, saved = x, []
    for l in range(DEPTH):
        h, sv = _layer_fwd(l, h, Ws[l], Ps[l], tables)
        saved.append(sv)

    def loss_bwd(r, c, _):
        loss, vjp = jax.vjp(lambda a, gain: _f_loss(a, gain, r[1]), r[0], c[0])
        da, dg = vjp(jnp.ones((), F32))
        return [da], [dg, jnp.broadcast_to(loss, (1, LANES))]

    dh, g_final, loss_row = _tile_call("loss", loss_bwd, n, _pick(n, 256), [_rows(h), _rows(target)], [inp["final_g"][None]],
                                       [(D_MODEL, F32)], acc_outs=[(1, D_MODEL), (1, LANES)])
    layer_g = [None] * DEPTH
    for l in reversed(range(DEPTH)):
        dh, layer_g[l] = _layer_bwd(l, dh, saved[l], Ws[l], Ps[l], tables)

    def stack(f):
        return jnp.stack([f(layer_g[l], l) for l in range(DEPTH)], axis=0)

    grads = dict(
        w_in=stack(lambda g, l: _unperm(g["in"], in_pieces)), mla_w_uq=stack(lambda g, l: _unperm(g["uq"], uq_pieces)),
        mla_w_ukv=stack(lambda g, l: g["ukv"]), s5_w_glu=stack(lambda g, l: jnp.concatenate([g["glu_a"], g["glu_b"]], axis=1)),
        w_branch=stack(lambda g, l: jnp.stack(g["branch"], axis=0)), w_out=stack(lambda g, l: g["out"]),
        ffn_w_gu=stack(lambda g, l: jnp.concatenate([g["ffn_g"], g["ffn_u"]], axis=1)),
        ffn_w_down=stack(lambda g, l: g["ffn_down"]),
        norm1_g=stack(lambda g, l: g["norm1_g"][0]), norm2_g=stack(lambda g, l: g["norm2_g"][0]),
        ret_gn_g=stack(lambda g, l: g["ret_gn_g"][0]), mla_q_norm_g=stack(lambda g, l: g["mla_q_norm_g"][0]),
        mla_kv_norm_g=stack(lambda g, l: g["mla_kv_norm_g"][0]), s5_d=stack(lambda g, l: g["s5_d"][0]),
        ret_decay=stack(lambda g, l: g["ret_decay"]), final_g=g_final[0])
    s5_grads = [[s5_vjps[l][d](layer_g[l]["s5"][d]) for d in range(2)] for l in range(DEPTH)]
    for i, k in enumerate(S5_NAMES):
        grads[k] = jnp.stack([jnp.stack([s5_grads[l][d][i] for d in range(2)], axis=0) for l in range(DEPTH)], axis=0)
    return loss_row[0, 0], dh, grads


def kernel(*args):
    inp = dict(zip(INPUT_NAMES, args))

    sharded_tr = 1024
    local = _pack_rows([inp[k].astype(BF16) for k in SHARDED], sharded_tr)
    gathered = _all_gather("gather_weights", local).reshape(N_DEV, -1)
    full, off = {}, 0
    for k in SHARDED:
        size = inp[k].size
        full[k] = _unshard(gathered[:, off:off + size], inp[k].shape, SHARD_AXIS[k])
        off += size

    loss, dh, grads = _local_step(inp, full, inp["x"][0], inp["loss_target"][0])
    loss = lax.psum(loss, ("x", "y", "c"))

    parts = jnp.concatenate([_shard_split(grads[k], SHARD_AXIS[k]) for k in SHARDED], axis=1).astype(BF16)
    pad = local.shape[0] * LANES - parts.shape[1]
    if pad:
        parts = jnp.concatenate([parts, jnp.zeros((N_DEV, pad), BF16)], axis=1)
    landed = _all_to_all("exchange_grads", parts.reshape(N_DEV, -1, LANES))
    packed = [_pack_rows([inp[p + k] for k in SHARDED], sharded_tr) for p in ("", "m_", "v_")]
    res_sharded = _adamw("adamw_sharded", landed, *packed, sharded_tr)

    small_tr = 512
    partial = _pack_rows([grads[k] for k in SMALL], small_tr)
    packed = [_pack_rows([inp[p + k] for k in SMALL], small_tr) for p in ("", "m_", "v_")]
    res_small = _adamw("adamw_small", _all_gather("gather_small_grads", partial), *packed, small_tr)

    out = {}
    for kind, a, b in zip(("grad_", "delta_", "new_m_", "new_v_"), res_sharded, res_small):
        for k, t in zip(SHARDED, _unpack_rows(a, [inp[k].shape for k in SHARDED])):
            out[kind + k] = t
        for k, t in zip(SMALL, _unpack_rows(b, [inp[k].shape for k in SMALL])):
            out[kind + k] = t
    return (loss, dh[None]) + tuple(out[kind + k] for kind in ("grad_", "delta_", "new_m_", "new_v_") for k in WEIGHTS)
```

```python
import functools
import math

import jax
import jax.numpy as jnp
from jax import lax
from jax.experimental import pallas as pl
from jax.experimental.pallas import tpu as pltpu

F32 = jnp.float32
BF16 = jnp.bfloat16

D_MODEL = 1024
DEPTH = 2
RMS_EPS = 1e-6
GN_EPS = 1e-5
ROPE_THETA = 10000.0
RET_HEADS, RET_QK, RET_V, RET_CHUNK = 4, 128, 256, 128
MLA_HEADS, MLA_Q_LORA, MLA_KV_LORA, MLA_NOPE, MLA_ROPE, MLA_V = 8, 384, 256, 128, 64, 128
S5_GROUPS, S5_GROUP, S5_STATE = 64, 16, 64
S5_BLOCKS = 8
FFN_HIDDEN = 2816
N_DEV = 8
ADAM_LR, ADAM_B1, ADAM_B2, ADAM_EPS, ADAM_WD, ADAM_STEP = 0.001, 0.9, 0.999, 1e-08, 0.01, 10

LANES = 128
SUBLANES = 8
VMEM_LIMIT = 48 * 1024 * 1024

ZW = 8192
Z_RET = 0
Z_RG = 2048
Z_U = 3072
Z_GATE = 4096
Z_CQ = 7168
Z_CKV = 7680
Z_KR = 7936
IN_SPLITS = (512, 512, 1024, 1024, 384, 256, 64, 1024, 3072)

SHARDED = ("w_in", "mla_w_uq", "mla_w_ukv", "s5_w_glu", "w_branch", "w_out", "ffn_w_gu", "ffn_w_down")
SHARD_AXIS = {"w_in": 2, "mla_w_uq": 2, "mla_w_ukv": 2, "s5_w_glu": 2, "w_branch": 2, "w_out": 1,
              "ffn_w_gu": 2, "ffn_w_down": 1}
SMALL = ("norm1_g", "ret_decay", "ret_gn_g", "mla_q_norm_g", "mla_kv_norm_g", "s5_a_re", "s5_a_im", "s5_log_dt",
         "s5_b_re", "s5_b_im", "s5_c_re", "s5_c_im", "s5_d", "norm2_g", "final_g")
WEIGHTS = ("norm1_g", "w_in", "ret_decay", "ret_gn_g", "mla_q_norm_g", "mla_w_uq", "mla_kv_norm_g", "mla_w_ukv",
           "s5_a_re", "s5_a_im", "s5_log_dt", "s5_b_re", "s5_b_im", "s5_c_re", "s5_c_im", "s5_d", "s5_w_glu",
           "w_branch", "w_out", "norm2_g", "ffn_w_gu", "ffn_w_down", "final_g")


def _params(sem=None):
    return pltpu.CompilerParams(dimension_semantics=sem, vmem_limit_bytes=VMEM_LIMIT)


def _pick(n, cap):
    if n <= cap:
        return n
    t = cap - cap % LANES
    while t >= LANES:
        if n % t == 0:
            return t
        t -= LANES
    return n


@functools.partial(jax.custom_vjp, nondiff_argnums=(2, 3))
def _bdot(a, b, ca, cb):
    return lax.dot_general(a.astype(BF16), b.astype(BF16), (((ca,), (cb,)), ((), ())), preferred_element_type=F32)


def _bdot_fwd(a, b, ca, cb):
    return _bdot(a, b, ca, cb), (a, b)


def _bdot_bwd(ca, cb, res, g):
    a, b = res
    da = _bdot(g, b, 1, 1 - cb) if ca == 1 else _bdot(b, g, 1 - cb, 1)
    db = _bdot(a, g, 1 - ca, 0) if cb == 0 else _bdot(g, a, 0, 1 - ca)
    return da, db


_bdot.defvjp(_bdot_fwd, _bdot_bwd)


@jax.custom_vjp
def _swap_halves(x):
    return pltpu.roll(x, LANES // 2, 1)


_swap_halves.defvjp(lambda x: (_swap_halves(x), None), lambda _, g: (_swap_halves(g),))


def _rope(x, cosf, sinf):
    return x * cosf + _swap_halves(x) * sinf


def _f_rms(x, g):
    return x * lax.rsqrt(jnp.mean(x * x, axis=-1, keepdims=True) + RMS_EPS) * g


def _rope_t(g, cosf, sinf):
    return g * cosf + _swap_halves(g * sinf)


def _f_gn_gate(yh, rgh, gh):
    mu = jnp.mean(yh, axis=-1, keepdims=True)
    var = jnp.mean(jnp.square(yh - mu), axis=-1, keepdims=True)
    return jax.nn.silu(rgh) * ((yh - mu) * lax.rsqrt(var + GN_EPS) * gh)


MLA_SCALE = (MLA_NOPE + MLA_ROPE) ** -0.5


def _mla_q(qraw, cosf, sinf, transpose):
    parts = []
    for h in range(MLA_HEADS):
        parts.append(qraw[:, 256 * h:256 * h + 128] * MLA_SCALE)
        r = qraw[:, 256 * h + 128:256 * h + 256]
        parts.append(_rope_t(r * MLA_SCALE, cosf, sinf) if transpose else _rope(r, cosf, sinf) * MLA_SCALE)
    return jnp.concatenate(parts, axis=1)


def _f_s5_act(ysum, u, d):
    return jax.nn.gelu(ysum + d * u)


def _f_glu(ga, gb):
    return ga * jax.nn.sigmoid(gb)


def _f_gate(zg, proj):
    return jax.nn.sigmoid(zg) * proj


def _f_swiglu(gp, up):
    return jax.nn.silu(gp) * up


def _f_loss(x, g, target):
    y = _f_rms(x, g)
    err = jnp.square(y - target)
    return 0.5 * jnp.sum(jnp.mean(err, axis=-1))


def _tile_call(name, fn, n_rows, tm, row_ins, consts, row_outs, acc_outs=(), alias=None):
    n_row_in, n_const = len(row_ins), len(consts)
    args = [a for a, _, _ in row_ins] + list(consts)
    in_specs = [pl.BlockSpec((tm, w), lambda i, cb=cb: (i, cb)) for _, w, cb in row_ins]
    in_specs += [pl.BlockSpec(c.shape, lambda i: (0, 0)) for c in consts]
    out_shape, out_specs, aliases = [], [], {}
    if alias is not None:
        arr, w, cb = alias
        in_specs.append(pl.BlockSpec((tm, w), lambda i, cb=cb: (i, cb)))
        aliases[len(args)] = 0
        args.append(arr)
        out_shape.append(jax.ShapeDtypeStruct(arr.shape, arr.dtype))
        out_specs.append(pl.BlockSpec((tm, w), lambda i, cb=cb: (i, cb)))
    for w, dt in row_outs:
        out_shape.append(jax.ShapeDtypeStruct((n_rows, w), dt))
        out_specs.append(pl.BlockSpec((tm, w), lambda i: (i, 0)))
    n_row_out = len(out_shape)
    for r, w in acc_outs:
        out_shape.append(jax.ShapeDtypeStruct((r, w), F32))
        out_specs.append(pl.BlockSpec((r, w), lambda i: (0, 0)))
    n_in = len(args)

    def body(*refs):
        ins, outs = refs[:n_in], refs[n_in:]
        rows = [r[...] for r in ins[:n_row_in]]
        cons = [r[...] for r in ins[n_row_in:n_row_in + n_const]]
        prev = ins[-1][...] if alias is not None else None
        res_rows, res_accs = fn(rows, cons, prev)
        for o, r in zip(outs[:n_row_out], res_rows):
            o[...] = r.astype(o.dtype)
        if acc_outs:
            @pl.when(pl.program_id(0) == 0)
            def _():
                for o in outs[n_row_out:]:
                    o[...] = jnp.zeros(o.shape, F32)
            for o, r in zip(outs[n_row_out:], res_accs):
                o[...] += r

    return pl.pallas_call(
        body, name=name, grid=(n_rows // tm,), in_specs=in_specs, out_specs=out_specs, out_shape=out_shape,
        input_output_aliases=aliases, compiler_params=_params(("arbitrary",)))(*args)


def _mm(name, a, b, *, ta=False, tb=False, add=None, out_dtype=F32):
    (K, M) = a.shape if ta else a.shape[::-1]
    (N, K2) = b.shape if tb else b.shape[::-1]
    assert K == K2, (name, a.shape, b.shape)
    tm, tn, tk = _pick(M, 1536), _pick(N, 1536), _pick(K, 1024)
    gi, gj, nk = M // tm, N // tn, K // tk
    a_bytes, b_bytes = a.size * a.dtype.itemsize, b.size * b.dtype.itemsize
    j_outer = nk == 1 and b_bytes + a_bytes * gj < a_bytes + b_bytes * gi

    def im(f):
        return (lambda g0, g1, k: f(g1, g0, k)) if j_outer else f

    a_spec = pl.BlockSpec((tk, tm), im(lambda i, j, k: (k, i))) if ta else pl.BlockSpec((tm, tk), im(lambda i, j, k: (i, k)))
    b_spec = pl.BlockSpec((tn, tk), im(lambda i, j, k: (j, k))) if tb else pl.BlockSpec((tk, tn), im(lambda i, j, k: (k, j)))
    o_spec = pl.BlockSpec((tm, tn), im(lambda i, j, k: (i, j)))
    dn = (((0 if ta else 1,), (1 if tb else 0,)), ((), ()))
    has_add = add is not None

    def body(*refs):
        if has_add:
            a_ref, b_ref, add_ref, o_ref, acc = refs
        else:
            a_ref, b_ref, o_ref, acc = refs
        k = pl.program_id(2)

        @pl.when(k == 0)
        def _():
            acc[...] = jnp.zeros(acc.shape, F32)

        acc[...] += lax.dot_general(a_ref[...].astype(BF16), b_ref[...].astype(BF16), dn, preferred_element_type=F32)

        @pl.when(k == nk - 1)
        def _():
            r = acc[...]
            if has_add:
                r = r + add_ref[...]
            o_ref[...] = r.astype(out_dtype)

    args, specs = [a, b], [a_spec, b_spec]
    if has_add:
        args.append(add)
        specs.append(o_spec)
    return pl.pallas_call(
        body, name=name, grid=(gj, gi, nk) if j_outer else (gi, gj, nk), in_specs=specs, out_specs=o_spec,
        out_shape=jax.ShapeDtypeStruct((M, N), out_dtype), scratch_shapes=[pltpu.VMEM((tm, tn), F32)],
        compiler_params=_params(("parallel", "parallel", "arbitrary")))(*args)


S5_BW = 2 * S5_STATE * (S5_GROUPS // S5_BLOCKS)


def _scan_tile(buf, ak_ref, ap_ref, carry, *, reverse, x_ref=None, da_ref=None):
    tt, bw = buf.shape
    hw = bw // 2
    ng = tt // SUBLANES
    rowid = lax.broadcasted_iota(jnp.int32, (SUBLANES, hw), 0)
    steps = []
    for n, k in enumerate((1, 2, 4)):
        valid = (rowid < SUBLANES - k) if reverse else (rowid >= k)
        steps.append((SUBLANES - k if reverse else k, valid, ak_ref[n:n + 1, :hw], ak_ref[n:n + 1, hw:]))
    apr, api = ap_ref[:, :hw], ap_ref[:, hw:]
    first = (rowid == SUBLANES - 1) if reverse else (rowid == 0)

    def group(gi, c):
        cr, ci = c
        r0 = pl.multiple_of(((ng - 1 - gi) if reverse else gi) * SUBLANES, SUBLANES)
        xr, xi = buf[pl.ds(r0, SUBLANES), :hw], buf[pl.ds(r0, SUBLANES), hw:]
        for sh, valid, kr, ki in steps:
            sr = jnp.where(valid, pltpu.roll(xr, sh, 0), 0.0)
            si = jnp.where(valid, pltpu.roll(xi, sh, 0), 0.0)
            xr, xi = xr + kr * sr - ki * si, xi + kr * si + ki * sr
        xr, xi = xr + apr * cr - api * ci, xi + apr * ci + api * cr
        buf[pl.ds(r0, SUBLANES), :hw] = xr
        buf[pl.ds(r0, SUBLANES), hw:] = xi
        if x_ref is not None:
            sh1 = SUBLANES - 1 if reverse else 1
            pr = jnp.where(first, cr, pltpu.roll(xr, sh1, 0))
            pi = jnp.where(first, ci, pltpu.roll(xi, sh1, 0))
            sr, si = x_ref[pl.ds(r0, SUBLANES), :hw], x_ref[pl.ds(r0, SUBLANES), hw:]
            da_ref[:, :hw] += pr * sr + pi * si
            da_ref[:, hw:] += pi * sr - pr * si
        last = 0 if reverse else SUBLANES - 1
        return (jnp.broadcast_to(xr[last:last + 1], (SUBLANES, hw)), jnp.broadcast_to(xi[last:last + 1], (SUBLANES, hw)))

    cr, ci = lax.fori_loop(0, ng, group, (carry[:, :hw], carry[:, hw:]))
    carry[:, :hw] = cr
    carry[:, hw:] = ci


def _s5_specs(n_rows, reverse):
    tt = _pick(n_rows, 256)
    nt = n_rows // tt

    def rows(width, off):
        return pl.BlockSpec((tt, width), lambda j, t: ((nt - 1 - t) if reverse else t, off + j))

    def per_block(r, c):
        return pl.BlockSpec((None, r, c), lambda j, t: (j, 0, 0))

    return tt, nt, rows, per_block, pl.BlockSpec((SUBLANES, S5_BW), lambda j, t: (0, j))


def _s5_fwd(name, z, m, *, reverse):
    n_rows = z.shape[0]
    tt, nt, rows, per_block, par = _s5_specs(n_rows, reverse)

    def body(u_ref, b_ref, c_ref, ak_ref, ap_ref, x_ref, y_ref, carry):
        @pl.when(pl.program_id(1) == 0)
        def _():
            carry[...] = jnp.zeros(carry.shape, F32)

        x_ref[...] = jnp.dot(u_ref[...].astype(BF16), b_ref[...].astype(BF16), preferred_element_type=F32)
        _scan_tile(x_ref, ak_ref, ap_ref, carry, reverse=reverse)
        y_ref[...] = jnp.dot(x_ref[...].astype(BF16), c_ref[...].astype(BF16), preferred_element_type=F32)

    return pl.pallas_call(
        body, name=name, grid=(S5_BLOCKS, nt),
        in_specs=[rows(LANES, Z_U // LANES), per_block(LANES, S5_BW), per_block(S5_BW, LANES), par, par],
        out_specs=[rows(S5_BW, 0), rows(LANES, 0)],
        out_shape=[jax.ShapeDtypeStruct((n_rows, S5_BLOCKS * S5_BW), F32), jax.ShapeDtypeStruct((n_rows, S5_BLOCKS * LANES), F32)],
        scratch_shapes=[pltpu.VMEM((SUBLANES, S5_BW), F32)],
        compiler_params=_params(("parallel", "arbitrary")))(z, m["b_map"], m["c_map"], m["ak"], m["apow"])


def _s5_bwd(name, z, dy, xs, m, *, reverse):
    n_rows = z.shape[0]
    tt, nt, rows, per_block, par = _s5_specs(n_rows, reverse)

    def body(u_ref, dy_ref, x_ref, bt_ref, ct_ref, ak_ref, ap_ref, du_ref, db_ref, dc_ref, da_ref, lam, carry):
        t = pl.program_id(1)

        @pl.when(t == 0)
        def _():
            carry[...] = jnp.zeros(carry.shape, F32)
            db_ref[...] = jnp.zeros(db_ref.shape, F32)
            dc_ref[...] = jnp.zeros(dc_ref.shape, F32)
            da_ref[...] = jnp.zeros(da_ref.shape, F32)

        dy_b = dy_ref[...].astype(BF16)
        lam[...] = jnp.dot(dy_b, ct_ref[...].astype(BF16), preferred_element_type=F32)
        _scan_tile(lam, ak_ref, ap_ref, carry, reverse=reverse, x_ref=x_ref, da_ref=da_ref)
        lam_b = lam[...].astype(BF16)
        du_ref[...] = jnp.dot(lam_b, bt_ref[...].astype(BF16), preferred_element_type=F32)
        db_ref[...] += lax.dot_general(u_ref[...].astype(BF16), lam_b, _TN, preferred_element_type=F32)
        dc_ref[...] += lax.dot_general(x_ref[...].astype(BF16), dy_b, _TN, preferred_element_type=F32)

        @pl.when(t == nt - 1)
        def _():
            da_ref[...] = jnp.broadcast_to(jnp.sum(da_ref[...], axis=0, keepdims=True), da_ref.shape)

    return pl.pallas_call(
        body, name=name, grid=(S5_BLOCKS, nt),
        in_specs=[rows(LANES, Z_U // LANES), rows(LANES, 0), rows(S5_BW, 0), per_block(S5_BW, LANES), per_block(LANES, S5_BW),
                  par, par],
        out_specs=[rows(LANES, 0), per_block(LANES, S5_BW), per_block(S5_BW, LANES), par],
        out_shape=[jax.ShapeDtypeStruct((n_rows, S5_BLOCKS * LANES), F32), jax.ShapeDtypeStruct((S5_BLOCKS, LANES, S5_BW), F32),
                   jax.ShapeDtypeStruct((S5_BLOCKS, S5_BW, LANES), F32), jax.ShapeDtypeStruct((SUBLANES, S5_BLOCKS * S5_BW), F32)],
        scratch_shapes=[pltpu.VMEM((tt, S5_BW), F32), pltpu.VMEM((SUBLANES, S5_BW), F32)],
        compiler_params=_params(("parallel", "arbitrary")))(z, dy, xs, m["b_map_t"], m["c_map_t"], m["ak_adj"], m["apow_adj"])


def _ret_chunk(zq, zk, v, cosf, sinf, state, rd, reverse):
    c = RET_CHUNK
    lg = jax.nn.log_sigmoid(rd)
    lg1 = jnp.max(lg, axis=1, keepdims=True)
    q = _rope(zq, cosf, sinf) * (RET_QK ** -0.5)
    k = _rope(zk, cosf, sinf)
    pi = lax.broadcasted_iota(jnp.int32, (c, c), 0).astype(F32)
    pj = lax.broadcasted_iota(jnp.int32, (c, c), 1).astype(F32)
    pcol = lax.broadcasted_iota(jnp.int32, (c, 1), 0).astype(F32)
    if reverse:
        diff, mask, pos = pj - pi, pj > pi, (c - 1) - pcol
    else:
        diff, mask, pos = pi - pj, pi >= pj, pcol
    decay_in = jnp.where(mask, jnp.exp(jnp.where(mask, diff, 0.0) * lg), 0.0)
    scores = _bdot(q, k, 1, 1) * decay_in
    inner = _bdot(scores, v, 1, 0)
    k_w = jnp.exp((c - 1 - pos) * lg1)
    kv = _bdot(k * k_w, v, 0, 0)
    q_w = jnp.exp((pos + 1) * lg1)
    cross = _bdot(q, state, 1, 0) * q_w
    new_state = jnp.exp(c * lg1) * state + kv
    return inner + cross, new_state


def _ret_specs(n_chunks, reverse_order):
    cmap = (lambda h, n: n_chunks - 1 - n) if reverse_order else (lambda h, n: n)
    z_spec = pl.BlockSpec((RET_CHUNK, 512), lambda h, n: (cmap(h, n), h))
    t_spec = pl.BlockSpec((RET_CHUNK, LANES), lambda h, n: (cmap(h, n), 0))
    rd_spec = pl.BlockSpec((None, 1, LANES), lambda h, n: (h, 0, 0))
    o_spec = pl.BlockSpec((RET_CHUNK, RET_V), lambda h, n: (cmap(h, n), h))
    st_spec = pl.BlockSpec((None, None, RET_QK, RET_V), lambda h, n: (h, cmap(h, n), 0, 0))
    return z_spec, t_spec, rd_spec, o_spec, st_spec


def _ret_fwd(name, z, cosf, sinf, rd, *, reverse):
    n_rows = z.shape[0]
    n_chunks = n_rows // RET_CHUNK
    z_spec, t_spec, rd_spec, o_spec, st_spec = _ret_specs(n_chunks, reverse)

    def body(z_ref, cos_ref, sin_ref, rd_ref, o_ref, st_ref, state):
        @pl.when(pl.program_id(1) == 0)
        def _():
            state[...] = jnp.zeros(state.shape, F32)

        st = state[...]
        st_ref[...] = st
        zt = z_ref[...]
        out, new = _ret_chunk(zt[:, :128], zt[:, 128:256], zt[:, 256:], cos_ref[...], sin_ref[...], st, rd_ref[...], reverse)
        o_ref[...] = out
        state[...] = new

    return pl.pallas_call(
        body, name=name, grid=(RET_HEADS, n_chunks), in_specs=[z_spec, t_spec, t_spec, rd_spec],
        out_specs=[o_spec, st_spec],
        out_shape=[jax.ShapeDtypeStruct((n_rows, RET_HEADS * RET_V), F32),
                   jax.ShapeDtypeStruct((RET_HEADS, n_chunks, RET_QK, RET_V), F32)],
        scratch_shapes=[pltpu.VMEM((RET_QK, RET_V), F32)], compiler_params=_params(("parallel", "arbitrary")))(z, cosf, sinf, rd)


def _ret_bwd(name, z, cosf, sinf, rd, states, dout, dz, *, reverse):
    n_rows = z.shape[0]
    n_chunks = n_rows // RET_CHUNK
    z_spec, t_spec, rd_spec, o_spec, st_spec = _ret_specs(n_chunks, not reverse)

    def body(z_ref, cos_ref, sin_ref, rd_ref, st_ref, do_ref, dzin_ref, dz_ref, drd_ref, dstate):
        @pl.when(pl.program_id(1) == 0)
        def _():
            dstate[...] = jnp.zeros(dstate.shape, F32)
            drd_ref[...] = jnp.zeros(drd_ref.shape, F32)

        zt = z_ref[...]
        cosv, sinv = cos_ref[...], sin_ref[...]
        _, vjp = jax.vjp(lambda a, b, c, s, r: _ret_chunk(a, b, c, cosv, sinv, s, r, reverse),
                         zt[:, :128], zt[:, 128:256], zt[:, 256:], st_ref[...], rd_ref[...])
        dq, dk, dv, dst, drd = vjp((do_ref[...], dstate[...]))
        dz_ref[...] = (dzin_ref[...].astype(F32) + jnp.concatenate([dq, dk, dv], axis=1)).astype(dz_ref.dtype)
        dstate[...] = dst
        drd_ref[...] += jnp.sum(drd, axis=1, keepdims=True)

    return pl.pallas_call(
        body, name=name, grid=(RET_HEADS, n_chunks),
        in_specs=[z_spec, t_spec, t_spec, rd_spec, st_spec, o_spec, z_spec],
        out_specs=[z_spec, rd_spec],
        out_shape=[jax.ShapeDtypeStruct(dz.shape, dz.dtype), jax.ShapeDtypeStruct((RET_HEADS, 1, LANES), F32)],
        input_output_aliases={6: 0}, scratch_shapes=[pltpu.VMEM((RET_QK, RET_V), F32)],
        compiler_params=_params(("parallel", "arbitrary")))(z, cosf, sinf, rd, states, dout, dz)


_NT = (((1,), (1,)), ((), ()))
_TN = (((0,), (0,)), ((), ()))


def _attn_tiles(n_rows):
    tk = _pick(n_rows, 1024)
    return _pick(n_rows, 512), tk, min(tk, 512)


def _attn_fwd(name, q, kv, kr):
    n_rows = q.shape[0]
    tq, tk, sub = _attn_tiles(n_rows)
    nk = n_rows // tk

    def body(q_ref, kn_ref, v_ref, kr_ref, o_ref, lse_ref, m_sc, l_sc, acc):
        j = pl.program_id(2)

        @pl.when(j == 0)
        def _():
            m_sc[...] = jnp.full(m_sc.shape, -jnp.inf, F32)
            l_sc[...] = jnp.zeros(l_sc.shape, F32)
            acc[...] = jnp.zeros(acc.shape, F32)

        qv = q_ref[...]
        for c in range(tk // sub):
            rows = slice(c * sub, (c + 1) * sub)
            k = jnp.concatenate([kn_ref[rows, :], kr_ref[rows, :]], axis=1)
            s = lax.dot_general(qv, k, _NT, preferred_element_type=F32)
            m_prev = m_sc[...]
            m_new = jnp.maximum(m_prev, jnp.max(s, axis=1, keepdims=True))
            p = jnp.exp(s - m_new)
            alpha = jnp.exp(m_prev - m_new)
            l_sc[...] = alpha * l_sc[...] + jnp.sum(p, axis=1, keepdims=True)
            acc[...] = alpha * acc[...] + jnp.dot(p.astype(BF16), v_ref[rows, :], preferred_element_type=F32)
            m_sc[...] = m_new

        @pl.when(j == nk - 1)
        def _():
            o_ref[...] = acc[...] / l_sc[...]
            lse_ref[...] = jnp.broadcast_to(m_sc[...] + jnp.log(l_sc[...]), lse_ref.shape)

    return pl.pallas_call(
        body, name=name, grid=(MLA_HEADS, n_rows // tq, nk),
        in_specs=[pl.BlockSpec((tq, 256), lambda h, i, j: (i, h)),
                  pl.BlockSpec((tk, 128), lambda h, i, j: (j, 2 * h)),
                  pl.BlockSpec((tk, 128), lambda h, i, j: (j, 2 * h + 1)),
                  pl.BlockSpec((tk, 128), lambda h, i, j: (j, 0))],
        out_specs=[pl.BlockSpec((tq, 128), lambda h, i, j: (i, h)),
                   pl.BlockSpec((None, tq, 128), lambda h, i, j: (h, i, 0))],
        out_shape=[jax.ShapeDtypeStruct((n_rows, MLA_HEADS * MLA_V), F32),
                   jax.ShapeDtypeStruct((MLA_HEADS, n_rows, LANES), F32)],
        scratch_shapes=[pltpu.VMEM((tq, 1), F32), pltpu.VMEM((tq, 1), F32), pltpu.VMEM((tq, 128), F32)],
        compiler_params=_params(("parallel", "parallel", "arbitrary")))(q, kv, kv, kr)


def _attn_bwd(name, q, kv, kr, o, lse, do):
    n_rows = q.shape[0]
    tq, tk, sub = _attn_tiles(n_rows)
    nq = n_rows // tq

    def body(q_ref, kn_ref, v_ref, kr_ref, o_ref, lse_ref, do_ref, dq_ref, dkv_ref, dkr_ref, dk_acc, dv_acc):
        j, i = pl.program_id(1), pl.program_id(2)

        @pl.when(i == 0)
        def _():
            dk_acc[...] = jnp.zeros(dk_acc.shape, F32)
            dv_acc[...] = jnp.zeros(dv_acc.shape, F32)

        @pl.when((i == 0) & (j == 0))
        def _():
            dq_ref[...] = jnp.zeros(dq_ref.shape, F32)

        qv = q_ref[...]
        do = do_ref[...]
        do_b = do.astype(BF16)
        delta = jnp.sum(do * o_ref[...], axis=1, keepdims=True)
        lse_col = lse_ref[...][:, :1]
        dq = None
        for c in range(tk // sub):
            rows = slice(c * sub, (c + 1) * sub)
            k = jnp.concatenate([kn_ref[rows, :], kr_ref[rows, :]], axis=1)
            s = lax.dot_general(qv, k, _NT, preferred_element_type=F32)
            p = jnp.exp(s - lse_col)
            dp = lax.dot_general(do_b, v_ref[rows, :], _NT, preferred_element_type=F32)
            ds = (p * (dp - delta)).astype(BF16)
            dv_acc[rows, :] += lax.dot_general(p.astype(BF16), do_b, _TN, preferred_element_type=F32)
            dk_acc[rows, :] += lax.dot_general(ds, qv, _TN, preferred_element_type=F32)
            t = jnp.dot(ds, k, preferred_element_type=F32)
            dq = t if dq is None else dq + t
        r0 = pl.multiple_of(i * tq, tq)
        dq_ref[pl.ds(r0, tq), :] += dq

        @pl.when(i == nq - 1)
        def _():
            dkv_ref[...] = jnp.concatenate([dk_acc[:, :128], dv_acc[...]], axis=1).astype(dkv_ref.dtype)
            dkr_ref[...] = dk_acc[:, 128:]

    return pl.pallas_call(
        body, name=name, grid=(MLA_HEADS, n_rows // tk, nq),
        in_specs=[pl.BlockSpec((tq, 256), lambda h, j, i: (i, h)),
                  pl.BlockSpec((tk, 128), lambda h, j, i: (j, 2 * h)),
                  pl.BlockSpec((tk, 128), lambda h, j, i: (j, 2 * h + 1)),
                  pl.BlockSpec((tk, 128), lambda h, j, i: (j, 0)),
                  pl.BlockSpec((tq, 128), lambda h, j, i: (i, h)),
                  pl.BlockSpec((None, tq, 128), lambda h, j, i: (h, i, 0)),
                  pl.BlockSpec((tq, 128), lambda h, j, i: (i, h))],
        out_specs=[pl.BlockSpec((n_rows, 256), lambda h, j, i: (0, h)),
                   pl.BlockSpec((tk, 256), lambda h, j, i: (j, h)),
                   pl.BlockSpec((tk, 128), lambda h, j, i: (j, h))],
        out_shape=[jax.ShapeDtypeStruct((n_rows, MLA_HEADS * 256), F32), jax.ShapeDtypeStruct((n_rows, MLA_HEADS * 256), BF16),
                   jax.ShapeDtypeStruct((n_rows, MLA_HEADS * 128), F32)],
        scratch_shapes=[pltpu.VMEM((tk, 256), F32), pltpu.VMEM((tk, 128), F32)],
        compiler_params=_params(("parallel", "arbitrary", "arbitrary")))(q, kv, kv, kr, o, lse, do)


_MESH = pl.DeviceIdType.MESH


def _comm_call(name, body, xs, out_shape):
    n = len(xs)
    hbm = pl.BlockSpec(memory_space=pl.ANY)
    return pl.pallas_call(
        body, name=name, out_shape=out_shape, in_specs=[hbm] * n, out_specs=[hbm] * n,
        scratch_shapes=[pltpu.SemaphoreType.DMA((7 * n,)), pltpu.SemaphoreType.DMA((7 * n,)), pltpu.SemaphoreType.DMA((n,))],
    )(*xs)


def _all_gather(name, xs):
    n = len(xs)

    def body(*refs):
        x_refs, out_refs = refs[:n], refs[n:2 * n]
        send_sems, recv_sems, local_sems = refs[2 * n:]
        mx, my, mc = lax.axis_index("x"), lax.axis_index("y"), lax.axis_index("c")
        me, sibling = (mx, my, mc), (mx, my, 1 - mc)
        chips = [(1 - mx, my), (mx, 1 - my), (1 - mx, 1 - my)]

        def copy(a, k, block, to, src=None):
            dst = out_refs[a].at[4 * block[0] + 2 * block[1] + block[2]]
            return pltpu.make_async_remote_copy(
                src_ref=dst if src is None else src, dst_ref=dst, send_sem=send_sems.at[7 * a + k],
                recv_sem=recv_sems.at[7 * a + k], device_id=to, device_id_type=_MESH)

        mine = [pltpu.make_async_copy(x_refs[a], out_refs[a].at[4 * mx + 2 * my + mc], local_sems.at[a]) for a in range(n)]
        for cp in mine:
            cp.start()
        first = []
        for a in range(n):
            first.append(copy(a, 0, me, sibling, src=x_refs[a]))
            first += [copy(a, 1 + j, me, (*chip, mc), src=x_refs[a]) for j, chip in enumerate(chips)]
        for cp in first:
            cp.start()
        passed = []
        for j, chip in enumerate(chips):
            for a in range(n):
                copy(a, 1 + j, (*chip, mc), me).wait_recv()
                passed.append(copy(a, 4 + j, (*chip, mc), sibling))
                passed[-1].start()
        for a in range(n):
            copy(a, 0, sibling, me).wait_recv()
            for j, chip in enumerate(chips):
                copy(a, 4 + j, (*chip, 1 - mc), me).wait_recv()
        for cp in first + passed:
            cp.wait_send()
        for cp in mine:
            cp.wait()

    return _comm_call(name, body, xs, [jax.ShapeDtypeStruct((N_DEV,) + x.shape, x.dtype) for x in xs])


def _all_to_all(name, gs):
    n = len(gs)

    def body(*refs):
        g_refs, land_refs = refs[:n], refs[n:2 * n]
        send_sems, recv_sems, local_sems = refs[2 * n:]
        mx, my, mc = lax.axis_index("x"), lax.axis_index("y"), lax.axis_index("c")
        me = 4 * mx + 2 * my + mc
        mine = [pltpu.make_async_copy(g_refs[a].at[me], land_refs[a].at[me], local_sems.at[a]) for a in range(n)]
        for cp in mine:
            cp.start()
        copies = []
        for k in range(1, N_DEV):
            px = 1 - mx if k & 4 else mx
            py = 1 - my if k & 2 else my
            pc = 1 - mc if k & 1 else mc
            peer = 4 * px + 2 * py + pc
            for a in range(n):
                sems = dict(send_sem=send_sems.at[7 * a + k - 1], recv_sem=recv_sems.at[7 * a + k - 1],
                            device_id=(px, py, pc), device_id_type=_MESH)
                send = pltpu.make_async_remote_copy(src_ref=g_refs[a].at[peer], dst_ref=land_refs[a].at[me], **sems)
                recv = pltpu.make_async_remote_copy(src_ref=g_refs[a].at[peer], dst_ref=land_refs[a].at[peer], **sems)
                send.start()
                copies.append((send, recv))
        for _, recv in copies:
            recv.wait_recv()
        for send, _ in copies:
            send.wait_send()
        for cp in mine:
            cp.wait()

    return _comm_call(name, body, gs, [jax.ShapeDtypeStruct(g.shape, g.dtype) for g in gs])


def _adamw(name, parts, w, m, v, tr):
    rows, cols = w.shape

    def body(p_ref, w_ref, m_ref, v_ref, g_ref, d_ref, nm_ref, nv_ref):
        g = p_ref[0].astype(F32)
        for d in range(1, N_DEV):
            g = g + p_ref[d].astype(F32)
        nm = ADAM_B1 * m_ref[...] + (1.0 - ADAM_B1) * g
        nv = ADAM_B2 * v_ref[...] + (1.0 - ADAM_B2) * jnp.square(g)
        m_hat = nm / (1.0 - ADAM_B1 ** ADAM_STEP)
        v_hat = nv / (1.0 - ADAM_B2 ** ADAM_STEP)
        g_ref[...] = g
        d_ref[...] = -ADAM_LR * (m_hat / (jnp.sqrt(v_hat) + ADAM_EPS) + ADAM_WD * w_ref[...])
        nm_ref[...] = nm
        nv_ref[...] = nv

    spec = pl.BlockSpec((tr, cols), lambda i: (i, 0))
    return pl.pallas_call(
        body, name=name, grid=(rows // tr,),
        in_specs=[pl.BlockSpec((N_DEV, tr, cols), lambda i: (0, i, 0)), spec, spec, spec],
        out_specs=[spec] * 4, out_shape=[jax.ShapeDtypeStruct((rows, cols), F32)] * 4,
        compiler_params=_params(("parallel",)))(parts, w, m, v)


def _in_pieces():
    p = []
    for h in range(RET_HEADS):
        p += [(128 * h, 128 * h + 128), (512 + 128 * h, 512 + 128 * h + 128), (1024 + 256 * h, 1024 + 256 * h + 256)]
    p += [(2048, 3072), (3776, 4800), (4800, 7872), (3072, 3456), 128, (3456, 3712),
          (3712, 3744), 32, (3744, 3776), 32, 128]
    return p


def _uq_pieces():
    p = []
    for h in range(MLA_HEADS):
        b = 192 * h
        p += [(b, b + 128), (b + 128, b + 160), 32, (b + 160, b + 192), 32]
    return p


def _perm(w, pieces):
    cols = [jnp.zeros(w.shape[:-1] + (p,), w.dtype) if isinstance(p, int) else w[..., p[0]:p[1]] for p in pieces]
    return jnp.concatenate(cols, axis=-1)


def _unperm(dw, pieces):
    found, off = [], 0
    for p in pieces:
        if isinstance(p, int):
            off += p
        else:
            found.append((p[0], dw[..., off:off + p[1] - p[0]]))
            off += p[1] - p[0]
    return jnp.concatenate([t for _, t in sorted(found, key=lambda s: s[0])], axis=-1)


def _unshard(blocks, axis):
    return jnp.concatenate([blocks[p] for p in range(N_DEV)], axis=axis)


def _shard_split(full, axis):
    return jnp.stack(jnp.split(full, N_DEV, axis=axis), axis=0)


def _row_tile(rows, cap=256, unit=16):
    return max(t for t in range(unit, cap + 1, unit) if rows % t == 0)


def _rope_tables(seq):
    pos = jnp.arange(seq, dtype=F32)[:, None]

    def table(dim):
        inv = 1.0 / (ROPE_THETA ** (jnp.arange(0, dim, 2, dtype=F32) / dim))
        ang = pos * inv[None, :]
        return jnp.cos(ang), jnp.sin(ang)

    cr, sr = table(RET_QK)
    cm, sm = table(MLA_ROPE)
    z = jnp.zeros_like(cm)
    return (jnp.concatenate([cr, cr], 1), jnp.concatenate([-sr, sr], 1),
            jnp.concatenate([cm, z, cm, z], 1), jnp.concatenate([-sm, z, sm, z], 1))


def _s5_maps(a_re, a_im, log_dt, b_re, b_im, c_re, c_im):
    dt = jnp.exp(log_dt)[:, None]
    ar = jnp.minimum(a_re, -1e-4)
    mag = jnp.exp(dt * ar)
    abar_re = mag * jnp.cos(dt * a_im)
    abar_im = mag * jnp.sin(dt * a_im)
    den = ar * ar + a_im * a_im
    nr = abar_re - 1.0
    ni = abar_im
    coef_re = (nr * ar + ni * a_im) / den
    coef_im = (ni * ar - nr * a_im) / den
    bb_re = coef_re[..., None] * b_re - coef_im[..., None] * b_im
    bb_im = coef_re[..., None] * b_im + coef_im[..., None] * b_re
    eye = jnp.eye(S5_BLOCKS, dtype=F32)

    def in_blocks(bb):
        t = bb.transpose(0, 2, 1).reshape(S5_BLOCKS, 8, S5_GROUP, S5_STATE)
        return jnp.einsum('jgcp,gh->jgchp', t, eye).reshape(S5_BLOCKS, 128, 512)

    def out_blocks(cc):
        t = cc.transpose(0, 2, 1).reshape(S5_BLOCKS, 8, S5_STATE, S5_GROUP)
        return jnp.einsum('jgpc,gh->jgphc', t, eye).reshape(S5_BLOCKS, 512, 128)

    arow = jnp.concatenate([abar_re.reshape(S5_BLOCKS, 512), abar_im.reshape(S5_BLOCKS, 512)], axis=1).reshape(1, -1)
    b_map = jnp.concatenate([in_blocks(bb_re), in_blocks(bb_im)], axis=2)
    c_map = jnp.concatenate([out_blocks(c_re), -out_blocks(c_im)], axis=1)
    return arow, b_map, c_map


def _power_tables(arow, conj, reverse):
    a = arow.reshape(S5_BLOCKS, 2, 512)
    ar, ai = a[:, 0], (-a[:, 1] if conj else a[:, 1])
    pw = [(ar, ai)]
    for _ in range(SUBLANES - 1):
        pr, pi = pw[-1]
        pw.append((pr * ar - pi * ai, pr * ai + pi * ar))

    def rows(sel):
        return jnp.stack([jnp.stack(list(pw[i]), axis=1) for i in sel], axis=0).reshape(len(sel), -1)

    ak = jnp.concatenate([rows([0, 1, 3]), jnp.zeros((SUBLANES - 3, arow.shape[1]), F32)], axis=0)
    order = list(range(SUBLANES))
    apow = rows(order[::-1] if reverse else order)
    return ak, apow


def _rows(arr):
    return (arr, arr.shape[1], 0)


def _vjp_rows(f, n_prim):
    def fn(r, c, _):
        _, vjp = jax.vjp(f, *r[:n_prim])
        return list(vjp(r[n_prim])), []
    return fn


def _norm_bwd(r, c, _):
    _, vjp = jax.vjp(_f_rms, r[0], c[0])
    dx, dg = vjp(r[1])
    return [dx + r[2]], [dg]


def _layer_fwd(l, x, W, P, T):
    n = x.shape[0]
    tm, tmw = _pick(n, 256), _pick(n, 128)
    cos_r, sin_r, cos_m, sin_m = T

    def nm(s):
        return f"l{l}_{s}"

    def one(name, f, rows, consts, width, dtype=BF16, tile=tm):
        return _tile_call(nm(name), lambda r, c, _: ([f(r, c)], []), n, tile, rows, consts, [(width, dtype)])[0]

    h = one("norm1", lambda r, c: _f_rms(r[0], c[0]), [_rows(x)], [P["norm1_g"]], D_MODEL)
    z = _mm(nm("in_proj"), h, W["in"])
    of, stf = _ret_fwd(nm("ret_f"), z, cos_r, sin_r, P["rd"][0], reverse=False)
    ob, stb = _ret_fwd(nm("ret_b"), z, cos_r, sin_r, P["rd"][1], reverse=True)

    def gn(r, c, _):
        yraw = r[0] + r[1]
        ys = [_f_gn_gate(yraw[:, RET_V * i:RET_V * (i + 1)], r[2][:, RET_V * i:RET_V * (i + 1)],
                         c[0][:, RET_V * i:RET_V * (i + 1)]) for i in range(RET_HEADS)]
        return [yraw, jnp.concatenate(ys, axis=1)], []

    yraw, yret = _tile_call(nm("ret_gn"), gn, n, tm, [_rows(of), _rows(ob), (z, 1024, Z_RG // 1024)], [P["ret_gn_g"]],
                            [(1024, F32), (1024, BF16)])

    cqn = one("q_norm", lambda r, c: _f_rms(r[0][:, :MLA_Q_LORA], c[0]), [(z, 512, Z_CQ // 512)], [P["mla_q_norm_g"]], MLA_Q_LORA)
    ckvn = one("kv_norm", lambda r, c: _f_rms(r[0], c[0]), [(z, 256, Z_CKV // 256)], [P["mla_kv_norm_g"]], MLA_KV_LORA)
    qraw = _mm(nm("q_up"), cqn, W["uq"])
    kv = _mm(nm("kv_up"), ckvn, W["ukv"], out_dtype=BF16)
    q = one("q_rope", lambda r, c: _mla_q(r[0], r[1], r[2], False), [_rows(qraw), _rows(cos_m), _rows(sin_m)], [], 2048)
    kr = one("k_rope", lambda r, c: _rope(r[0], r[1], r[2]), [(z, 128, Z_KR // 128), _rows(cos_m), _rows(sin_m)], [], 128)
    o, lse = _attn_fwd(nm("attn"), q, kv, kr)

    xs, y_dir = [], []
    for d in range(2):
        x_d, y_d = _s5_fwd(nm(f"s5_scan{d}"), z, P["s5"][d], reverse=(d == 1))
        xs.append(x_d)
        y_dir.append(y_d)

    def s5_act(r, c, _):
        ysum = r[0] + r[1]
        return [ysum, _f_s5_act(ysum, r[2], c[0])], []

    ysum, gact = _tile_call(nm("s5_act"), s5_act, n, tm, [_rows(y_dir[0]), _rows(y_dir[1]), (z, 1024, Z_U // 1024)], [P["s5_d"]],
                            [(1024, F32), (1024, BF16)])
    ga = _mm(nm("glu_a"), gact, W["glu_a"])
    gb = _mm(nm("glu_b"), gact, W["glu_b"])
    ys5 = one("glu", lambda r, c: _f_glu(r[0], r[1]), [_rows(ga), _rows(gb)], [], 1024)

    ys = (yret, o, ys5)
    ps = [_mm(nm(f"branch{i}"), ys[i], W["branch"][i]) for i in range(3)]
    mix = one("mix", lambda r, c: _f_gate(r[0], r[3]) + _f_gate(r[1], r[4]) + _f_gate(r[2], r[5]),
              [(z, 1024, Z_GATE // 1024 + i) for i in range(3)] + [_rows(p) for p in ps], [], 1024)
    x1 = _mm(nm("out_proj"), mix, W["out"], add=x)
    h2 = one("norm2", lambda r, c: _f_rms(r[0], c[0]), [_rows(x1)], [P["norm2_g"]], D_MODEL)
    gp = _mm(nm("ffn_g"), h2, W["ffn_g"])
    up = _mm(nm("ffn_u"), h2, W["ffn_u"])
    act = one("swiglu", lambda r, c: _f_swiglu(r[0], r[1]), [_rows(gp), _rows(up)], [], FFN_HIDDEN, tile=tmw)
    x2 = _mm(nm("ffn_down"), act, W["ffn_down"], add=x1)
    saved = dict(x=x, h=h, z=z, stf=stf, stb=stb, yraw=yraw, ys=ys, cqn=cqn, ckvn=ckvn, qraw=qraw, q=q, kv=kv, kr=kr,
                 lse=lse, xs=xs, ysum=ysum, gact=gact, ga=ga, gb=gb, ps=ps, mix=mix, x1=x1, h2=h2, gp=gp, up=up, act=act)
    return x2, saved


def _layer_bwd(l, dx2, sv, W, P, T):
    n = dx2.shape[0]
    tm, tmw = _pick(n, 256), _pick(n, 128)
    cos_r, sin_r, cos_m, sin_m = T
    z = sv["z"]
    g = {}

    def nm(s):
        return f"l{l}_{s}"

    dact = _mm(nm("d_act"), dx2, W["ffn_down"], tb=True)
    g["ffn_down"] = _mm(nm("dw_ffn_down"), sv["act"], dx2, ta=True)
    dgp, dup = _tile_call(nm("d_swiglu"), _vjp_rows(_f_swiglu, 2), n, tmw, [_rows(sv["gp"]), _rows(sv["up"]), _rows(dact)], [],
                          [(FFN_HIDDEN, BF16)] * 2)
    dh2 = _mm(nm("d_h2_g"), dgp, W["ffn_g"], tb=True)
    dh2 = _mm(nm("d_h2_u"), dup, W["ffn_u"], tb=True, add=dh2)
    g["ffn_g"] = _mm(nm("dw_ffn_g"), sv["h2"], dgp, ta=True)
    g["ffn_u"] = _mm(nm("dw_ffn_u"), sv["h2"], dup, ta=True)
    dx1, g["norm2_g"] = _tile_call(nm("d_norm2"), _norm_bwd, n, tm, [_rows(sv["x1"]), _rows(dh2), _rows(dx2)], [P["norm2_g"]],
                                   [(D_MODEL, F32)], acc_outs=[(1, D_MODEL)])

    dmix = _mm(nm("d_mix"), dx1, W["out"], tb=True)
    g["out"] = _mm(nm("dw_out"), sv["mix"], dx1, ta=True)
    dz = jnp.zeros((n, ZW), BF16)
    dys, g["branch"] = [], []
    for i in range(3):
        dz, dp = _tile_call(nm(f"d_gate{i}"), _vjp_rows(_f_gate, 2), n, tm,
                            [(z, 1024, Z_GATE // 1024 + i), _rows(sv["ps"][i]), _rows(dmix)], [], [(1024, BF16)],
                            alias=(dz, 1024, Z_GATE // 1024 + i))
        dys.append(_mm(nm(f"d_branch{i}"), dp, W["branch"][i], tb=True))
        g["branch"].append(_mm(nm(f"dw_branch{i}"), sv["ys"][i], dp, ta=True))

    dga, dgb = _tile_call(nm("d_glu"), _vjp_rows(_f_glu, 2), n, tm, [_rows(sv["ga"]), _rows(sv["gb"]), _rows(dys[2])], [],
                          [(1024, BF16)] * 2)
    dgact = _mm(nm("d_gact_a"), dga, W["glu_a"], tb=True)
    dgact = _mm(nm("d_gact_b"), dgb, W["glu_b"], tb=True, add=dgact)
    g["glu_a"] = _mm(nm("dw_glu_a"), sv["gact"], dga, ta=True)
    g["glu_b"] = _mm(nm("dw_glu_b"), sv["gact"], dgb, ta=True)

    def act_bwd(r, c, _):
        _, vjp = jax.vjp(_f_s5_act, r[0], r[1], c[0])
        dy, du, dd = vjp(r[2])
        return [dy, du], [dd]

    dysum, du_part, g["s5_d"] = _tile_call(nm("d_s5_act"), act_bwd, n, tm,
                                           [_rows(sv["ysum"]), (z, 1024, Z_U // 1024), _rows(dgact)], [P["s5_d"]],
                                           [(1024, F32)] * 2, acc_outs=[(1, 1024)])
    dus, g["s5"] = [], []
    for d in range(2):
        du, g_b, g_c, da = _s5_bwd(nm(f"d_s5_scan{d}"), z, dysum, sv["xs"][d], P["s5"][d], reverse=(d == 0))
        dus.append(du)
        g["s5"].append((da[:1], g_b, g_c))
    dz, = _tile_call(nm("d_s5_u"), lambda r, c, _: ([r[0] + r[1] + r[2]], []), n, tm,
                     [_rows(du_part), _rows(dus[0]), _rows(dus[1])], [], [], alias=(dz, 1024, Z_U // 1024))

    o = sv["ys"][1]
    dq, dkv, dkr = _attn_bwd(nm("d_attn"), sv["q"], sv["kv"], sv["kr"], o, sv["lse"], dys[1])
    dqraw, = _tile_call(nm("d_q_rope"), lambda r, c, _: ([_mla_q(r[0], r[1], r[2], True)], []), n, tm,
                        [_rows(dq), _rows(cos_m), _rows(sin_m)], [], [(2048, BF16)])

    def kr_bwd(r, c, _):
        tot = r[0][:, :128]
        for h in range(1, MLA_HEADS):
            tot = tot + r[0][:, 128 * h:128 * (h + 1)]
        return [_rope_t(tot, r[1], r[2])], []

    dz, = _tile_call(nm("d_k_rope"), kr_bwd, n, tm, [_rows(dkr), _rows(cos_m), _rows(sin_m)], [], [],
                     alias=(dz, 128, Z_KR // 128))
    dcqn = _mm(nm("d_cqn"), dqraw, W["uq"], tb=True)
    g["uq"] = _mm(nm("dw_uq"), sv["cqn"], dqraw, ta=True)
    dckvn = _mm(nm("d_ckvn"), dkv, W["ukv"], tb=True)
    g["ukv"] = _mm(nm("dw_ukv"), sv["ckvn"], dkv, ta=True)

    def qn_bwd(r, c, _):
        _, vjp = jax.vjp(_f_rms, r[0][:, :MLA_Q_LORA], c[0])
        da, dg = vjp(r[1])
        return [jnp.concatenate([da, jnp.zeros((da.shape[0], 512 - MLA_Q_LORA), F32)], axis=1)], [dg]

    dz, g["mla_q_norm_g"] = _tile_call(nm("d_q_norm"), qn_bwd, n, tm, [(z, 512, Z_CQ // 512), _rows(dcqn)], [P["mla_q_norm_g"]],
                                       [], acc_outs=[(1, MLA_Q_LORA)], alias=(dz, 512, Z_CQ // 512))

    def kvn_bwd(r, c, _):
        _, vjp = jax.vjp(_f_rms, r[0], c[0])
        da, dg = vjp(r[1])
        return [da], [dg]

    dz, g["mla_kv_norm_g"] = _tile_call(nm("d_kv_norm"), kvn_bwd, n, tm, [(z, 256, Z_CKV // 256), _rows(dckvn)],
                                        [P["mla_kv_norm_g"]], [], acc_outs=[(1, MLA_KV_LORA)], alias=(dz, 256, Z_CKV // 256))

    def gn_bwd(r, c, _):
        drg, dy, dg = [], [], []
        for i in range(RET_HEADS):
            sl = slice(RET_V * i, RET_V * (i + 1))
            _, vjp = jax.vjp(_f_gn_gate, r[0][:, sl], r[1][:, sl], c[0][:, sl])
            a, b, e = vjp(r[2][:, sl])
            dy.append(a)
            drg.append(b)
            dg.append(e)
        return [jnp.concatenate(drg, axis=1), jnp.concatenate(dy, axis=1)], [jnp.concatenate(dg, axis=1)]

    dz, dyraw, g["ret_gn_g"] = _tile_call(nm("d_ret_gn"), gn_bwd, n, tm,
                                          [_rows(sv["yraw"]), (z, 1024, Z_RG // 1024), _rows(dys[0])], [P["ret_gn_g"]],
                                          [(1024, F32)], acc_outs=[(1, 1024)], alias=(dz, 1024, Z_RG // 1024))
    dz, drd_f = _ret_bwd(nm("d_ret_f"), z, cos_r, sin_r, P["rd"][0], sv["stf"], dyraw, dz, reverse=False)
    dz, drd_b = _ret_bwd(nm("d_ret_b"), z, cos_r, sin_r, P["rd"][1], sv["stb"], dyraw, dz, reverse=True)
    g["ret_decay"] = jnp.stack([drd_f[:, 0, 0], drd_b[:, 0, 0]], axis=0)

    dh = _mm(nm("d_h"), dz, W["in"], tb=True)
    g["in"] = _mm(nm("dw_in"), sv["h"], dz, ta=True)
    dx, g["norm1_g"] = _tile_call(nm("d_norm1"), _norm_bwd, n, tm, [_rows(sv["x"]), _rows(dh), _rows(dx1)], [P["norm1_g"]],
                                  [(D_MODEL, F32)], acc_outs=[(1, D_MODEL)])
    return dx, g


INPUT_NAMES = ("x",) + WEIGHTS + ("loss_target",) + tuple("m_" + n for n in WEIGHTS) + tuple("v_" + n for n in WEIGHTS)
S5_NAMES = ("s5_a_re", "s5_a_im", "s5_log_dt", "s5_b_re", "s5_b_im", "s5_c_re", "s5_c_im")


def _pack_rows(arrays, tile_rows):
    flat = jnp.concatenate([a.reshape(-1) for a in arrays])
    pad = -flat.shape[0] % (tile_rows * LANES)
    if pad:
        flat = jnp.concatenate([flat, jnp.zeros((pad,), flat.dtype)])
    return flat.reshape(-1, LANES)


def _unpack_rows(packed, shapes):
    flat, out, off = packed.reshape(-1), [], 0
    for s in shapes:
        size = math.prod(s)
        out.append(flat[off:off + size].reshape(s))
        off += size
    return out


def _local_step(inp, full, x, target):
    n = x.shape[0]
    tables = _rope_tables(n)
    in_pieces, uq_pieces = _in_pieces(), _uq_pieces()
    Ws, Ps, s5_vjps = [], [], []
    for l in range(DEPTH):
        Ws.append(dict(
            **{"in": _perm(full["w_in"][l], in_pieces)}, uq=_perm(full["mla_w_uq"][l], uq_pieces), ukv=full["mla_w_ukv"][l],
            glu_a=full["s5_w_glu"][l][:, :1024], glu_b=full["s5_w_glu"][l][:, 1024:],
            branch=[full["w_branch"][l, i] for i in range(3)], out=full["w_out"][l],
            ffn_g=full["ffn_w_gu"][l][:, :FFN_HIDDEN], ffn_u=full["ffn_w_gu"][l][:, FFN_HIDDEN:],
            ffn_down=full["ffn_w_down"][l]))
        s5, vjps = [], []
        for d in range(2):
            (arow, b_map, c_map), vjp = jax.vjp(_s5_maps, *[inp[k][l, d] for k in S5_NAMES])
            arow = lax.stop_gradient(arow)
            ak, apow = _power_tables(arow, False, d == 1)
            ak_adj, apow_adj = _power_tables(arow, True, d == 0)
            s5.append(dict(b_map=b_map, c_map=c_map, b_map_t=b_map.transpose(0, 2, 1), c_map_t=c_map.transpose(0, 2, 1),
                           ak=ak, apow=apow, ak_adj=ak_adj, apow_adj=apow_adj))
            vjps.append(vjp)
        s5_vjps.append(vjps)
        Ps.append(dict(
            norm1_g=inp["norm1_g"][l][None], norm2_g=inp["norm2_g"][l][None], ret_gn_g=inp["ret_gn_g"][l][None],
            mla_q_norm_g=inp["mla_q_norm_g"][l][None], mla_kv_norm_g=inp["mla_kv_norm_g"][l][None], s5_d=inp["s5_d"][l][None],
            rd=[jnp.broadcast_to(inp["ret_decay"][l, d][:, None, None], (RET_HEADS, 1, LANES)) for d in range(2)], s5=s5))

    h, saved = x, []
    for l in range(DEPTH):
        h, sv = _layer_fwd(l, h, Ws[l], Ps[l], tables)
        saved.append(sv)

    def loss_bwd(r, c, _):
        loss, vjp = jax.vjp(lambda a, gain: _f_loss(a, gain, r[1]), r[0], c[0])
        da, dg = vjp(jnp.ones((), F32))
        return [da], [dg, jnp.broadcast_to(loss, (1, LANES))]

    dh, g_final, loss_row = _tile_call("loss", loss_bwd, n, _pick(n, 256), [_rows(h), _rows(target)], [inp["final_g"][None]],
                                       [(D_MODEL, F32)], acc_outs=[(1, D_MODEL), (1, LANES)])
    layer_g = [None] * DEPTH
    for l in reversed(range(DEPTH)):
        dh, layer_g[l] = _layer_bwd(l, dh, saved[l], Ws[l], Ps[l], tables)

    def stack(f):
        return jnp.stack([f(layer_g[l], l) for l in range(DEPTH)], axis=0)

    grads = dict(
        w_in=stack(lambda g, l: _unperm(g["in"], in_pieces)), mla_w_uq=stack(lambda g, l: _unperm(g["uq"], uq_pieces)),
        mla_w_ukv=stack(lambda g, l: g["ukv"]), s5_w_glu=stack(lambda g, l: jnp.concatenate([g["glu_a"], g["glu_b"]], axis=1)),
        w_branch=stack(lambda g, l: jnp.stack(g["branch"], axis=0)), w_out=stack(lambda g, l: g["out"]),
        ffn_w_gu=stack(lambda g, l: jnp.concatenate([g["ffn_g"], g["ffn_u"]], axis=1)),
        ffn_w_down=stack(lambda g, l: g["ffn_down"]),
        norm1_g=stack(lambda g, l: g["norm1_g"][0]), norm2_g=stack(lambda g, l: g["norm2_g"][0]),
        ret_gn_g=stack(lambda g, l: g["ret_gn_g"][0]), mla_q_norm_g=stack(lambda g, l: g["mla_q_norm_g"][0]),
        mla_kv_norm_g=stack(lambda g, l: g["mla_kv_norm_g"][0]), s5_d=stack(lambda g, l: g["s5_d"][0]),
        ret_decay=stack(lambda g, l: g["ret_decay"]), final_g=g_final[0])
    s5_grads = [[s5_vjps[l][d](layer_g[l]["s5"][d]) for d in range(2)] for l in range(DEPTH)]
    for i, k in enumerate(S5_NAMES):
        grads[k] = jnp.stack([jnp.stack([s5_grads[l][d][i] for d in range(2)], axis=0) for l in range(DEPTH)], axis=0)
    return loss_row[0, 0], dh, grads


def kernel(*args):
    inp = dict(zip(INPUT_NAMES, args))

    kinds = ("grad_", "delta_", "new_m_", "new_v_")

    gathered = _all_gather("gather_weights", [inp[k].astype(BF16) for k in SHARDED])
    full = {k: _unshard(g, SHARD_AXIS[k]) for k, g in zip(SHARDED, gathered)}

    loss, dh, grads = _local_step(inp, full, inp["x"][0], inp["loss_target"][0])
    loss = lax.psum(loss, ("x", "y", "c"))

    landed = _all_to_all("exchange_grads", [_shard_split(grads[k], SHARD_AXIS[k]).astype(BF16) for k in SHARDED])
    out = {}
    for k, land in zip(SHARDED, landed):
        shape = inp[k].shape
        rows, cols = math.prod(shape[:-1]), shape[-1]
        res = _adamw("adamw_" + k, land.reshape(N_DEV, rows, cols), *[inp[p + k].reshape(rows, cols) for p in ("", "m_", "v_")],
                     _row_tile(rows))
        for kind, t in zip(kinds, res):
            out[kind + k] = t.reshape(shape)

    small_tr = 512
    partial = _pack_rows([grads[k] for k in SMALL], small_tr)
    packed = [_pack_rows([inp[p + k] for k in SMALL], small_tr) for p in ("", "m_", "v_")]
    res_small = _adamw("adamw_small", _all_gather("gather_small_grads", [partial])[0], *packed, small_tr)
    for kind, b in zip(kinds, res_small):
        for k, t in zip(SMALL, _unpack_rows(b, [inp[k].shape for k in SMALL])):
            out[kind + k] = t
    return (loss, dh[None]) + tuple(out[kind + k] for kind in kinds for k in WEIGHTS)
```

```python
import functools
import math

import jax
import jax.numpy as jnp
from jax import lax
from jax.experimental import pallas as pl
from jax.experimental.pallas import tpu as pltpu

F32 = jnp.float32
BF16 = jnp.bfloat16

D_MODEL = 1024
DEPTH = 2
RMS_EPS = 1e-6
GN_EPS = 1e-5
ROPE_THETA = 10000.0
RET_HEADS, RET_QK, RET_V, RET_CHUNK = 4, 128, 256, 128
MLA_HEADS, MLA_Q_LORA, MLA_KV_LORA, MLA_NOPE, MLA_ROPE, MLA_V = 8, 384, 256, 128, 64, 128
S5_GROUPS, S5_GROUP, S5_STATE = 64, 16, 64
S5_BLOCKS = 8
FFN_HIDDEN = 2816
N_DEV = 8
ADAM_LR, ADAM_B1, ADAM_B2, ADAM_EPS, ADAM_WD, ADAM_STEP = 0.001, 0.9, 0.999, 1e-08, 0.01, 10

LANES = 128
SUBLANES = 8
VMEM_LIMIT = 48 * 1024 * 1024

ZW = 8192
Z_RET = 0
Z_RG = 2048
Z_U = 3072
Z_GATE = 4096
Z_CQ = 7168
Z_CKV = 7680
Z_KR = 7936
IN_SPLITS = (512, 512, 1024, 1024, 384, 256, 64, 1024, 3072)

SHARDED = ("w_in", "mla_w_uq", "mla_w_ukv", "s5_w_glu", "w_branch", "w_out", "ffn_w_gu", "ffn_w_down")
SHARD_AXIS = {"w_in": 2, "mla_w_uq": 2, "mla_w_ukv": 2, "s5_w_glu": 2, "w_branch": 2, "w_out": 1,
              "ffn_w_gu": 2, "ffn_w_down": 1}
SMALL = ("norm1_g", "ret_decay", "ret_gn_g", "mla_q_norm_g", "mla_kv_norm_g", "s5_a_re", "s5_a_im", "s5_log_dt",
         "s5_b_re", "s5_b_im", "s5_c_re", "s5_c_im", "s5_d", "norm2_g", "final_g")
WEIGHTS = ("norm1_g", "w_in", "ret_decay", "ret_gn_g", "mla_q_norm_g", "mla_w_uq", "mla_kv_norm_g", "mla_w_ukv",
           "s5_a_re", "s5_a_im", "s5_log_dt", "s5_b_re", "s5_b_im", "s5_c_re", "s5_c_im", "s5_d", "s5_w_glu",
           "w_branch", "w_out", "norm2_g", "ffn_w_gu", "ffn_w_down", "final_g")


def _params(sem=None):
    return pltpu.CompilerParams(dimension_semantics=sem, vmem_limit_bytes=VMEM_LIMIT)


def _pick(n, cap):
    if n <= cap:
        return n
    t = cap - cap % LANES
    while t >= LANES:
        if n % t == 0:
            return t
        t -= LANES
    return n


@functools.partial(jax.custom_vjp, nondiff_argnums=(2, 3))
def _bdot(a, b, ca, cb):
    return lax.dot_general(a.astype(BF16), b.astype(BF16), (((ca,), (cb,)), ((), ())), preferred_element_type=F32)


def _bdot_fwd(a, b, ca, cb):
    return _bdot(a, b, ca, cb), (a, b)


def _bdot_bwd(ca, cb, res, g):
    a, b = res
    da = _bdot(g, b, 1, 1 - cb) if ca == 1 else _bdot(b, g, 1 - cb, 1)
    db = _bdot(a, g, 1 - ca, 0) if cb == 0 else _bdot(g, a, 0, 1 - ca)
    return da, db


_bdot.defvjp(_bdot_fwd, _bdot_bwd)


@jax.custom_vjp
def _swap_halves(x):
    return pltpu.roll(x, LANES // 2, 1)


_swap_halves.defvjp(lambda x: (_swap_halves(x), None), lambda _, g: (_swap_halves(g),))


def _rope(x, cosf, sinf):
    return x * cosf + _swap_halves(x) * sinf


def _f_rms(x, g):
    return x * lax.rsqrt(jnp.mean(x * x, axis=-1, keepdims=True) + RMS_EPS) * g


def _rope_t(g, cosf, sinf):
    return g * cosf + _swap_halves(g * sinf)


def _f_gn_gate(yh, rgh, gh):
    mu = jnp.mean(yh, axis=-1, keepdims=True)
    var = jnp.mean(jnp.square(yh - mu), axis=-1, keepdims=True)
    return jax.nn.silu(rgh) * ((yh - mu) * lax.rsqrt(var + GN_EPS) * gh)


MLA_SCALE = (MLA_NOPE + MLA_ROPE) ** -0.5


def _mla_q(qraw, cosf, sinf, transpose):
    parts = []
    for h in range(MLA_HEADS):
        parts.append(qraw[:, 256 * h:256 * h + 128] * MLA_SCALE)
        r = qraw[:, 256 * h + 128:256 * h + 256]
        parts.append(_rope_t(r * MLA_SCALE, cosf, sinf) if transpose else _rope(r, cosf, sinf) * MLA_SCALE)
    return jnp.concatenate(parts, axis=1)


def _f_s5_act(ysum, u, d):
    return jax.nn.gelu(ysum + d * u)


def _f_glu(ga, gb):
    return ga * jax.nn.sigmoid(gb)


def _f_gate(zg, proj):
    return jax.nn.sigmoid(zg) * proj


def _f_swiglu(gp, up):
    return jax.nn.silu(gp) * up


def _f_loss(x, g, target):
    y = _f_rms(x, g)
    err = jnp.square(y - target)
    return 0.5 * jnp.sum(jnp.mean(err, axis=-1))


def _tile_call(name, fn, n_rows, tm, row_ins, consts, row_outs, acc_outs=(), alias=None):
    n_row_in, n_const = len(row_ins), len(consts)
    args = [a for a, _, _ in row_ins] + list(consts)
    in_specs = [pl.BlockSpec((tm, w), lambda i, cb=cb: (i, cb)) for _, w, cb in row_ins]
    in_specs += [pl.BlockSpec(c.shape, lambda i: (0, 0)) for c in consts]
    out_shape, out_specs, aliases = [], [], {}
    if alias is not None:
        arr, w, cb = alias
        in_specs.append(pl.BlockSpec((tm, w), lambda i, cb=cb: (i, cb)))
        aliases[len(args)] = 0
        args.append(arr)
        out_shape.append(jax.ShapeDtypeStruct(arr.shape, arr.dtype))
        out_specs.append(pl.BlockSpec((tm, w), lambda i, cb=cb: (i, cb)))
    for w, dt in row_outs:
        out_shape.append(jax.ShapeDtypeStruct((n_rows, w), dt))
        out_specs.append(pl.BlockSpec((tm, w), lambda i: (i, 0)))
    n_row_out = len(out_shape)
    for r, w in acc_outs:
        out_shape.append(jax.ShapeDtypeStruct((r, w), F32))
        out_specs.append(pl.BlockSpec((r, w), lambda i: (0, 0)))
    n_in = len(args)

    def body(*refs):
        ins, outs = refs[:n_in], refs[n_in:]
        rows = [r[...] for r in ins[:n_row_in]]
        cons = [r[...] for r in ins[n_row_in:n_row_in + n_const]]
        prev = ins[-1][...] if alias is not None else None
        res_rows, res_accs = fn(rows, cons, prev)
        for o, r in zip(outs[:n_row_out], res_rows):
            o[...] = r.astype(o.dtype)
        if acc_outs:
            @pl.when(pl.program_id(0) == 0)
            def _():
                for o in outs[n_row_out:]:
                    o[...] = jnp.zeros(o.shape, F32)
            for o, r in zip(outs[n_row_out:], res_accs):
                o[...] += r

    return pl.pallas_call(
        body, name=name, grid=(n_rows // tm,), in_specs=in_specs, out_specs=out_specs, out_shape=out_shape,
        input_output_aliases=aliases, compiler_params=_params(("arbitrary",)))(*args)


def _mm(name, a, b, *, ta=False, tb=False, add=None, out_dtype=F32):
    (K, M) = a.shape if ta else a.shape[::-1]
    (N, K2) = b.shape if tb else b.shape[::-1]
    assert K == K2, (name, a.shape, b.shape)
    tm, tn, tk = _pick(M, 1536), _pick(N, 1536), _pick(K, 1024)
    gi, gj, nk = M // tm, N // tn, K // tk
    a_bytes, b_bytes = a.size * a.dtype.itemsize, b.size * b.dtype.itemsize
    j_outer = nk == 1 and b_bytes + a_bytes * gj < a_bytes + b_bytes * gi

    def im(f):
        return (lambda g0, g1, k: f(g1, g0, k)) if j_outer else f

    a_spec = pl.BlockSpec((tk, tm), im(lambda i, j, k: (k, i))) if ta else pl.BlockSpec((tm, tk), im(lambda i, j, k: (i, k)))
    b_spec = pl.BlockSpec((tn, tk), im(lambda i, j, k: (j, k))) if tb else pl.BlockSpec((tk, tn), im(lambda i, j, k: (k, j)))
    o_spec = pl.BlockSpec((tm, tn), im(lambda i, j, k: (i, j)))
    dn = (((0 if ta else 1,), (1 if tb else 0,)), ((), ()))
    has_add = add is not None

    def body(*refs):
        if has_add:
            a_ref, b_ref, add_ref, o_ref, acc = refs
        else:
            a_ref, b_ref, o_ref, acc = refs
        k = pl.program_id(2)

        @pl.when(k == 0)
        def _():
            acc[...] = jnp.zeros(acc.shape, F32)

        acc[...] += lax.dot_general(a_ref[...].astype(BF16), b_ref[...].astype(BF16), dn, preferred_element_type=F32)

        @pl.when(k == nk - 1)
        def _():
            r = acc[...]
            if has_add:
                r = r + add_ref[...]
            o_ref[...] = r.astype(out_dtype)

    args, specs = [a, b], [a_spec, b_spec]
    if has_add:
        args.append(add)
        specs.append(o_spec)
    return pl.pallas_call(
        body, name=name, grid=(gj, gi, nk) if j_outer else (gi, gj, nk), in_specs=specs, out_specs=o_spec,
        out_shape=jax.ShapeDtypeStruct((M, N), out_dtype), scratch_shapes=[pltpu.VMEM((tm, tn), F32)],
        compiler_params=_params(("parallel", "parallel", "arbitrary")))(*args)


S5_BW = 2 * S5_STATE * (S5_GROUPS // S5_BLOCKS)


def _scan_tile(buf, ak_ref, ap_ref, carry, *, reverse, x_ref=None, da_ref=None):
    tt, bw = buf.shape
    hw = bw // 2
    ng = tt // SUBLANES
    rowid = lax.broadcasted_iota(jnp.int32, (SUBLANES, hw), 0)
    steps = [(SUBLANES - k if reverse else k, SUBLANES * n) for n, k in enumerate((1, 2, 4))]
    apr, api = ap_ref[:, :hw], ap_ref[:, hw:]
    first = (rowid == SUBLANES - 1) if reverse else (rowid == 0)

    def group(gi, c):
        cr, ci = c
        r0 = pl.multiple_of(((ng - 1 - gi) if reverse else gi) * SUBLANES, SUBLANES)
        xr, xi = buf[pl.ds(r0, SUBLANES), :hw], buf[pl.ds(r0, SUBLANES), hw:]
        for sh, a0 in steps:
            kr, ki = ak_ref[a0:a0 + SUBLANES, :hw], ak_ref[a0:a0 + SUBLANES, hw:]
            sr, si = pltpu.roll(xr, sh, 0), pltpu.roll(xi, sh, 0)
            xr, xi = xr + kr * sr - ki * si, xi + kr * si + ki * sr
        xr, xi = xr + apr * cr - api * ci, xi + apr * ci + api * cr
        buf[pl.ds(r0, SUBLANES), :hw] = xr
        buf[pl.ds(r0, SUBLANES), hw:] = xi
        if x_ref is not None:
            sh1 = SUBLANES - 1 if reverse else 1
            pr = jnp.where(first, cr, pltpu.roll(xr, sh1, 0))
            pi = jnp.where(first, ci, pltpu.roll(xi, sh1, 0))
            sr, si = x_ref[pl.ds(r0, SUBLANES), :hw], x_ref[pl.ds(r0, SUBLANES), hw:]
            da_ref[:, :hw] += pr * sr + pi * si
            da_ref[:, hw:] += pi * sr - pr * si
        last = 0 if reverse else SUBLANES - 1
        return (jnp.broadcast_to(xr[last:last + 1], (SUBLANES, hw)), jnp.broadcast_to(xi[last:last + 1], (SUBLANES, hw)))

    cr, ci = lax.fori_loop(0, ng, group, (carry[:, :hw], carry[:, hw:]))
    carry[:, :hw] = cr
    carry[:, hw:] = ci


def _s5_specs(n_rows, reverse):
    tt = _pick(n_rows, 256)
    nt = n_rows // tt

    def rows(width, off):
        return pl.BlockSpec((tt, width), lambda j, t: ((nt - 1 - t) if reverse else t, off + j))

    def per_block(r, c):
        return pl.BlockSpec((None, r, c), lambda j, t: (j, 0, 0))

    def par(r):
        return pl.BlockSpec((r, S5_BW), lambda j, t: (0, j))

    return tt, nt, rows, per_block, par


def _s5_fwd(name, z, m, *, reverse):
    n_rows = z.shape[0]
    tt, nt, rows, per_block, par = _s5_specs(n_rows, reverse)

    def body(u_ref, b_ref, c_ref, ak_ref, ap_ref, x_ref, y_ref, carry):
        @pl.when(pl.program_id(1) == 0)
        def _():
            carry[...] = jnp.zeros(carry.shape, F32)

        x_ref[...] = jnp.dot(u_ref[...].astype(BF16), b_ref[...].astype(BF16), preferred_element_type=F32)
        _scan_tile(x_ref, ak_ref, ap_ref, carry, reverse=reverse)
        y_ref[...] = jnp.dot(x_ref[...].astype(BF16), c_ref[...].astype(BF16), preferred_element_type=F32)

    return pl.pallas_call(
        body, name=name, grid=(S5_BLOCKS, nt),
        in_specs=[rows(LANES, Z_U // LANES), per_block(LANES, S5_BW), per_block(S5_BW, LANES), par(3 * SUBLANES), par(SUBLANES)],
        out_specs=[rows(S5_BW, 0), rows(LANES, 0)],
        out_shape=[jax.ShapeDtypeStruct((n_rows, S5_BLOCKS * S5_BW), F32), jax.ShapeDtypeStruct((n_rows, S5_BLOCKS * LANES), F32)],
        scratch_shapes=[pltpu.VMEM((SUBLANES, S5_BW), F32)],
        compiler_params=_params(("parallel", "arbitrary")))(z, m["b_map"], m["c_map"], m["ak"], m["apow"])


def _s5_bwd(name, z, dy, xs, m, *, reverse):
    n_rows = z.shape[0]
    tt, nt, rows, per_block, par = _s5_specs(n_rows, reverse)

    def body(u_ref, dy_ref, x_ref, bt_ref, ct_ref, ak_ref, ap_ref, du_ref, db_ref, dc_ref, da_ref, lam, carry):
        t = pl.program_id(1)

        @pl.when(t == 0)
        def _():
            carry[...] = jnp.zeros(carry.shape, F32)
            db_ref[...] = jnp.zeros(db_ref.shape, F32)
            dc_ref[...] = jnp.zeros(dc_ref.shape, F32)
            da_ref[...] = jnp.zeros(da_ref.shape, F32)

        dy_b = dy_ref[...].astype(BF16)
        lam[...] = jnp.dot(dy_b, ct_ref[...].astype(BF16), preferred_element_type=F32)
        _scan_tile(lam, ak_ref, ap_ref, carry, reverse=reverse, x_ref=x_ref, da_ref=da_ref)
        lam_b = lam[...].astype(BF16)
        du_ref[...] = jnp.dot(lam_b, bt_ref[...].astype(BF16), preferred_element_type=F32)
        db_ref[...] += lax.dot_general(u_ref[...].astype(BF16), lam_b, _TN, preferred_element_type=F32)
        dc_ref[...] += lax.dot_general(x_ref[...].astype(BF16), dy_b, _TN, preferred_element_type=F32)

        @pl.when(t == nt - 1)
        def _():
            da_ref[...] = jnp.broadcast_to(jnp.sum(da_ref[...], axis=0, keepdims=True), da_ref.shape)

    return pl.pallas_call(
        body, name=name, grid=(S5_BLOCKS, nt),
        in_specs=[rows(LANES, Z_U // LANES), rows(LANES, 0), rows(S5_BW, 0), per_block(S5_BW, LANES), per_block(LANES, S5_BW),
                  par(3 * SUBLANES), par(SUBLANES)],
        out_specs=[rows(LANES, 0), per_block(LANES, S5_BW), per_block(S5_BW, LANES), par(SUBLANES)],
        out_shape=[jax.ShapeDtypeStruct((n_rows, S5_BLOCKS * LANES), F32), jax.ShapeDtypeStruct((S5_BLOCKS, LANES, S5_BW), F32),
                   jax.ShapeDtypeStruct((S5_BLOCKS, S5_BW, LANES), F32), jax.ShapeDtypeStruct((SUBLANES, S5_BLOCKS * S5_BW), F32)],
        scratch_shapes=[pltpu.VMEM((tt, S5_BW), F32), pltpu.VMEM((SUBLANES, S5_BW), F32)],
        compiler_params=_params(("parallel", "arbitrary")))(z, dy, xs, m["b_map_t"], m["c_map_t"], m["ak_adj"], m["apow_adj"])


def _ret_chunk(zq, zk, v, cosf, sinf, state, rd, reverse):
    c = RET_CHUNK
    lg = jax.nn.log_sigmoid(rd)
    lg1 = jnp.max(lg, axis=1, keepdims=True)
    q = _rope(zq, cosf, sinf) * (RET_QK ** -0.5)
    k = _rope(zk, cosf, sinf)
    pi = lax.broadcasted_iota(jnp.int32, (c, c), 0).astype(F32)
    pj = lax.broadcasted_iota(jnp.int32, (c, c), 1).astype(F32)
    pcol = lax.broadcasted_iota(jnp.int32, (c, 1), 0).astype(F32)
    if reverse:
        diff, mask, pos = pj - pi, pj > pi, (c - 1) - pcol
    else:
        diff, mask, pos = pi - pj, pi >= pj, pcol
    decay_in = jnp.where(mask, jnp.exp(jnp.where(mask, diff, 0.0) * lg), 0.0)
    scores = _bdot(q, k, 1, 1) * decay_in
    inner = _bdot(scores, v, 1, 0)
    k_w = jnp.exp((c - 1 - pos) * lg1)
    kv = _bdot(k * k_w, v, 0, 0)
    q_w = jnp.exp((pos + 1) * lg1)
    cross = _bdot(q, state, 1, 0) * q_w
    new_state = jnp.exp(c * lg1) * state + kv
    return inner + cross, new_state


RET_ZW = RET_HEADS * (2 * RET_QK + RET_V)


def _ret_specs(n_chunks, reverse_order):
    cmap = (lambda n: n_chunks - 1 - n) if reverse_order else (lambda n: n)
    z_spec = pl.BlockSpec((RET_CHUNK, RET_ZW), lambda n: (cmap(n), 0))
    t_spec = pl.BlockSpec((RET_CHUNK, LANES), lambda n: (cmap(n), 0))
    rd_spec = pl.BlockSpec((RET_HEADS, 1, LANES), lambda n: (0, 0, 0))
    o_spec = pl.BlockSpec((RET_CHUNK, RET_HEADS * RET_V), lambda n: (cmap(n), 0))
    st_spec = pl.BlockSpec((RET_HEADS, None, RET_QK, RET_V), lambda n: (0, cmap(n), 0, 0))
    return z_spec, t_spec, rd_spec, o_spec, st_spec


def _ret_head(zt, h):
    b = h * (2 * RET_QK + RET_V)
    return zt[:, b:b + RET_QK], zt[:, b + RET_QK:b + 2 * RET_QK], zt[:, b + 2 * RET_QK:b + 2 * RET_QK + RET_V]


def _ret_fwd(name, z, cosf, sinf, rd, *, reverse):
    n_rows = z.shape[0]
    n_chunks = n_rows // RET_CHUNK
    z_spec, t_spec, rd_spec, o_spec, st_spec = _ret_specs(n_chunks, reverse)

    def body(z_ref, cos_ref, sin_ref, rd_ref, o_ref, st_ref, state):
        @pl.when(pl.program_id(0) == 0)
        def _():
            state[...] = jnp.zeros(state.shape, F32)

        zt = z_ref[...]
        cosv, sinv = cos_ref[...], sin_ref[...]
        for h in range(RET_HEADS):
            st = state[h]
            st_ref[h] = st
            out, new = _ret_chunk(*_ret_head(zt, h), cosv, sinv, st, rd_ref[h], reverse)
            o_ref[:, RET_V * h:RET_V * (h + 1)] = out
            state[h] = new

    return pl.pallas_call(
        body, name=name, grid=(n_chunks,), in_specs=[z_spec, t_spec, t_spec, rd_spec],
        out_specs=[o_spec, st_spec],
        out_shape=[jax.ShapeDtypeStruct((n_rows, RET_HEADS * RET_V), F32),
                   jax.ShapeDtypeStruct((RET_HEADS, n_chunks, RET_QK, RET_V), F32)],
        scratch_shapes=[pltpu.VMEM((RET_HEADS, RET_QK, RET_V), F32)], compiler_params=_params(("arbitrary",)))(z, cosf, sinf, rd)


def _ret_bwd(name, z, cosf, sinf, rd, states, dout, dz, *, reverse):
    n_rows = z.shape[0]
    n_chunks = n_rows // RET_CHUNK
    z_spec, t_spec, rd_spec, o_spec, st_spec = _ret_specs(n_chunks, not reverse)

    def body(z_ref, cos_ref, sin_ref, rd_ref, st_ref, do_ref, dzin_ref, dz_ref, drd_ref, dstate):
        @pl.when(pl.program_id(0) == 0)
        def _():
            dstate[...] = jnp.zeros(dstate.shape, F32)
            drd_ref[...] = jnp.zeros(drd_ref.shape, F32)

        zt = z_ref[...]
        cosv, sinv = cos_ref[...], sin_ref[...]
        parts = []
        for h in range(RET_HEADS):
            _, vjp = jax.vjp(lambda a, b, c, s, r: _ret_chunk(a, b, c, cosv, sinv, s, r, reverse),
                             *_ret_head(zt, h), st_ref[h], rd_ref[h])
            dq, dk, dv, dst, drd = vjp((do_ref[:, RET_V * h:RET_V * (h + 1)], dstate[h]))
            parts += [dq, dk, dv]
            dstate[h] = dst
            drd_ref[h] += jnp.sum(drd, axis=1, keepdims=True)
        dz_ref[...] = (dzin_ref[...].astype(F32) + jnp.concatenate(parts, axis=1)).astype(dz_ref.dtype)

    return pl.pallas_call(
        body, name=name, grid=(n_chunks,),
        in_specs=[z_spec, t_spec, t_spec, rd_spec, st_spec, o_spec, z_spec],
        out_specs=[z_spec, rd_spec],
        out_shape=[jax.ShapeDtypeStruct(dz.shape, dz.dtype), jax.ShapeDtypeStruct((RET_HEADS, 1, LANES), F32)],
        input_output_aliases={6: 0}, scratch_shapes=[pltpu.VMEM((RET_HEADS, RET_QK, RET_V), F32)],
        compiler_params=_params(("arbitrary",)))(z, cosf, sinf, rd, states, dout, dz)


_NT = (((1,), (1,)), ((), ()))
_TN = (((0,), (0,)), ((), ()))


def _attn_tiles(n_rows, tq_cap):
    tq, tk = _pick(n_rows, tq_cap), _pick(n_rows, 2048)
    return tq, max(tq // 2, LANES), tk, min(tk, 1024)


def _attn_fwd(name, q, kv, kr):
    n_rows = q.shape[0]
    tq, hq, tk, sub = _attn_tiles(n_rows, 512)
    nk = n_rows // tk

    def body(q_ref, kn_ref, v_ref, kr_ref, o_ref, lse_ref, m_sc, acc):
        j = pl.program_id(2)

        @pl.when(j == 0)
        def _():
            m_sc[...] = jnp.full(m_sc.shape, -jnp.inf, F32)
            acc[...] = jnp.zeros(acc.shape, F32)

        for c in range(tk // sub):
            rows = slice(c * sub, (c + 1) * sub)
            k = jnp.concatenate([kn_ref[rows, :], kr_ref[rows, :]], axis=1)
            v1 = jnp.concatenate([v_ref[rows, :], jnp.ones((sub, LANES), BF16)], axis=1)
            for part in range(tq // hq):
                qr = slice(part * hq, (part + 1) * hq)
                s = lax.dot_general(q_ref[qr, :], k, _NT, preferred_element_type=F32)
                m_prev = m_sc[qr, :]
                m_new = jnp.maximum(m_prev, jnp.max(s, axis=1, keepdims=True))
                p = jnp.exp(s - m_new)
                acc[qr, :] = jnp.exp(m_prev - m_new) * acc[qr, :] + jnp.dot(p.astype(BF16), v1, preferred_element_type=F32)
                m_sc[qr, :] = m_new

        @pl.when(j == nk - 1)
        def _():
            l = acc[:, LANES:]
            o_ref[...] = acc[:, :LANES] / l
            lse_ref[...] = m_sc[...] + jnp.log(l)

    return pl.pallas_call(
        body, name=name, grid=(MLA_HEADS, n_rows // tq, nk),
        in_specs=[pl.BlockSpec((tq, 256), lambda h, i, j: (i, h)),
                  pl.BlockSpec((tk, 128), lambda h, i, j: (j, 2 * h)),
                  pl.BlockSpec((tk, 128), lambda h, i, j: (j, 2 * h + 1)),
                  pl.BlockSpec((tk, 128), lambda h, i, j: (j, 0))],
        out_specs=[pl.BlockSpec((tq, 128), lambda h, i, j: (i, h)),
                   pl.BlockSpec((None, tq, 128), lambda h, i, j: (h, i, 0))],
        out_shape=[jax.ShapeDtypeStruct((n_rows, MLA_HEADS * MLA_V), F32),
                   jax.ShapeDtypeStruct((MLA_HEADS, n_rows, LANES), F32)],
        scratch_shapes=[pltpu.VMEM((tq, 1), F32), pltpu.VMEM((tq, 2 * LANES), F32)],
        compiler_params=_params(("parallel", "parallel", "arbitrary")))(q, kv, kv, kr)


def _attn_bwd(name, q, kv, kr, o, lse, do):
    n_rows = q.shape[0]
    tq, hq, tk, sub = _attn_tiles(n_rows, 1024)
    nq = n_rows // tq

    def body(q_ref, kn_ref, v_ref, kr_ref, o_ref, lse_ref, do_ref, dq_ref, dkv_ref, dkr_ref, dk_acc, dv_acc):
        j, i = pl.program_id(1), pl.program_id(2)

        @pl.when(i == 0)
        def _():
            dk_acc[...] = jnp.zeros(dk_acc.shape, F32)
            dv_acc[...] = jnp.zeros(dv_acc.shape, F32)

        @pl.when((i == 0) & (j == 0))
        def _():
            dq_ref[...] = jnp.zeros(dq_ref.shape, F32)

        for part in range(tq // hq):
            qr = slice(part * hq, (part + 1) * hq)
            qv = q_ref[qr, :]
            do = do_ref[qr, :]
            do_b = do.astype(BF16)
            delta = jnp.sum(do * o_ref[qr, :], axis=1, keepdims=True)
            lse_col = lse_ref[qr, :][:, :1]
            dq = None
            for c in range(tk // sub):
                rows = slice(c * sub, (c + 1) * sub)
                k = jnp.concatenate([kn_ref[rows, :], kr_ref[rows, :]], axis=1)
                s = lax.dot_general(qv, k, _NT, preferred_element_type=F32)
                p = jnp.exp(s - lse_col)
                dp = lax.dot_general(do_b, v_ref[rows, :], _NT, preferred_element_type=F32)
                ds = (p * (dp - delta)).astype(BF16)
                dv_acc[rows, :] += lax.dot_general(p.astype(BF16), do_b, _TN, preferred_element_type=F32)
                dk_acc[rows, :] += lax.dot_general(ds, qv, _TN, preferred_element_type=F32)
                t = jnp.dot(ds, k, preferred_element_type=F32)
                dq = t if dq is None else dq + t
            r0 = pl.multiple_of(i * tq + part * hq, hq)
            dq_ref[pl.ds(r0, hq), :] += dq

        @pl.when(i == nq - 1)
        def _():
            dkv_ref[...] = jnp.concatenate([dk_acc[:, :128], dv_acc[...]], axis=1).astype(dkv_ref.dtype)
            dkr_ref[...] = dk_acc[:, 128:]

    return pl.pallas_call(
        body, name=name, grid=(MLA_HEADS, n_rows // tk, nq),
        in_specs=[pl.BlockSpec((tq, 256), lambda h, j, i: (i, h)),
                  pl.BlockSpec((tk, 128), lambda h, j, i: (j, 2 * h)),
                  pl.BlockSpec((tk, 128), lambda h, j, i: (j, 2 * h + 1)),
                  pl.BlockSpec((tk, 128), lambda h, j, i: (j, 0)),
                  pl.BlockSpec((tq, 128), lambda h, j, i: (i, h)),
                  pl.BlockSpec((None, tq, 128), lambda h, j, i: (h, i, 0)),
                  pl.BlockSpec((tq, 128), lambda h, j, i: (i, h))],
        out_specs=[pl.BlockSpec((n_rows, 256), lambda h, j, i: (0, h)),
                   pl.BlockSpec((tk, 256), lambda h, j, i: (j, h)),
                   pl.BlockSpec((tk, 128), lambda h, j, i: (j, h))],
        out_shape=[jax.ShapeDtypeStruct((n_rows, MLA_HEADS * 256), F32), jax.ShapeDtypeStruct((n_rows, MLA_HEADS * 256), BF16),
                   jax.ShapeDtypeStruct((n_rows, MLA_HEADS * 128), F32)],
        scratch_shapes=[pltpu.VMEM((tk, 256), F32), pltpu.VMEM((tk, 128), F32)],
        compiler_params=_params(("parallel", "arbitrary", "arbitrary")))(q, kv, kv, kr, o, lse, do)


_MESH = pl.DeviceIdType.MESH


def _comm_call(name, body, xs, out_shape):
    n = len(xs)
    hbm = pl.BlockSpec(memory_space=pl.ANY)
    return pl.pallas_call(
        body, name=name, out_shape=out_shape, in_specs=[hbm] * n, out_specs=[hbm] * n,
        scratch_shapes=[pltpu.SemaphoreType.DMA((7 * n,)), pltpu.SemaphoreType.DMA((7 * n,)), pltpu.SemaphoreType.DMA((n,))],
    )(*xs)


def _all_gather(name, xs):
    n = len(xs)

    def body(*refs):
        x_refs, out_refs = refs[:n], refs[n:2 * n]
        send_sems, recv_sems, local_sems = refs[2 * n:]
        mx, my, mc = lax.axis_index("x"), lax.axis_index("y"), lax.axis_index("c")
        me, sibling = (mx, my, mc), (mx, my, 1 - mc)
        chips = [(1 - mx, my), (mx, 1 - my), (1 - mx, 1 - my)]

        def copy(a, k, block, to, src=None):
            dst = out_refs[a].at[4 * block[0] + 2 * block[1] + block[2]]
            return pltpu.make_async_remote_copy(
                src_ref=dst if src is None else src, dst_ref=dst, send_sem=send_sems.at[7 * a + k],
                recv_sem=recv_sems.at[7 * a + k], device_id=to, device_id_type=_MESH)

        mine = [pltpu.make_async_copy(x_refs[a], out_refs[a].at[4 * mx + 2 * my + mc], local_sems.at[a]) for a in range(n)]
        for cp in mine:
            cp.start()
        first = []
        for a in range(n):
            first.append(copy(a, 0, me, sibling, src=x_refs[a]))
            first += [copy(a, 1 + j, me, (*chip, mc), src=x_refs[a]) for j, chip in enumerate(chips)]
        for cp in first:
            cp.start()
        passed = []
        for j, chip in enumerate(chips):
            for a in range(n):
                copy(a, 1 + j, (*chip, mc), me).wait_recv()
                passed.append(copy(a, 4 + j, (*chip, mc), sibling))
                passed[-1].start()
        for a in range(n):
            copy(a, 0, sibling, me).wait_recv()
            for j, chip in enumerate(chips):
                copy(a, 4 + j, (*chip, 1 - mc), me).wait_recv()
        for cp in first + passed:
            cp.wait_send()
        for cp in mine:
            cp.wait()

    return _comm_call(name, body, xs, [jax.ShapeDtypeStruct((N_DEV,) + x.shape, x.dtype) for x in xs])


def _all_to_all(name, gs):
    n = len(gs)

    def body(*refs):
        g_refs, land_refs = refs[:n], refs[n:2 * n]
        send_sems, recv_sems, local_sems = refs[2 * n:]
        mx, my, mc = lax.axis_index("x"), lax.axis_index("y"), lax.axis_index("c")
        me = 4 * mx + 2 * my + mc
        mine = [pltpu.make_async_copy(g_refs[a].at[me], land_refs[a].at[me], local_sems.at[a]) for a in range(n)]
        for cp in mine:
            cp.start()
        copies = []
        for k in range(1, N_DEV):
            px = 1 - mx if k & 4 else mx
            py = 1 - my if k & 2 else my
            pc = 1 - mc if k & 1 else mc
            peer = 4 * px + 2 * py + pc
            for a in range(n):
                sems = dict(send_sem=send_sems.at[7 * a + k - 1], recv_sem=recv_sems.at[7 * a + k - 1],
                            device_id=(px, py, pc), device_id_type=_MESH)
                send = pltpu.make_async_remote_copy(src_ref=g_refs[a].at[peer], dst_ref=land_refs[a].at[me], **sems)
                recv = pltpu.make_async_remote_copy(src_ref=g_refs[a].at[peer], dst_ref=land_refs[a].at[peer], **sems)
                send.start()
                copies.append((send, recv))
        for _, recv in copies:
            recv.wait_recv()
        for send, _ in copies:
            send.wait_send()
        for cp in mine:
            cp.wait()

    return _comm_call(name, body, gs, [jax.ShapeDtypeStruct(g.shape, g.dtype) for g in gs])


def _adamw(name, parts, w, m, v, tr):
    rows, cols = w.shape

    def body(p_ref, w_ref, m_ref, v_ref, g_ref, d_ref, nm_ref, nv_ref):
        g = p_ref[0].astype(F32)
        for d in range(1, N_DEV):
            g = g + p_ref[d].astype(F32)
        nm = ADAM_B1 * m_ref[...] + (1.0 - ADAM_B1) * g
        nv = ADAM_B2 * v_ref[...] + (1.0 - ADAM_B2) * jnp.square(g)
        m_hat = nm / (1.0 - ADAM_B1 ** ADAM_STEP)
        v_hat = nv / (1.0 - ADAM_B2 ** ADAM_STEP)
        g_ref[...] = g
        d_ref[...] = -ADAM_LR * (m_hat / (jnp.sqrt(v_hat) + ADAM_EPS) + ADAM_WD * w_ref[...])
        nm_ref[...] = nm
        nv_ref[...] = nv

    spec = pl.BlockSpec((tr, cols), lambda i: (i, 0))
    return pl.pallas_call(
        body, name=name, grid=(rows // tr,),
        in_specs=[pl.BlockSpec((N_DEV, tr, cols), lambda i: (0, i, 0)), spec, spec, spec],
        out_specs=[spec] * 4, out_shape=[jax.ShapeDtypeStruct((rows, cols), F32)] * 4,
        compiler_params=_params(("parallel",)))(parts, w, m, v)


def _in_pieces():
    p = []
    for h in range(RET_HEADS):
        p += [(128 * h, 128 * h + 128), (512 + 128 * h, 512 + 128 * h + 128), (1024 + 256 * h, 1024 + 256 * h + 256)]
    p += [(2048, 3072), (3776, 4800), (4800, 7872), (3072, 3456), 128, (3456, 3712),
          (3712, 3744), 32, (3744, 3776), 32, 128]
    return p


def _uq_pieces():
    p = []
    for h in range(MLA_HEADS):
        b = 192 * h
        p += [(b, b + 128), (b + 128, b + 160), 32, (b + 160, b + 192), 32]
    return p


def _perm(w, pieces):
    cols = [jnp.zeros(w.shape[:-1] + (p,), w.dtype) if isinstance(p, int) else w[..., p[0]:p[1]] for p in pieces]
    return jnp.concatenate(cols, axis=-1)


def _unperm(dw, pieces):
    found, off = [], 0
    for p in pieces:
        if isinstance(p, int):
            off += p
        else:
            found.append((p[0], dw[..., off:off + p[1] - p[0]]))
            off += p[1] - p[0]
    return jnp.concatenate([t for _, t in sorted(found, key=lambda s: s[0])], axis=-1)


def _unshard(blocks, axis):
    return jnp.concatenate([blocks[p] for p in range(N_DEV)], axis=axis)


def _shard_split(full, axis):
    return jnp.stack(jnp.split(full, N_DEV, axis=axis), axis=0)


def _row_tile(rows, cap=256, unit=16):
    return max(t for t in range(unit, cap + 1, unit) if rows % t == 0)


def _rope_tables(seq):
    pos = jnp.arange(seq, dtype=F32)[:, None]

    def table(dim):
        inv = 1.0 / (ROPE_THETA ** (jnp.arange(0, dim, 2, dtype=F32) / dim))
        ang = pos * inv[None, :]
        return jnp.cos(ang), jnp.sin(ang)

    cr, sr = table(RET_QK)
    cm, sm = table(MLA_ROPE)
    z = jnp.zeros_like(cm)
    return (jnp.concatenate([cr, cr], 1), jnp.concatenate([-sr, sr], 1),
            jnp.concatenate([cm, z, cm, z], 1), jnp.concatenate([-sm, z, sm, z], 1))


def _s5_maps(a_re, a_im, log_dt, b_re, b_im, c_re, c_im):
    dt = jnp.exp(log_dt)[:, None]
    ar = jnp.minimum(a_re, -1e-4)
    mag = jnp.exp(dt * ar)
    abar_re = mag * jnp.cos(dt * a_im)
    abar_im = mag * jnp.sin(dt * a_im)
    den = ar * ar + a_im * a_im
    nr = abar_re - 1.0
    ni = abar_im
    coef_re = (nr * ar + ni * a_im) / den
    coef_im = (ni * ar - nr * a_im) / den
    bb_re = coef_re[..., None] * b_re - coef_im[..., None] * b_im
    bb_im = coef_re[..., None] * b_im + coef_im[..., None] * b_re
    eye = jnp.eye(S5_BLOCKS, dtype=F32)

    def in_blocks(bb):
        t = bb.transpose(0, 2, 1).reshape(S5_BLOCKS, 8, S5_GROUP, S5_STATE)
        return jnp.einsum('jgcp,gh->jgchp', t, eye).reshape(S5_BLOCKS, 128, 512)

    def out_blocks(cc):
        t = cc.transpose(0, 2, 1).reshape(S5_BLOCKS, 8, S5_STATE, S5_GROUP)
        return jnp.einsum('jgpc,gh->jgphc', t, eye).reshape(S5_BLOCKS, 512, 128)

    arow = jnp.concatenate([abar_re.reshape(S5_BLOCKS, 512), abar_im.reshape(S5_BLOCKS, 512)], axis=1).reshape(1, -1)
    b_map = jnp.concatenate([in_blocks(bb_re), in_blocks(bb_im)], axis=2)
    c_map = jnp.concatenate([out_blocks(c_re), -out_blocks(c_im)], axis=1)
    return arow, b_map, c_map


def _power_tables(arow, conj, reverse):
    a = arow.reshape(S5_BLOCKS, 2, 512)
    ar, ai = a[:, 0], (-a[:, 1] if conj else a[:, 1])
    pw = [(ar, ai)]
    for _ in range(SUBLANES - 1):
        pr, pi = pw[-1]
        pw.append((pr * ar - pi * ai, pr * ai + pi * ar))

    def rows(sel):
        return jnp.stack([jnp.stack(list(pw[i]), axis=1) for i in sel], axis=0).reshape(len(sel), -1)

    rowid = jnp.arange(SUBLANES)[:, None]
    ak = jnp.concatenate([jnp.where((rowid < SUBLANES - k) if reverse else (rowid >= k), rows([k - 1]), 0.0)
                          for k in (1, 2, 4)], axis=0)
    order = list(range(SUBLANES))
    apow = rows(order[::-1] if reverse else order)
    return ak, apow


def _rows(arr):
    return (arr, arr.shape[1], 0)


def _vjp_rows(f, n_prim):
    def fn(r, c, _):
        _, vjp = jax.vjp(f, *r[:n_prim])
        return list(vjp(r[n_prim])), []
    return fn


def _norm_bwd(r, c, _):
    _, vjp = jax.vjp(_f_rms, r[0], c[0])
    dx, dg = vjp(r[1])
    return [dx + r[2]], [dg]


def _layer_fwd(l, x, W, P, T):
    n = x.shape[0]
    tm, tmw = _pick(n, 256), _pick(n, 128)
    cos_r, sin_r, cos_m, sin_m = T

    def nm(s):
        return f"l{l}_{s}"

    def one(name, f, rows, consts, width, dtype=BF16, tile=tm):
        return _tile_call(nm(name), lambda r, c, _: ([f(r, c)], []), n, tile, rows, consts, [(width, dtype)])[0]

    h = one("norm1", lambda r, c: _f_rms(r[0], c[0]), [_rows(x)], [P["norm1_g"]], D_MODEL)
    z = _mm(nm("in_proj"), h, W["in"])
    of, stf = _ret_fwd(nm("ret_f"), z, cos_r, sin_r, P["rd"][0], reverse=False)
    ob, stb = _ret_fwd(nm("ret_b"), z, cos_r, sin_r, P["rd"][1], reverse=True)

    def gn(r, c, _):
        yraw = r[0] + r[1]
        ys = [_f_gn_gate(yraw[:, RET_V * i:RET_V * (i + 1)], r[2][:, RET_V * i:RET_V * (i + 1)],
                         c[0][:, RET_V * i:RET_V * (i + 1)]) for i in range(RET_HEADS)]
        return [yraw, jnp.concatenate(ys, axis=1)], []

    yraw, yret = _tile_call(nm("ret_gn"), gn, n, tm, [_rows(of), _rows(ob), (z, 1024, Z_RG // 1024)], [P["ret_gn_g"]],
                            [(1024, F32), (1024, BF16)])

    cqn = one("q_norm", lambda r, c: _f_rms(r[0][:, :MLA_Q_LORA], c[0]), [(z, 512, Z_CQ // 512)], [P["mla_q_norm_g"]], MLA_Q_LORA)
    ckvn = one("kv_norm", lambda r, c: _f_rms(r[0], c[0]), [(z, 256, Z_CKV // 256)], [P["mla_kv_norm_g"]], MLA_KV_LORA)
    qraw = _mm(nm("q_up"), cqn, W["uq"])
    kv = _mm(nm("kv_up"), ckvn, W["ukv"], out_dtype=BF16)
    q = one("q_rope", lambda r, c: _mla_q(r[0], r[1], r[2], False), [_rows(qraw), _rows(cos_m), _rows(sin_m)], [], 2048)
    kr = one("k_rope", lambda r, c: _rope(r[0], r[1], r[2]), [(z, 128, Z_KR // 128), _rows(cos_m), _rows(sin_m)], [], 128)
    o, lse = _attn_fwd(nm("attn"), q, kv, kr)

    xs, y_dir = [], []
    for d in range(2):
        x_d, y_d = _s5_fwd(nm(f"s5_scan{d}"), z, P["s5"][d], reverse=(d == 1))
        xs.append(x_d)
        y_dir.append(y_d)

    def s5_act(r, c, _):
        ysum = r[0] + r[1]
        return [ysum, _f_s5_act(ysum, r[2], c[0])], []

    ysum, gact = _tile_call(nm("s5_act"), s5_act, n, tm, [_rows(y_dir[0]), _rows(y_dir[1]), (z, 1024, Z_U // 1024)], [P["s5_d"]],
                            [(1024, F32), (1024, BF16)])
    ga = _mm(nm("glu_a"), gact, W["glu_a"])
    gb = _mm(nm("glu_b"), gact, W["glu_b"])
    ys5 = one("glu", lambda r, c: _f_glu(r[0], r[1]), [_rows(ga), _rows(gb)], [], 1024)

    ys = (yret, o, ys5)
    ps = [_mm(nm(f"branch{i}"), ys[i], W["branch"][i]) for i in range(3)]
    mix = one("mix", lambda r, c: _f_gate(r[0], r[3]) + _f_gate(r[1], r[4]) + _f_gate(r[2], r[5]),
              [(z, 1024, Z_GATE // 1024 + i) for i in range(3)] + [_rows(p) for p in ps], [], 1024)
    x1 = _mm(nm("out_proj"), mix, W["out"], add=x)
    h2 = one("norm2", lambda r, c: _f_rms(r[0], c[0]), [_rows(x1)], [P["norm2_g"]], D_MODEL)
    gp = _mm(nm("ffn_g"), h2, W["ffn_g"])
    up = _mm(nm("ffn_u"), h2, W["ffn_u"])
    act = one("swiglu", lambda r, c: _f_swiglu(r[0], r[1]), [_rows(gp), _rows(up)], [], FFN_HIDDEN, tile=tmw)
    x2 = _mm(nm("ffn_down"), act, W["ffn_down"], add=x1)
    saved = dict(x=x, h=h, z=z, stf=stf, stb=stb, yraw=yraw, ys=ys, cqn=cqn, ckvn=ckvn, qraw=qraw, q=q, kv=kv, kr=kr,
                 lse=lse, xs=xs, ysum=ysum, gact=gact, ga=ga, gb=gb, ps=ps, mix=mix, x1=x1, h2=h2, gp=gp, up=up, act=act)
    return x2, saved


def _layer_bwd(l, dx2, sv, W, P, T):
    n = dx2.shape[0]
    tm, tmw = _pick(n, 256), _pick(n, 128)
    cos_r, sin_r, cos_m, sin_m = T
    z = sv["z"]
    g = {}

    def nm(s):
        return f"l{l}_{s}"

    dact = _mm(nm("d_act"), dx2, W["ffn_down"], tb=True)
    g["ffn_down"] = _mm(nm("dw_ffn_down"), sv["act"], dx2, ta=True)
    dgp, dup = _tile_call(nm("d_swiglu"), _vjp_rows(_f_swiglu, 2), n, tmw, [_rows(sv["gp"]), _rows(sv["up"]), _rows(dact)], [],
                          [(FFN_HIDDEN, BF16)] * 2)
    dh2 = _mm(nm("d_h2_g"), dgp, W["ffn_g"], tb=True)
    dh2 = _mm(nm("d_h2_u"), dup, W["ffn_u"], tb=True, add=dh2)
    g["ffn_g"] = _mm(nm("dw_ffn_g"), sv["h2"], dgp, ta=True)
    g["ffn_u"] = _mm(nm("dw_ffn_u"), sv["h2"], dup, ta=True)
    dx1, g["norm2_g"] = _tile_call(nm("d_norm2"), _norm_bwd, n, tm, [_rows(sv["x1"]), _rows(dh2), _rows(dx2)], [P["norm2_g"]],
                                   [(D_MODEL, F32)], acc_outs=[(1, D_MODEL)])

    dmix = _mm(nm("d_mix"), dx1, W["out"], tb=True)
    g["out"] = _mm(nm("dw_out"), sv["mix"], dx1, ta=True)
    dz = jnp.zeros((n, ZW), BF16)
    dys, g["branch"] = [], []
    for i in range(3):
        dz, dp = _tile_call(nm(f"d_gate{i}"), _vjp_rows(_f_gate, 2), n, tm,
                            [(z, 1024, Z_GATE // 1024 + i), _rows(sv["ps"][i]), _rows(dmix)], [], [(1024, BF16)],
                            alias=(dz, 1024, Z_GATE // 1024 + i))
        dys.append(_mm(nm(f"d_branch{i}"), dp, W["branch"][i], tb=True))
        g["branch"].append(_mm(nm(f"dw_branch{i}"), sv["ys"][i], dp, ta=True))

    dga, dgb = _tile_call(nm("d_glu"), _vjp_rows(_f_glu, 2), n, tm, [_rows(sv["ga"]), _rows(sv["gb"]), _rows(dys[2])], [],
                          [(1024, BF16)] * 2)
    dgact = _mm(nm("d_gact_a"), dga, W["glu_a"], tb=True)
    dgact = _mm(nm("d_gact_b"), dgb, W["glu_b"], tb=True, add=dgact)
    g["glu_a"] = _mm(nm("dw_glu_a"), sv["gact"], dga, ta=True)
    g["glu_b"] = _mm(nm("dw_glu_b"), sv["gact"], dgb, ta=True)

    def act_bwd(r, c, _):
        _, vjp = jax.vjp(_f_s5_act, r[0], r[1], c[0])
        dy, du, dd = vjp(r[2])
        return [dy, du], [dd]

    dysum, du_part, g["s5_d"] = _tile_call(nm("d_s5_act"), act_bwd, n, tm,
                                           [_rows(sv["ysum"]), (z, 1024, Z_U // 1024), _rows(dgact)], [P["s5_d"]],
                                           [(1024, F32)] * 2, acc_outs=[(1, 1024)])
    dus, g["s5"] = [], []
    for d in range(2):
        du, g_b, g_c, da = _s5_bwd(nm(f"d_s5_scan{d}"), z, dysum, sv["xs"][d], P["s5"][d], reverse=(d == 0))
        dus.append(du)
        g["s5"].append((da[:1], g_b, g_c))
    dz, = _tile_call(nm("d_s5_u"), lambda r, c, _: ([r[0] + r[1] + r[2]], []), n, tm,
                     [_rows(du_part), _rows(dus[0]), _rows(dus[1])], [], [], alias=(dz, 1024, Z_U // 1024))

    o = sv["ys"][1]
    dq, dkv, dkr = _attn_bwd(nm("d_attn"), sv["q"], sv["kv"], sv["kr"], o, sv["lse"], dys[1])
    dqraw, = _tile_call(nm("d_q_rope"), lambda r, c, _: ([_mla_q(r[0], r[1], r[2], True)], []), n, tm,
                        [_rows(dq), _rows(cos_m), _rows(sin_m)], [], [(2048, BF16)])

    def kr_bwd(r, c, _):
        tot = r[0][:, :128]
        for h in range(1, MLA_HEADS):
            tot = tot + r[0][:, 128 * h:128 * (h + 1)]
        return [_rope_t(tot, r[1], r[2])], []

    dz, = _tile_call(nm("d_k_rope"), kr_bwd, n, tm, [_rows(dkr), _rows(cos_m), _rows(sin_m)], [], [],
                     alias=(dz, 128, Z_KR // 128))
    dcqn = _mm(nm("d_cqn"), dqraw, W["uq"], tb=True)
    g["uq"] = _mm(nm("dw_uq"), sv["cqn"], dqraw, ta=True)
    dckvn = _mm(nm("d_ckvn"), dkv, W["ukv"], tb=True)
    g["ukv"] = _mm(nm("dw_ukv"), sv["ckvn"], dkv, ta=True)

    def qn_bwd(r, c, _):
        _, vjp = jax.vjp(_f_rms, r[0][:, :MLA_Q_LORA], c[0])
        da, dg = vjp(r[1])
        return [jnp.concatenate([da, jnp.zeros((da.shape[0], 512 - MLA_Q_LORA), F32)], axis=1)], [dg]

    dz, g["mla_q_norm_g"] = _tile_call(nm("d_q_norm"), qn_bwd, n, tm, [(z, 512, Z_CQ // 512), _rows(dcqn)], [P["mla_q_norm_g"]],
                                       [], acc_outs=[(1, MLA_Q_LORA)], alias=(dz, 512, Z_CQ // 512))

    def kvn_bwd(r, c, _):
        _, vjp = jax.vjp(_f_rms, r[0], c[0])
        da, dg = vjp(r[1])
        return [da], [dg]

    dz, g["mla_kv_norm_g"] = _tile_call(nm("d_kv_norm"), kvn_bwd, n, tm, [(z, 256, Z_CKV // 256), _rows(dckvn)],
                                        [P["mla_kv_norm_g"]], [], acc_outs=[(1, MLA_KV_LORA)], alias=(dz, 256, Z_CKV // 256))

    def gn_bwd(r, c, _):
        drg, dy, dg = [], [], []
        for i in range(RET_HEADS):
            sl = slice(RET_V * i, RET_V * (i + 1))
            _, vjp = jax.vjp(_f_gn_gate, r[0][:, sl], r[1][:, sl], c[0][:, sl])
            a, b, e = vjp(r[2][:, sl])
            dy.append(a)
            drg.append(b)
            dg.append(e)
        return [jnp.concatenate(drg, axis=1), jnp.concatenate(dy, axis=1)], [jnp.concatenate(dg, axis=1)]

    dz, dyraw, g["ret_gn_g"] = _tile_call(nm("d_ret_gn"), gn_bwd, n, tm,
                                          [_rows(sv["yraw"]), (z, 1024, Z_RG // 1024), _rows(dys[0])], [P["ret_gn_g"]],
                                          [(1024, F32)], acc_outs=[(1, 1024)], alias=(dz, 1024, Z_RG // 1024))
    dz, drd_f = _ret_bwd(nm("d_ret_f"), z, cos_r, sin_r, P["rd"][0], sv["stf"], dyraw, dz, reverse=False)
    dz, drd_b = _ret_bwd(nm("d_ret_b"), z, cos_r, sin_r, P["rd"][1], sv["stb"], dyraw, dz, reverse=True)
    g["ret_decay"] = jnp.stack([drd_f[:, 0, 0], drd_b[:, 0, 0]], axis=0)

    dh = _mm(nm("d_h"), dz, W["in"], tb=True)
    g["in"] = _mm(nm("dw_in"), sv["h"], dz, ta=True)
    dx, g["norm1_g"] = _tile_call(nm("d_norm1"), _norm_bwd, n, tm, [_rows(sv["x"]), _rows(dh), _rows(dx1)], [P["norm1_g"]],
                                  [(D_MODEL, F32)], acc_outs=[(1, D_MODEL)])
    return dx, g


INPUT_NAMES = ("x",) + WEIGHTS + ("loss_target",) + tuple("m_" + n for n in WEIGHTS) + tuple("v_" + n for n in WEIGHTS)
S5_NAMES = ("s5_a_re", "s5_a_im", "s5_log_dt", "s5_b_re", "s5_b_im", "s5_c_re", "s5_c_im")


def _pack_rows(arrays, tile_rows):
    pieces = []
    for a in arrays:
        flat = a.reshape(-1)
        pad = -flat.shape[0] % LANES
        if pad:
            flat = jnp.concatenate([flat, jnp.zeros((pad,), flat.dtype)])
        pieces.append(flat.reshape(-1, LANES))
    pad = -sum(p.shape[0] for p in pieces) % tile_rows
    if pad:
        pieces.append(jnp.zeros((pad, LANES), pieces[0].dtype))
    return jnp.concatenate(pieces, axis=0)


def _unpack_rows(packed, shapes):
    out, r0 = [], 0
    for s in shapes:
        size = math.prod(s)
        rows = -(-size // LANES)
        out.append(packed[r0:r0 + rows].reshape(-1)[:size].reshape(s))
        r0 += rows
    return out


def _local_step(inp, full, x, target):
    n = x.shape[0]
    tables = _rope_tables(n)
    in_pieces, uq_pieces = _in_pieces(), _uq_pieces()
    Ws, Ps, s5_vjps = [], [], []
    for l in range(DEPTH):
        Ws.append(dict(
            **{"in": _perm(full["w_in"][l], in_pieces)}, uq=_perm(full["mla_w_uq"][l], uq_pieces), ukv=full["mla_w_ukv"][l],
            glu_a=full["s5_w_glu"][l][:, :1024], glu_b=full["s5_w_glu"][l][:, 1024:],
            branch=[full["w_branch"][l, i] for i in range(3)], out=full["w_out"][l],
            ffn_g=full["ffn_w_gu"][l][:, :FFN_HIDDEN], ffn_u=full["ffn_w_gu"][l][:, FFN_HIDDEN:],
            ffn_down=full["ffn_w_down"][l]))
        s5, vjps = [], []
        for d in range(2):
            (arow, b_map, c_map), vjp = jax.vjp(_s5_maps, *[inp[k][l, d] for k in S5_NAMES])
            arow = lax.stop_gradient(arow)
            ak, apow = _power_tables(arow, False, d == 1)
            ak_adj, apow_adj = _power_tables(arow, True, d == 0)
            s5.append(dict(b_map=b_map, c_map=c_map, b_map_t=b_map.transpose(0, 2, 1), c_map_t=c_map.transpose(0, 2, 1),
                           ak=ak, apow=apow, ak_adj=ak_adj, apow_adj=apow_adj))
            vjps.append(vjp)
        s5_vjps.append(vjps)
        Ps.append(dict(
            norm1_g=inp["norm1_g"][l][None], norm2_g=inp["norm2_g"][l][None], ret_gn_g=inp["ret_gn_g"][l][None],
            mla_q_norm_g=inp["mla_q_norm_g"][l][None], mla_kv_norm_g=inp["mla_kv_norm_g"][l][None], s5_d=inp["s5_d"][l][None],
            rd=[jnp.broadcast_to(inp["ret_decay"][l, d][:, None, None], (RET_HEADS, 1, LANES)) for d in range(2)], s5=s5))

    h, saved = x, []
    for l in range(DEPTH):
        h, sv = _layer_fwd(l, h, Ws[l], Ps[l], tables)
        saved.append(sv)

    def loss_bwd(r, c, _):
        loss, vjp = jax.vjp(lambda a, gain: _f_loss(a, gain, r[1]), r[0], c[0])
        da, dg = vjp(jnp.ones((), F32))
        return [da], [dg, jnp.broadcast_to(loss, (1, LANES))]

    dh, g_final, loss_row = _tile_call("loss", loss_bwd, n, _pick(n, 256), [_rows(h), _rows(target)], [inp["final_g"][None]],
                                       [(D_MODEL, F32)], acc_outs=[(1, D_MODEL), (1, LANES)])
    layer_g = [None] * DEPTH
    for l in reversed(range(DEPTH)):
        dh, layer_g[l] = _layer_bwd(l, dh, saved[l], Ws[l], Ps[l], tables)

    def stack(f):
        return jnp.stack([f(layer_g[l], l) for l in range(DEPTH)], axis=0)

    grads = dict(
        w_in=stack(lambda g, l: _unperm(g["in"], in_pieces)), mla_w_uq=stack(lambda g, l: _unperm(g["uq"], uq_pieces)),
        mla_w_ukv=stack(lambda g, l: g["ukv"]), s5_w_glu=stack(lambda g, l: jnp.concatenate([g["glu_a"], g["glu_b"]], axis=1)),
        w_branch=stack(lambda g, l: jnp.stack(g["branch"], axis=0)), w_out=stack(lambda g, l: g["out"]),
        ffn_w_gu=stack(lambda g, l: jnp.concatenate([g["ffn_g"], g["ffn_u"]], axis=1)),
        ffn_w_down=stack(lambda g, l: g["ffn_down"]),
        norm1_g=stack(lambda g, l: g["norm1_g"][0]), norm2_g=stack(lambda g, l: g["norm2_g"][0]),
        ret_gn_g=stack(lambda g, l: g["ret_gn_g"][0]), mla_q_norm_g=stack(lambda g, l: g["mla_q_norm_g"][0]),
        mla_kv_norm_g=stack(lambda g, l: g["mla_kv_norm_g"][0]), s5_d=stack(lambda g, l: g["s5_d"][0]),
        ret_decay=stack(lambda g, l: g["ret_decay"]), final_g=g_final[0])
    s5_grads = [[s5_vjps[l][d](layer_g[l]["s5"][d]) for d in range(2)] for l in range(DEPTH)]
    for i, k in enumerate(S5_NAMES):
        grads[k] = jnp.stack([jnp.stack([s5_grads[l][d][i] for d in range(2)], axis=0) for l in range(DEPTH)], axis=0)
    return loss_row[0, 0], dh, grads


def kernel(*args):
    inp = dict(zip(INPUT_NAMES, args))

    kinds = ("grad_", "delta_", "new_m_", "new_v_")

    gathered = _all_gather("gather_weights", [inp[k].astype(BF16) for k in SHARDED])
    full = {k: _unshard(g, SHARD_AXIS[k]) for k, g in zip(SHARDED, gathered)}

    loss, dh, grads = _local_step(inp, full, inp["x"][0], inp["loss_target"][0])
    loss = lax.psum(loss, ("x", "y", "c"))

    landed = _all_to_all("exchange_grads", [_shard_split(grads[k], SHARD_AXIS[k]).astype(BF16) for k in SHARDED])
    out = {}
    for k, land in zip(SHARDED, landed):
        shape = inp[k].shape
        rows, cols = math.prod(shape[:-1]), shape[-1]
        res = _adamw("adamw_" + k, land.reshape(N_DEV, rows, cols), *[inp[p + k].reshape(rows, cols) for p in ("", "m_", "v_")],
                     _row_tile(rows))
        for kind, t in zip(kinds, res):
            out[kind + k] = t.reshape(shape)

    small_tr = 512
    partial = _pack_rows([grads[k] for k in SMALL], small_tr)
    packed = [_pack_rows([inp[p + k] for k in SMALL], small_tr) for p in ("", "m_", "v_")]
    res_small = _adamw("adamw_small", _all_gather("gather_small_grads", [partial])[0], *packed, small_tr)
    for kind, b in zip(kinds, res_small):
        for k, t in zip(SMALL, _unpack_rows(b, [inp[k].shape for k in SMALL])):
            out[kind + k] = t
    return (loss, dh[None]) + tuple(out[kind + k] for kind in kinds for k in WEIGHTS)
```

```python
import functools
import math

import jax
import jax.numpy as jnp
from jax import lax
from jax.experimental import pallas as pl
from jax.experimental.pallas import tpu as pltpu

F32 = jnp.float32
BF16 = jnp.bfloat16

D_MODEL = 1024
DEPTH = 2
RMS_EPS = 1e-6
GN_EPS = 1e-5
ROPE_THETA = 10000.0
RET_HEADS, RET_QK, RET_V, RET_CHUNK = 4, 128, 256, 128
MLA_HEADS, MLA_Q_LORA, MLA_KV_LORA, MLA_NOPE, MLA_ROPE, MLA_V = 8, 384, 256, 128, 64, 128
S5_GROUPS, S5_GROUP, S5_STATE = 64, 16, 64
S5_BLOCKS = 8
FFN_HIDDEN = 2816
N_DEV = 8
ADAM_LR, ADAM_B1, ADAM_B2, ADAM_EPS, ADAM_WD, ADAM_STEP = 0.001, 0.9, 0.999, 1e-08, 0.01, 10

LANES = 128
SUBLANES = 8
VMEM_LIMIT = 48 * 1024 * 1024

ZW = 8192
Z_RET = 0
Z_RG = 2048
Z_U = 3072
Z_GATE = 4096
Z_CQ = 7168
Z_CKV = 7680
Z_KR = 7936
IN_SPLITS = (512, 512, 1024, 1024, 384, 256, 64, 1024, 3072)

SHARDED = ("w_in", "mla_w_uq", "mla_w_ukv", "s5_w_glu", "w_branch", "w_out", "ffn_w_gu", "ffn_w_down")
SHARD_AXIS = {"w_in": 2, "mla_w_uq": 2, "mla_w_ukv": 2, "s5_w_glu": 2, "w_branch": 2, "w_out": 1,
              "ffn_w_gu": 2, "ffn_w_down": 1}
SMALL = ("norm1_g", "ret_decay", "ret_gn_g", "mla_q_norm_g", "mla_kv_norm_g", "s5_a_re", "s5_a_im", "s5_log_dt",
         "s5_b_re", "s5_b_im", "s5_c_re", "s5_c_im", "s5_d", "norm2_g", "final_g")
WEIGHTS = ("norm1_g", "w_in", "ret_decay", "ret_gn_g", "mla_q_norm_g", "mla_w_uq", "mla_kv_norm_g", "mla_w_ukv",
           "s5_a_re", "s5_a_im", "s5_log_dt", "s5_b_re", "s5_b_im", "s5_c_re", "s5_c_im", "s5_d", "s5_w_glu",
           "w_branch", "w_out", "norm2_g", "ffn_w_gu", "ffn_w_down", "final_g")


def _params(sem=None):
    return pltpu.CompilerParams(dimension_semantics=sem, vmem_limit_bytes=VMEM_LIMIT)


def _pick(n, cap):
    if n <= cap:
        return n
    t = cap - cap % LANES
    while t >= LANES:
        if n % t == 0:
            return t
        t -= LANES
    return n


@functools.partial(jax.custom_vjp, nondiff_argnums=(2, 3))
def _bdot(a, b, ca, cb):
    return lax.dot_general(a.astype(BF16), b.astype(BF16), (((ca,), (cb,)), ((), ())), preferred_element_type=F32)


def _bdot_fwd(a, b, ca, cb):
    return _bdot(a, b, ca, cb), (a, b)


def _bdot_bwd(ca, cb, res, g):
    a, b = res
    da = _bdot(g, b, 1, 1 - cb) if ca == 1 else _bdot(b, g, 1 - cb, 1)
    db = _bdot(a, g, 1 - ca, 0) if cb == 0 else _bdot(g, a, 0, 1 - ca)
    return da, db


_bdot.defvjp(_bdot_fwd, _bdot_bwd)


@jax.custom_vjp
def _swap_halves(x):
    return pltpu.roll(x, LANES // 2, 1)


_swap_halves.defvjp(lambda x: (_swap_halves(x), None), lambda _, g: (_swap_halves(g),))


def _rope(x, cosf, sinf):
    return x * cosf + _swap_halves(x) * sinf


def _f_rms(x, g):
    return x * lax.rsqrt(jnp.mean(x * x, axis=-1, keepdims=True) + RMS_EPS) * g


def _rope_t(g, cosf, sinf):
    return g * cosf + _swap_halves(g * sinf)


def _f_gn_gate(yh, rgh, gh):
    mu = jnp.mean(yh, axis=-1, keepdims=True)
    var = jnp.mean(jnp.square(yh - mu), axis=-1, keepdims=True)
    return jax.nn.silu(rgh) * ((yh - mu) * lax.rsqrt(var + GN_EPS) * gh)


MLA_SCALE = (MLA_NOPE + MLA_ROPE) ** -0.5


def _mla_q(qraw, cosf, sinf, transpose):
    parts = []
    for h in range(MLA_HEADS):
        parts.append(qraw[:, 256 * h:256 * h + 128] * MLA_SCALE)
        r = qraw[:, 256 * h + 128:256 * h + 256]
        parts.append(_rope_t(r * MLA_SCALE, cosf, sinf) if transpose else _rope(r, cosf, sinf) * MLA_SCALE)
    return jnp.concatenate(parts, axis=1)


def _f_s5_act(ysum, u, d):
    return jax.nn.gelu(ysum + d * u)


def _f_glu(ga, gb):
    return ga * jax.nn.sigmoid(gb)


def _f_gate(zg, proj):
    return jax.nn.sigmoid(zg) * proj


def _f_swiglu(gp, up):
    return jax.nn.silu(gp) * up


def _f_loss(x, g, target):
    y = _f_rms(x, g)
    err = jnp.square(y - target)
    return 0.5 * jnp.sum(jnp.mean(err, axis=-1))


def _tile_call(name, fn, n_rows, tm, row_ins, consts, row_outs, acc_outs=(), alias=None):
    n_row_in, n_const = len(row_ins), len(consts)
    args = [a for a, _, _ in row_ins] + list(consts)
    in_specs = [pl.BlockSpec((tm, w), lambda i, cb=cb: (i, cb)) for _, w, cb in row_ins]
    in_specs += [pl.BlockSpec(c.shape, lambda i: (0, 0)) for c in consts]
    out_shape, out_specs, aliases = [], [], {}
    if alias is not None:
        arr, w, cb = alias
        in_specs.append(pl.BlockSpec((tm, w), lambda i, cb=cb: (i, cb)))
        aliases[len(args)] = 0
        args.append(arr)
        out_shape.append(jax.ShapeDtypeStruct(arr.shape, arr.dtype))
        out_specs.append(pl.BlockSpec((tm, w), lambda i, cb=cb: (i, cb)))
    for w, dt in row_outs:
        out_shape.append(jax.ShapeDtypeStruct((n_rows, w), dt))
        out_specs.append(pl.BlockSpec((tm, w), lambda i: (i, 0)))
    n_row_out = len(out_shape)
    for r, w in acc_outs:
        out_shape.append(jax.ShapeDtypeStruct((r, w), F32))
        out_specs.append(pl.BlockSpec((r, w), lambda i: (0, 0)))
    n_in = len(args)

    def body(*refs):
        ins, outs = refs[:n_in], refs[n_in:]
        rows = [r[...] for r in ins[:n_row_in]]
        cons = [r[...] for r in ins[n_row_in:n_row_in + n_const]]
        prev = ins[-1][...] if alias is not None else None
        res_rows, res_accs = fn(rows, cons, prev)
        for o, r in zip(outs[:n_row_out], res_rows):
            o[...] = r.astype(o.dtype)
        if acc_outs:
            @pl.when(pl.program_id(0) == 0)
            def _():
                for o in outs[n_row_out:]:
                    o[...] = jnp.zeros(o.shape, F32)
            for o, r in zip(outs[n_row_out:], res_accs):
                o[...] += r

    return pl.pallas_call(
        body, name=name, grid=(n_rows // tm,), in_specs=in_specs, out_specs=out_specs, out_shape=out_shape,
        input_output_aliases=aliases, compiler_params=_params(("arbitrary",)))(*args)


def _mm(name, a, b, *, ta=False, tb=False, add=None, out_dtype=F32):
    (K, M) = a.shape if ta else a.shape[::-1]
    (N, K2) = b.shape if tb else b.shape[::-1]
    assert K == K2, (name, a.shape, b.shape)
    tm, tn, tk = _pick(M, 1536), _pick(N, 1536), _pick(K, 1024)
    gi, gj, nk = M // tm, N // tn, K // tk
    a_bytes, b_bytes = a.size * a.dtype.itemsize, b.size * b.dtype.itemsize
    j_outer = nk == 1 and b_bytes + a_bytes * gj < a_bytes + b_bytes * gi

    def im(f):
        return (lambda g0, g1, k: f(g1, g0, k)) if j_outer else f

    a_spec = pl.BlockSpec((tk, tm), im(lambda i, j, k: (k, i))) if ta else pl.BlockSpec((tm, tk), im(lambda i, j, k: (i, k)))
    b_spec = pl.BlockSpec((tn, tk), im(lambda i, j, k: (j, k))) if tb else pl.BlockSpec((tk, tn), im(lambda i, j, k: (k, j)))
    o_spec = pl.BlockSpec((tm, tn), im(lambda i, j, k: (i, j)))
    dn = (((0 if ta else 1,), (1 if tb else 0,)), ((), ()))
    has_add = add is not None

    def body(*refs):
        if has_add:
            a_ref, b_ref, add_ref, o_ref, acc = refs
        else:
            a_ref, b_ref, o_ref, acc = refs
        k = pl.program_id(2)

        @pl.when(k == 0)
        def _():
            acc[...] = jnp.zeros(acc.shape, F32)

        acc[...] += lax.dot_general(a_ref[...].astype(BF16), b_ref[...].astype(BF16), dn, preferred_element_type=F32)

        @pl.when(k == nk - 1)
        def _():
            r = acc[...]
            if has_add:
                r = r + add_ref[...]
            o_ref[...] = r.astype(out_dtype)

    args, specs = [a, b], [a_spec, b_spec]
    if has_add:
        args.append(add)
        specs.append(o_spec)
    return pl.pallas_call(
        body, name=name, grid=(gj, gi, nk) if j_outer else (gi, gj, nk), in_specs=specs, out_specs=o_spec,
        out_shape=jax.ShapeDtypeStruct((M, N), out_dtype), scratch_shapes=[pltpu.VMEM((tm, tn), F32)],
        compiler_params=_params(("parallel", "parallel", "arbitrary")))(*args)


S5_BW = 2 * S5_STATE * (S5_GROUPS // S5_BLOCKS)


def _s5_tile(n_rows):
    return _pick(n_rows, 256)


def _scan_tile(buf, tab, carry, *, reverse, x_ref=None, da_ref=None):
    a8_ref, pw_ref, ak_ref, ap_ref = tab
    tt, bw = buf.shape
    hw = bw // 2
    seg = tt // SUBLANES
    ar, ai = a8_ref[:, :hw], a8_ref[:, hw:]

    def rows(p):
        return pl.ds(pl.multiple_of(((seg - 1 - p) if reverse else p) * SUBLANES, SUBLANES), SUBLANES)

    def local(p, c):
        xr, xi = c
        br = buf[rows(p), :hw] + ar * xr - ai * xi
        bi = buf[rows(p), hw:] + ar * xi + ai * xr
        buf[rows(p), :hw] = br
        buf[rows(p), hw:] = bi
        return br, bi

    zero = jnp.zeros((SUBLANES, hw), F32)
    xr, xi = lax.fori_loop(0, seg, local, (zero, zero), unroll=2)

    rowid = lax.broadcasted_iota(jnp.int32, (SUBLANES, hw), 0)
    for n, k in enumerate((1, 2, 4)):
        kr, ki = ak_ref[SUBLANES * n:SUBLANES * (n + 1), :hw], ak_ref[SUBLANES * n:SUBLANES * (n + 1), hw:]
        sh = SUBLANES - k if reverse else k
        sr, si = pltpu.roll(xr, sh, 0), pltpu.roll(xi, sh, 0)
        xr, xi = xr + kr * sr - ki * si, xi + kr * si + ki * sr
    cr, ci = carry[:, :hw], carry[:, hw:]
    xr, xi = xr + ap_ref[:, :hw] * cr - ap_ref[:, hw:] * ci, xi + ap_ref[:, :hw] * ci + ap_ref[:, hw:] * cr
    first = (rowid == SUBLANES - 1) if reverse else (rowid == 0)
    sh1 = SUBLANES - 1 if reverse else 1
    cr = jnp.where(first, cr, pltpu.roll(xr, sh1, 0))
    ci = jnp.where(first, ci, pltpu.roll(xi, sh1, 0))
    last = 0 if reverse else SUBLANES - 1
    carry[:, :hw] = jnp.broadcast_to(xr[last:last + 1], (SUBLANES, hw))
    carry[:, hw:] = jnp.broadcast_to(xi[last:last + 1], (SUBLANES, hw))

    with_da = x_ref is not None

    def fix(p, c):
        pr, pi, dr, di = c
        w0 = pl.ds(pl.multiple_of(p * SUBLANES, SUBLANES), SUBLANES)
        wr, wi = pw_ref[w0, :hw], pw_ref[w0, hw:]
        fr = buf[rows(p), :hw] + wr * cr - wi * ci
        fi = buf[rows(p), hw:] + wr * ci + wi * cr
        buf[rows(p), :hw] = fr
        buf[rows(p), hw:] = fi
        if with_da:
            sr, si = x_ref[rows(p), :hw], x_ref[rows(p), hw:]
            dr, di = dr + pr * sr + pi * si, di + pi * sr - pr * si
        return fr, fi, dr, di

    _, _, dr, di = lax.fori_loop(0, seg, fix, (cr, ci, zero, zero), unroll=2)
    if with_da:
        da_ref[:, :hw] += dr
        da_ref[:, hw:] += di


def _unpermute(pt, v):
    hi = v.astype(BF16)
    r1 = v - hi.astype(F32)
    mid = r1.astype(BF16)
    lo = (r1 - mid.astype(F32)).astype(BF16)
    return (jnp.dot(pt, hi, preferred_element_type=F32) + jnp.dot(pt, mid, preferred_element_type=F32)
            + jnp.dot(pt, lo, preferred_element_type=F32))


def _s5_specs(n_rows, reverse):
    tt = _s5_tile(n_rows)
    nt = n_rows // tt

    def rows(width, off):
        return pl.BlockSpec((tt, width), lambda j, t: ((nt - 1 - t) if reverse else t, off + j))

    def per_block(r, c):
        return pl.BlockSpec((None, r, c), lambda j, t: (j, 0, 0))

    def par(r):
        return pl.BlockSpec((r, S5_BW), lambda j, t: (0, j))

    whole = pl.BlockSpec((tt, tt), lambda j, t: (0, 0))
    tabs = [par(SUBLANES), par(tt), par(3 * SUBLANES), par(SUBLANES)]
    return tt, nt, rows, per_block, par, whole, tabs


def _s5_call(name, core, args, in_specs, out_specs, out_shape, scratch, nt, side):
    n_out = len(out_shape)
    if side is None:
        body, sem = core, ("parallel", "arbitrary")
    else:
        make, shapes, sems = _exchange_parts(side)
        n, n_in, n_sc = len(shapes), len(args), len(scratch)
        hbm = pl.BlockSpec(memory_space=pl.ANY)

        def body(*refs):
            ins, xs = refs[:n_in], refs[n_in:n_in + n]
            outs, lands = refs[n_in + n:n_in + n + n_out], refs[n_in + n + n_out:n_in + 2 * n + n_out]
            sc = refs[n_in + 2 * n + n_out:n_in + 2 * n + n_out + n_sc]
            start, forward, finish = make(xs, lands, *refs[-3:])
            j, t = pl.program_id(0), pl.program_id(1)
            pl.when((j == 0) & (t == 0))(start)
            pl.when((j == S5_BLOCKS // 2) & (t == 0))(forward)
            core(*ins, *outs, *sc)
            pl.when((j == S5_BLOCKS - 1) & (t == nt - 1))(finish)

        args, in_specs = list(args) + list(side[1]), list(in_specs) + [hbm] * n
        out_specs, out_shape = list(out_specs) + [hbm] * n, list(out_shape) + shapes
        scratch, sem = list(scratch) + sems, ("arbitrary", "arbitrary")
    res = pl.pallas_call(body, name=name, grid=(S5_BLOCKS, nt), in_specs=in_specs, out_specs=out_specs, out_shape=out_shape,
                         scratch_shapes=scratch, compiler_params=_params(sem))(*args)
    return list(res[:n_out]) + [None if side is None else list(res[n_out:])]


def _s5_fwd(name, z, m, perm, *, reverse, side=None):
    n_rows = z.shape[0]
    tt, nt, rows, per_block, par, whole, tabs = _s5_specs(n_rows, reverse)

    def body(u_ref, p_ref, pt_ref, b_ref, c_ref, a8_ref, pw_ref, ak_ref, ap_ref, x_ref, y_ref, carry):
        @pl.when(pl.program_id(1) == 0)
        def _():
            carry[...] = jnp.zeros(carry.shape, F32)

        u_p = jnp.dot(p_ref[...], u_ref[...].astype(BF16), preferred_element_type=F32).astype(BF16)
        x_ref[...] = jnp.dot(u_p, b_ref[...].astype(BF16), preferred_element_type=F32)
        _scan_tile(x_ref, (a8_ref, pw_ref, ak_ref, ap_ref), carry, reverse=reverse)
        y_p = jnp.dot(x_ref[...].astype(BF16), c_ref[...].astype(BF16), preferred_element_type=F32)
        y_ref[...] = _unpermute(pt_ref[...], y_p)

    return _s5_call(
        name, body, [z, *perm, m["b_map"], m["c_map"], *m["tab"]],
        [rows(LANES, Z_U // LANES), whole, whole, per_block(LANES, S5_BW), per_block(S5_BW, LANES)] + tabs,
        [rows(S5_BW, 0), rows(LANES, 0)],
        [jax.ShapeDtypeStruct((n_rows, S5_BLOCKS * S5_BW), F32), jax.ShapeDtypeStruct((n_rows, S5_BLOCKS * LANES), F32)],
        [pltpu.VMEM((SUBLANES, S5_BW), F32)], nt, side)


def _s5_bwd(name, z, dy, xs, m, perm, *, reverse, side=None):
    n_rows = z.shape[0]
    tt, nt, rows, per_block, par, whole, tabs = _s5_specs(n_rows, reverse)

    def body(u_ref, dy_ref, x_ref, p_ref, pt_ref, bt_ref, ct_ref, a8_ref, pw_ref, ak_ref, ap_ref,
             du_ref, db_ref, dc_ref, da_ref, lam, carry):
        t = pl.program_id(1)

        @pl.when(t == 0)
        def _():
            carry[...] = jnp.zeros(carry.shape, F32)
            db_ref[...] = jnp.zeros(db_ref.shape, F32)
            dc_ref[...] = jnp.zeros(dc_ref.shape, F32)
            da_ref[...] = jnp.zeros(da_ref.shape, F32)

        perm_b = p_ref[...]
        dy_p = jnp.dot(perm_b, dy_ref[...].astype(BF16), preferred_element_type=F32).astype(BF16)
        u_p = jnp.dot(perm_b, u_ref[...].astype(BF16), preferred_element_type=F32).astype(BF16)
        lam[...] = jnp.dot(dy_p, ct_ref[...].astype(BF16), preferred_element_type=F32)
        _scan_tile(lam, (a8_ref, pw_ref, ak_ref, ap_ref), carry, reverse=reverse, x_ref=x_ref, da_ref=da_ref)
        lam_b = lam[...].astype(BF16)
        du_ref[...] = _unpermute(pt_ref[...], jnp.dot(lam_b, bt_ref[...].astype(BF16), preferred_element_type=F32))
        db_ref[...] += lax.dot_general(u_p, lam_b, _TN, preferred_element_type=F32)
        dc_ref[...] += lax.dot_general(x_ref[...].astype(BF16), dy_p, _TN, preferred_element_type=F32)

        @pl.when(t == nt - 1)
        def _():
            da_ref[...] = jnp.broadcast_to(jnp.sum(da_ref[...], axis=0, keepdims=True), da_ref.shape)

    return _s5_call(
        name, body, [z, dy, xs, *perm, m["b_map_t"], m["c_map_t"], *m["tab_adj"]],
        [rows(LANES, Z_U // LANES), rows(LANES, 0), rows(S5_BW, 0), whole, whole,
         per_block(S5_BW, LANES), per_block(LANES, S5_BW)] + tabs,
        [rows(LANES, 0), per_block(LANES, S5_BW), per_block(S5_BW, LANES), par(SUBLANES)],
        [jax.ShapeDtypeStruct((n_rows, S5_BLOCKS * LANES), F32), jax.ShapeDtypeStruct((S5_BLOCKS, LANES, S5_BW), F32),
         jax.ShapeDtypeStruct((S5_BLOCKS, S5_BW, LANES), F32), jax.ShapeDtypeStruct((SUBLANES, S5_BLOCKS * S5_BW), F32)],
        [pltpu.VMEM((tt, S5_BW), F32), pltpu.VMEM((SUBLANES, S5_BW), F32)], nt, side)


def _ret_chunk(zq, zk, v, cosf, sinf, state, rd, reverse):
    c = RET_CHUNK
    lg = jax.nn.log_sigmoid(rd)
    lg1 = jnp.max(lg, axis=1, keepdims=True)
    q = _rope(zq, cosf, sinf) * (RET_QK ** -0.5)
    k = _rope(zk, cosf, sinf)
    pi = lax.broadcasted_iota(jnp.int32, (c, c), 0).astype(F32)
    pj = lax.broadcasted_iota(jnp.int32, (c, c), 1).astype(F32)
    pcol = lax.broadcasted_iota(jnp.int32, (c, 1), 0).astype(F32)
    if reverse:
        diff, mask, pos = pj - pi, pj > pi, (c - 1) - pcol
    else:
        diff, mask, pos = pi - pj, pi >= pj, pcol
    decay_in = jnp.where(mask, jnp.exp(jnp.where(mask, diff, 0.0) * lg), 0.0)
    scores = _bdot(q, k, 1, 1) * decay_in
    inner = _bdot(scores, v, 1, 0)
    k_w = jnp.exp((c - 1 - pos) * lg1)
    kv = _bdot(k * k_w, v, 0, 0)
    q_w = jnp.exp((pos + 1) * lg1)
    cross = _bdot(q, state, 1, 0) * q_w
    new_state = jnp.exp(c * lg1) * state + kv
    return inner + cross, new_state


RET_ZW = RET_HEADS * (2 * RET_QK + RET_V)


def _ret_specs(n_chunks, reverse_order):
    cmap = (lambda n: n_chunks - 1 - n) if reverse_order else (lambda n: n)
    z_spec = pl.BlockSpec((RET_CHUNK, RET_ZW), lambda n: (cmap(n), 0))
    t_spec = pl.BlockSpec((RET_CHUNK, LANES), lambda n: (cmap(n), 0))
    rd_spec = pl.BlockSpec((RET_HEADS, 1, LANES), lambda n: (0, 0, 0))
    o_spec = pl.BlockSpec((RET_CHUNK, RET_HEADS * RET_V), lambda n: (cmap(n), 0))
    st_spec = pl.BlockSpec((RET_HEADS, None, RET_QK, RET_V), lambda n: (0, cmap(n), 0, 0))
    return z_spec, t_spec, rd_spec, o_spec, st_spec


def _ret_head(zt, h):
    b = h * (2 * RET_QK + RET_V)
    return zt[:, b:b + RET_QK], zt[:, b + RET_QK:b + 2 * RET_QK], zt[:, b + 2 * RET_QK:b + 2 * RET_QK + RET_V]


def _ret_fwd(name, z, cosf, sinf, rd, *, reverse):
    n_rows = z.shape[0]
    n_chunks = n_rows // RET_CHUNK
    z_spec, t_spec, rd_spec, o_spec, st_spec = _ret_specs(n_chunks, reverse)

    def body(z_ref, cos_ref, sin_ref, rd_ref, o_ref, st_ref, state):
        @pl.when(pl.program_id(0) == 0)
        def _():
            state[...] = jnp.zeros(state.shape, F32)

        zt = z_ref[...]
        cosv, sinv = cos_ref[...], sin_ref[...]
        for h in range(RET_HEADS):
            st = state[h]
            st_ref[h] = st
            out, new = _ret_chunk(*_ret_head(zt, h), cosv, sinv, st, rd_ref[h], reverse)
            o_ref[:, RET_V * h:RET_V * (h + 1)] = out
            state[h] = new

    return pl.pallas_call(
        body, name=name, grid=(n_chunks,), in_specs=[z_spec, t_spec, t_spec, rd_spec],
        out_specs=[o_spec, st_spec],
        out_shape=[jax.ShapeDtypeStruct((n_rows, RET_HEADS * RET_V), F32),
                   jax.ShapeDtypeStruct((RET_HEADS, n_chunks, RET_QK, RET_V), F32)],
        scratch_shapes=[pltpu.VMEM((RET_HEADS, RET_QK, RET_V), F32)], compiler_params=_params(("arbitrary",)))(z, cosf, sinf, rd)


def _ret_bwd(name, z, cosf, sinf, rd, states, dout, dz, *, reverse):
    n_rows = z.shape[0]
    n_chunks = n_rows // RET_CHUNK
    z_spec, t_spec, rd_spec, o_spec, st_spec = _ret_specs(n_chunks, not reverse)

    def body(z_ref, cos_ref, sin_ref, rd_ref, st_ref, do_ref, dzin_ref, dz_ref, drd_ref, dstate):
        @pl.when(pl.program_id(0) == 0)
        def _():
            dstate[...] = jnp.zeros(dstate.shape, F32)
            drd_ref[...] = jnp.zeros(drd_ref.shape, F32)

        zt = z_ref[...]
        cosv, sinv = cos_ref[...], sin_ref[...]
        parts = []
        for h in range(RET_HEADS):
            _, vjp = jax.vjp(lambda a, b, c, s, r: _ret_chunk(a, b, c, cosv, sinv, s, r, reverse),
                             *_ret_head(zt, h), st_ref[h], rd_ref[h])
            dq, dk, dv, dst, drd = vjp((do_ref[:, RET_V * h:RET_V * (h + 1)], dstate[h]))
            parts += [dq, dk, dv]
            dstate[h] = dst
            drd_ref[h] += jnp.sum(drd, axis=1, keepdims=True)
        dz_ref[...] = (dzin_ref[...].astype(F32) + jnp.concatenate(parts, axis=1)).astype(dz_ref.dtype)

    return pl.pallas_call(
        body, name=name, grid=(n_chunks,),
        in_specs=[z_spec, t_spec, t_spec, rd_spec, st_spec, o_spec, z_spec],
        out_specs=[z_spec, rd_spec],
        out_shape=[jax.ShapeDtypeStruct(dz.shape, dz.dtype), jax.ShapeDtypeStruct((RET_HEADS, 1, LANES), F32)],
        input_output_aliases={6: 0}, scratch_shapes=[pltpu.VMEM((RET_HEADS, RET_QK, RET_V), F32)],
        compiler_params=_params(("arbitrary",)))(z, cosf, sinf, rd, states, dout, dz)


_NT = (((1,), (1,)), ((), ()))
_TN = (((0,), (0,)), ((), ()))


def _attn_tiles(n_rows, tq_cap):
    tq, tk = _pick(n_rows, tq_cap), _pick(n_rows, 2048)
    return tq, max(tq // 2, LANES), tk, min(tk, 1024)


def _attn_fwd(name, q, kv, kr):
    n_rows = q.shape[0]
    tq, hq, tk, sub = _attn_tiles(n_rows, 512)
    nk = n_rows // tk

    def body(q_ref, kn_ref, v_ref, kr_ref, o_ref, lse_ref, m_sc, acc):
        j = pl.program_id(2)

        @pl.when(j == 0)
        def _():
            m_sc[...] = jnp.full(m_sc.shape, -jnp.inf, F32)
            acc[...] = jnp.zeros(acc.shape, F32)

        for c in range(tk // sub):
            rows = slice(c * sub, (c + 1) * sub)
            k = jnp.concatenate([kn_ref[rows, :], kr_ref[rows, :]], axis=1)
            v1 = jnp.concatenate([v_ref[rows, :], jnp.ones((sub, LANES), BF16)], axis=1)
            for part in range(tq // hq):
                qr = slice(part * hq, (part + 1) * hq)
                s = lax.dot_general(q_ref[qr, :], k, _NT, preferred_element_type=F32)
                m_prev = m_sc[qr, :]
                m_new = jnp.maximum(m_prev, jnp.max(s, axis=1, keepdims=True))
                p = jnp.exp(s - m_new)
                acc[qr, :] = jnp.exp(m_prev - m_new) * acc[qr, :] + jnp.dot(p.astype(BF16), v1, preferred_element_type=F32)
                m_sc[qr, :] = m_new

        @pl.when(j == nk - 1)
        def _():
            l = acc[:, LANES:]
            o_ref[...] = acc[:, :LANES] / l
            lse_ref[...] = m_sc[...] + jnp.log(l)

    return pl.pallas_call(
        body, name=name, grid=(MLA_HEADS, n_rows // tq, nk),
        in_specs=[pl.BlockSpec((tq, 256), lambda h, i, j: (i, h)),
                  pl.BlockSpec((tk, 128), lambda h, i, j: (j, 2 * h)),
                  pl.BlockSpec((tk, 128), lambda h, i, j: (j, 2 * h + 1)),
                  pl.BlockSpec((tk, 128), lambda h, i, j: (j, 0))],
        out_specs=[pl.BlockSpec((tq, 128), lambda h, i, j: (i, h)),
                   pl.BlockSpec((None, tq, 128), lambda h, i, j: (h, i, 0))],
        out_shape=[jax.ShapeDtypeStruct((n_rows, MLA_HEADS * MLA_V), F32),
                   jax.ShapeDtypeStruct((MLA_HEADS, n_rows, LANES), F32)],
        scratch_shapes=[pltpu.VMEM((tq, 1), F32), pltpu.VMEM((tq, 2 * LANES), F32)],
        compiler_params=_params(("parallel", "parallel", "arbitrary")))(q, kv, kv, kr)


def _attn_bwd(name, q, kv, kr, o, lse, do):
    n_rows = q.shape[0]
    tq, hq, tk, sub = _attn_tiles(n_rows, 1024)
    nq = n_rows // tq

    def body(q_ref, kn_ref, v_ref, kr_ref, o_ref, lse_ref, do_ref, dq_ref, dkv_ref, dkr_ref, dk_acc, dv_acc):
        j, i = pl.program_id(1), pl.program_id(2)

        @pl.when(i == 0)
        def _():
            dk_acc[...] = jnp.zeros(dk_acc.shape, F32)
            dv_acc[...] = jnp.zeros(dv_acc.shape, F32)

        @pl.when((i == 0) & (j == 0))
        def _():
            dq_ref[...] = jnp.zeros(dq_ref.shape, F32)

        for part in range(tq // hq):
            qr = slice(part * hq, (part + 1) * hq)
            qv = q_ref[qr, :]
            do = do_ref[qr, :]
            do_b = do.astype(BF16)
            delta = jnp.sum(do * o_ref[qr, :], axis=1, keepdims=True)
            lse_col = lse_ref[qr, :][:, :1]
            dq = None
            for c in range(tk // sub):
                rows = slice(c * sub, (c + 1) * sub)
                k = jnp.concatenate([kn_ref[rows, :], kr_ref[rows, :]], axis=1)
                s = lax.dot_general(qv, k, _NT, preferred_element_type=F32)
                p = jnp.exp(s - lse_col)
                dp = lax.dot_general(do_b, v_ref[rows, :], _NT, preferred_element_type=F32)
                ds = (p * (dp - delta)).astype(BF16)
                dv_acc[rows, :] += lax.dot_general(p.astype(BF16), do_b, _TN, preferred_element_type=F32)
                dk_acc[rows, :] += lax.dot_general(ds, qv, _TN, preferred_element_type=F32)
                t = jnp.dot(ds, k, preferred_element_type=F32)
                dq = t if dq is None else dq + t
            r0 = pl.multiple_of(i * tq + part * hq, hq)
            dq_ref[pl.ds(r0, hq), :] += dq

        @pl.when(i == nq - 1)
        def _():
            dkv_ref[...] = jnp.concatenate([dk_acc[:, :128], dv_acc[...]], axis=1).astype(dkv_ref.dtype)
            dkr_ref[...] = dk_acc[:, 128:]

    return pl.pallas_call(
        body, name=name, grid=(MLA_HEADS, n_rows // tk, nq),
        in_specs=[pl.BlockSpec((tq, 256), lambda h, j, i: (i, h)),
                  pl.BlockSpec((tk, 128), lambda h, j, i: (j, 2 * h)),
                  pl.BlockSpec((tk, 128), lambda h, j, i: (j, 2 * h + 1)),
                  pl.BlockSpec((tk, 128), lambda h, j, i: (j, 0)),
                  pl.BlockSpec((tq, 128), lambda h, j, i: (i, h)),
                  pl.BlockSpec((None, tq, 128), lambda h, j, i: (h, i, 0)),
                  pl.BlockSpec((tq, 128), lambda h, j, i: (i, h))],
        out_specs=[pl.BlockSpec((n_rows, 256), lambda h, j, i: (0, h)),
                   pl.BlockSpec((tk, 256), lambda h, j, i: (j, h)),
                   pl.BlockSpec((tk, 128), lambda h, j, i: (j, h))],
        out_shape=[jax.ShapeDtypeStruct((n_rows, MLA_HEADS * 256), F32), jax.ShapeDtypeStruct((n_rows, MLA_HEADS * 256), BF16),
                   jax.ShapeDtypeStruct((n_rows, MLA_HEADS * 128), F32)],
        scratch_shapes=[pltpu.VMEM((tk, 256), F32), pltpu.VMEM((tk, 128), F32)],
        compiler_params=_params(("parallel", "arbitrary", "arbitrary")))(q, kv, kv, kr, o, lse, do)


_MESH = pl.DeviceIdType.MESH


def _gather_stages(x_refs, out_refs, send_sems, recv_sems, local_sems):
    n = len(x_refs)
    mx, my, mc = lax.axis_index("x"), lax.axis_index("y"), lax.axis_index("c")
    me, sibling = (mx, my, mc), (mx, my, 1 - mc)
    chips = [(1 - mx, my), (mx, 1 - my), (1 - mx, 1 - my)]

    def copy(a, k, block, to, src=None):
        dst = out_refs[a].at[4 * block[0] + 2 * block[1] + block[2]]
        return pltpu.make_async_remote_copy(
            src_ref=dst if src is None else src, dst_ref=dst, send_sem=send_sems.at[7 * a + k],
            recv_sem=recv_sems.at[7 * a + k], device_id=to, device_id_type=_MESH)

    def mine():
        return [pltpu.make_async_copy(x_refs[a], out_refs[a].at[4 * mx + 2 * my + mc], local_sems.at[a]) for a in range(n)]

    def first():
        out = []
        for a in range(n):
            out.append(copy(a, 0, me, sibling, src=x_refs[a]))
            out += [copy(a, 1 + j, me, (*chip, mc), src=x_refs[a]) for j, chip in enumerate(chips)]
        return out

    def passed():
        return [copy(a, 4 + j, (*chip, mc), sibling) for j, chip in enumerate(chips) for a in range(n)]

    def start():
        for cp in mine() + first():
            cp.start()

    def forward():
        for j, chip in enumerate(chips):
            for a in range(n):
                copy(a, 1 + j, (*chip, mc), me).wait_recv()
        for cp in passed():
            cp.start()

    def finish():
        for a in range(n):
            copy(a, 0, sibling, me).wait_recv()
            for j, chip in enumerate(chips):
                copy(a, 4 + j, (*chip, 1 - mc), me).wait_recv()
        for cp in first() + passed():
            cp.wait_send()
        for cp in mine():
            cp.wait()

    return start, forward, finish


def _exchange_stages(g_refs, land_refs, send_sems, recv_sems, local_sems):
    n = len(g_refs)
    mx, my, mc = lax.axis_index("x"), lax.axis_index("y"), lax.axis_index("c")
    me = 4 * mx + 2 * my + mc

    def mine():
        return [pltpu.make_async_copy(g_refs[a].at[me], land_refs[a].at[me], local_sems.at[a]) for a in range(n)]

    def copies():
        out = []
        for k in range(1, N_DEV):
            px = 1 - mx if k & 4 else mx
            py = 1 - my if k & 2 else my
            pc = 1 - mc if k & 1 else mc
            peer = 4 * px + 2 * py + pc
            for a in range(n):
                sems = dict(send_sem=send_sems.at[7 * a + k - 1], recv_sem=recv_sems.at[7 * a + k - 1],
                            device_id=(px, py, pc), device_id_type=_MESH)
                out.append((pltpu.make_async_remote_copy(src_ref=g_refs[a].at[peer], dst_ref=land_refs[a].at[me], **sems),
                            pltpu.make_async_remote_copy(src_ref=g_refs[a].at[peer], dst_ref=land_refs[a].at[peer], **sems)))
        return out

    def start():
        for cp in mine():
            cp.start()
        for send, _ in copies():
            send.start()

    def finish():
        both = copies()
        for _, recv in both:
            recv.wait_recv()
        for send, _ in both:
            send.wait_send()
        for cp in mine():
            cp.wait()

    return start, lambda: None, finish


def _exchange_parts(side):
    kind, arrays = side
    n = len(arrays)
    if kind == "gather":
        make, shapes = _gather_stages, [jax.ShapeDtypeStruct((N_DEV,) + a.shape, a.dtype) for a in arrays]
    else:
        make, shapes = _exchange_stages, [jax.ShapeDtypeStruct(a.shape, a.dtype) for a in arrays]
    sems = [pltpu.SemaphoreType.DMA((7 * n,)), pltpu.SemaphoreType.DMA((7 * n,)), pltpu.SemaphoreType.DMA((n,))]
    return make, shapes, sems


def _exchange(name, side):
    make, shapes, sems = _exchange_parts(side)
    n = len(shapes)

    def body(*refs):
        start, forward, finish = make(refs[:n], refs[n:2 * n], *refs[2 * n:])
        start()
        forward()
        finish()

    hbm = pl.BlockSpec(memory_space=pl.ANY)
    return pl.pallas_call(body, name=name, out_shape=shapes, in_specs=[hbm] * n, out_specs=[hbm] * n,
                          scratch_shapes=sems)(*side[1])


def _all_gather(name, xs):
    return _exchange(name, ("gather", xs))


def _all_to_all(name, gs):
    return _exchange(name, ("exchange", gs))


def _adamw(name, parts, w, m, v, tr):
    rows, cols = w.shape

    def body(p_ref, w_ref, m_ref, v_ref, g_ref, d_ref, nm_ref, nv_ref):
        g = p_ref[0].astype(F32)
        for d in range(1, N_DEV):
            g = g + p_ref[d].astype(F32)
        nm = ADAM_B1 * m_ref[...] + (1.0 - ADAM_B1) * g
        nv = ADAM_B2 * v_ref[...] + (1.0 - ADAM_B2) * jnp.square(g)
        m_hat = nm / (1.0 - ADAM_B1 ** ADAM_STEP)
        v_hat = nv / (1.0 - ADAM_B2 ** ADAM_STEP)
        g_ref[...] = g
        d_ref[...] = -ADAM_LR * (m_hat / (jnp.sqrt(v_hat) + ADAM_EPS) + ADAM_WD * w_ref[...])
        nm_ref[...] = nm
        nv_ref[...] = nv

    spec = pl.BlockSpec((tr, cols), lambda i: (i, 0))
    return pl.pallas_call(
        body, name=name, grid=(rows // tr,),
        in_specs=[pl.BlockSpec((N_DEV, tr, cols), lambda i: (0, i, 0)), spec, spec, spec],
        out_specs=[spec] * 4, out_shape=[jax.ShapeDtypeStruct((rows, cols), F32)] * 4,
        compiler_params=_params(("parallel",)))(parts, w, m, v)


def _in_pieces():
    p = []
    for h in range(RET_HEADS):
        p += [(128 * h, 128 * h + 128), (512 + 128 * h, 512 + 128 * h + 128), (1024 + 256 * h, 1024 + 256 * h + 256)]
    p += [(2048, 3072), (3776, 4800), (4800, 7872), (3072, 3456), 128, (3456, 3712),
          (3712, 3744), 32, (3744, 3776), 32, 128]
    return p


def _uq_pieces():
    p = []
    for h in range(MLA_HEADS):
        b = 192 * h
        p += [(b, b + 128), (b + 128, b + 160), 32, (b + 160, b + 192), 32]
    return p


def _perm(w, pieces):
    cols = [jnp.zeros(w.shape[:-1] + (p,), w.dtype) if isinstance(p, int) else w[..., p[0]:p[1]] for p in pieces]
    return jnp.concatenate(cols, axis=-1)


def _unperm(dw, pieces):
    found, off = [], 0
    for p in pieces:
        if isinstance(p, int):
            off += p
        else:
            found.append((p[0], dw[..., off:off + p[1] - p[0]]))
            off += p[1] - p[0]
    return jnp.concatenate([t for _, t in sorted(found, key=lambda s: s[0])], axis=-1)


def _unshard(blocks, axis):
    return jnp.concatenate([blocks[p] for p in range(N_DEV)], axis=axis)


def _shard_split(full, axis):
    return jnp.stack(jnp.split(full, N_DEV, axis=axis), axis=0)


def _row_tile(rows, cap=256, unit=16):
    return max(t for t in range(unit, cap + 1, unit) if rows % t == 0)


def _rope_tables(seq):
    pos = jnp.arange(seq, dtype=F32)[:, None]

    def table(dim):
        inv = 1.0 / (ROPE_THETA ** (jnp.arange(0, dim, 2, dtype=F32) / dim))
        ang = pos * inv[None, :]
        return jnp.cos(ang), jnp.sin(ang)

    cr, sr = table(RET_QK)
    cm, sm = table(MLA_ROPE)
    z = jnp.zeros_like(cm)
    return (jnp.concatenate([cr, cr], 1), jnp.concatenate([-sr, sr], 1),
            jnp.concatenate([cm, z, cm, z], 1), jnp.concatenate([-sm, z, sm, z], 1))


def _s5_maps(a_re, a_im, log_dt, b_re, b_im, c_re, c_im):
    dt = jnp.exp(log_dt)[:, None]
    ar = jnp.minimum(a_re, -1e-4)
    mag = jnp.exp(dt * ar)
    abar_re = mag * jnp.cos(dt * a_im)
    abar_im = mag * jnp.sin(dt * a_im)
    den = ar * ar + a_im * a_im
    nr = abar_re - 1.0
    ni = abar_im
    coef_re = (nr * ar + ni * a_im) / den
    coef_im = (ni * ar - nr * a_im) / den
    bb_re = coef_re[..., None] * b_re - coef_im[..., None] * b_im
    bb_im = coef_re[..., None] * b_im + coef_im[..., None] * b_re
    eye = jnp.eye(S5_BLOCKS, dtype=F32)

    def in_blocks(bb):
        t = bb.transpose(0, 2, 1).reshape(S5_BLOCKS, 8, S5_GROUP, S5_STATE)
        return jnp.einsum('jgcp,gh->jgchp', t, eye).reshape(S5_BLOCKS, 128, 512)

    def out_blocks(cc):
        t = cc.transpose(0, 2, 1).reshape(S5_BLOCKS, 8, S5_STATE, S5_GROUP)
        return jnp.einsum('jgpc,gh->jgphc', t, eye).reshape(S5_BLOCKS, 512, 128)

    arow = jnp.concatenate([abar_re.reshape(S5_BLOCKS, 512), abar_im.reshape(S5_BLOCKS, 512)], axis=1).reshape(1, -1)
    b_map = jnp.concatenate([in_blocks(bb_re), in_blocks(bb_im)], axis=2)
    c_map = jnp.concatenate([out_blocks(c_re), -out_blocks(c_im)], axis=1)
    return arow, b_map, c_map


def _power_tables(arow, conj, reverse, seg):
    a = arow.reshape(S5_BLOCKS, 2, 512)
    ar, ai = a[:, 0], (-a[:, 1] if conj else a[:, 1])

    def powers(br, bi, count):
        out = [(br, bi)]
        for _ in range(count - 1):
            pr, pi = out[-1]
            out.append((pr * br - pi * bi, pr * bi + pi * br))
        return out

    def rows(pairs):
        return jnp.stack([jnp.stack(list(p), axis=1) for p in pairs], axis=0).reshape(len(pairs), -1)

    pw = powers(ar, ai, seg)
    big = powers(*pw[-1], SUBLANES)
    rowid = jnp.arange(SUBLANES)[:, None]
    ak = jnp.concatenate([jnp.where((rowid < SUBLANES - k) if reverse else (rowid >= k), rows([big[k - 1]]), 0.0)
                          for k in (1, 2, 4)], axis=0)
    ap = rows(big[::-1] if reverse else big)
    return rows([pw[0]] * SUBLANES), jnp.repeat(rows(pw), SUBLANES, axis=0), ak, ap


def _perm_matrices(tt):
    seg = tt // SUBLANES
    idx = jnp.arange(tt)
    src = (idx % SUBLANES) * seg + idx // SUBLANES
    p = (src[:, None] == idx[None, :]).astype(BF16)
    return p, p.T


def _rows(arr):
    return (arr, arr.shape[1], 0)


def _vjp_rows(f, n_prim):
    def fn(r, c, _):
        _, vjp = jax.vjp(f, *r[:n_prim])
        return list(vjp(r[n_prim])), []
    return fn


def _norm_bwd(r, c, _):
    _, vjp = jax.vjp(_f_rms, r[0], c[0])
    dx, dg = vjp(r[1])
    return [dx + r[2]], [dg]


def _layer_fwd(l, x, W, P, T, side=None):
    n = x.shape[0]
    tm, tmw = _pick(n, 256), _pick(n, 128)
    cos_r, sin_r, cos_m, sin_m = T

    def nm(s):
        return f"l{l}_{s}"

    def one(name, f, rows, consts, width, dtype=BF16, tile=tm):
        return _tile_call(nm(name), lambda r, c, _: ([f(r, c)], []), n, tile, rows, consts, [(width, dtype)])[0]

    h = one("norm1", lambda r, c: _f_rms(r[0], c[0]), [_rows(x)], [P["norm1_g"]], D_MODEL)
    z = _mm(nm("in_proj"), h, W["in"])
    of, stf = _ret_fwd(nm("ret_f"), z, cos_r, sin_r, P["rd"][0], reverse=False)
    ob, stb = _ret_fwd(nm("ret_b"), z, cos_r, sin_r, P["rd"][1], reverse=True)

    def gn(r, c, _):
        yraw = r[0] + r[1]
        ys = [_f_gn_gate(yraw[:, RET_V * i:RET_V * (i + 1)], r[2][:, RET_V * i:RET_V * (i + 1)],
                         c[0][:, RET_V * i:RET_V * (i + 1)]) for i in range(RET_HEADS)]
        return [yraw, jnp.concatenate(ys, axis=1)], []

    yraw, yret = _tile_call(nm("ret_gn"), gn, n, tm, [_rows(of), _rows(ob), (z, 1024, Z_RG // 1024)], [P["ret_gn_g"]],
                            [(1024, F32), (1024, BF16)])

    cqn = one("q_norm", lambda r, c: _f_rms(r[0][:, :MLA_Q_LORA], c[0]), [(z, 512, Z_CQ // 512)], [P["mla_q_norm_g"]], MLA_Q_LORA)
    ckvn = one("kv_norm", lambda r, c: _f_rms(r[0], c[0]), [(z, 256, Z_CKV // 256)], [P["mla_kv_norm_g"]], MLA_KV_LORA)
    qraw = _mm(nm("q_up"), cqn, W["uq"])
    kv = _mm(nm("kv_up"), ckvn, W["ukv"], out_dtype=BF16)
    q = one("q_rope", lambda r, c: _mla_q(r[0], r[1], r[2], False), [_rows(qraw), _rows(cos_m), _rows(sin_m)], [], 2048)
    kr = one("k_rope", lambda r, c: _rope(r[0], r[1], r[2]), [(z, 128, Z_KR // 128), _rows(cos_m), _rows(sin_m)], [], 128)
    o, lse = _attn_fwd(nm("attn"), q, kv, kr)

    xs, y_dir = [], []
    for d in range(2):
        x_d, y_d, res = _s5_fwd(nm(f"s5_scan{d}"), z, P["s5"][d], P["perm"], reverse=(d == 1), side=side if d == 0 else None)
        side_res = res if d == 0 else side_res
        xs.append(x_d)
        y_dir.append(y_d)

    def s5_act(r, c, _):
        ysum = r[0] + r[1]
        return [ysum, _f_s5_act(ysum, r[2], c[0])], []

    ysum, gact = _tile_call(nm("s5_act"), s5_act, n, tm, [_rows(y_dir[0]), _rows(y_dir[1]), (z, 1024, Z_U // 1024)], [P["s5_d"]],
                            [(1024, F32), (1024, BF16)])
    ga = _mm(nm("glu_a"), gact, W["glu_a"])
    gb = _mm(nm("glu_b"), gact, W["glu_b"])
    ys5 = one("glu", lambda r, c: _f_glu(r[0], r[1]), [_rows(ga), _rows(gb)], [], 1024)

    ys = (yret, o, ys5)
    ps = [_mm(nm(f"branch{i}"), ys[i], W["branch"][i]) for i in range(3)]
    mix = one("mix", lambda r, c: _f_gate(r[0], r[3]) + _f_gate(r[1], r[4]) + _f_gate(r[2], r[5]),
              [(z, 1024, Z_GATE // 1024 + i) for i in range(3)] + [_rows(p) for p in ps], [], 1024)
    x1 = _mm(nm("out_proj"), mix, W["out"], add=x)
    h2 = one("norm2", lambda r, c: _f_rms(r[0], c[0]), [_rows(x1)], [P["norm2_g"]], D_MODEL)
    gp = _mm(nm("ffn_g"), h2, W["ffn_g"])
    up = _mm(nm("ffn_u"), h2, W["ffn_u"])
    act = one("swiglu", lambda r, c: _f_swiglu(r[0], r[1]), [_rows(gp), _rows(up)], [], FFN_HIDDEN, tile=tmw)
    x2 = _mm(nm("ffn_down"), act, W["ffn_down"], add=x1)
    saved = dict(x=x, h=h, z=z, stf=stf, stb=stb, yraw=yraw, ys=ys, cqn=cqn, ckvn=ckvn, qraw=qraw, q=q, kv=kv, kr=kr,
                 lse=lse, xs=xs, ysum=ysum, gact=gact, ga=ga, gb=gb, ps=ps, mix=mix, x1=x1, h2=h2, gp=gp, up=up, act=act)
    return x2, saved, side_res


def _layer_bwd(l, dx2, sv, W, P, T, side=None):
    n = dx2.shape[0]
    tm, tmw = _pick(n, 256), _pick(n, 128)
    cos_r, sin_r, cos_m, sin_m = T
    z = sv["z"]
    g = {}

    def nm(s):
        return f"l{l}_{s}"

    dact = _mm(nm("d_act"), dx2, W["ffn_down"], tb=True)
    g["ffn_down"] = _mm(nm("dw_ffn_down"), sv["act"], dx2, ta=True)
    dgp, dup = _tile_call(nm("d_swiglu"), _vjp_rows(_f_swiglu, 2), n, tmw, [_rows(sv["gp"]), _rows(sv["up"]), _rows(dact)], [],
                          [(FFN_HIDDEN, BF16)] * 2)
    dh2 = _mm(nm("d_h2_g"), dgp, W["ffn_g"], tb=True)
    dh2 = _mm(nm("d_h2_u"), dup, W["ffn_u"], tb=True, add=dh2)
    g["ffn_g"] = _mm(nm("dw_ffn_g"), sv["h2"], dgp, ta=True)
    g["ffn_u"] = _mm(nm("dw_ffn_u"), sv["h2"], dup, ta=True)
    dx1, g["norm2_g"] = _tile_call(nm("d_norm2"), _norm_bwd, n, tm, [_rows(sv["x1"]), _rows(dh2), _rows(dx2)], [P["norm2_g"]],
                                   [(D_MODEL, F32)], acc_outs=[(1, D_MODEL)])

    dmix = _mm(nm("d_mix"), dx1, W["out"], tb=True)
    g["out"] = _mm(nm("dw_out"), sv["mix"], dx1, ta=True)
    dz = jnp.zeros((n, ZW), BF16)
    dys, g["branch"] = [], []
    for i in range(3):
        dz, dp = _tile_call(nm(f"d_gate{i}"), _vjp_rows(_f_gate, 2), n, tm,
                            [(z, 1024, Z_GATE // 1024 + i), _rows(sv["ps"][i]), _rows(dmix)], [], [(1024, BF16)],
                            alias=(dz, 1024, Z_GATE // 1024 + i))
        dys.append(_mm(nm(f"d_branch{i}"), dp, W["branch"][i], tb=True))
        g["branch"].append(_mm(nm(f"dw_branch{i}"), sv["ys"][i], dp, ta=True))

    dga, dgb = _tile_call(nm("d_glu"), _vjp_rows(_f_glu, 2), n, tm, [_rows(sv["ga"]), _rows(sv["gb"]), _rows(dys[2])], [],
                          [(1024, BF16)] * 2)
    dgact = _mm(nm("d_gact_a"), dga, W["glu_a"], tb=True)
    dgact = _mm(nm("d_gact_b"), dgb, W["glu_b"], tb=True, add=dgact)
    g["glu_a"] = _mm(nm("dw_glu_a"), sv["gact"], dga, ta=True)
    g["glu_b"] = _mm(nm("dw_glu_b"), sv["gact"], dgb, ta=True)

    def act_bwd(r, c, _):
        _, vjp = jax.vjp(_f_s5_act, r[0], r[1], c[0])
        dy, du, dd = vjp(r[2])
        return [dy, du], [dd]

    dysum, du_part, g["s5_d"] = _tile_call(nm("d_s5_act"), act_bwd, n, tm,
                                           [_rows(sv["ysum"]), (z, 1024, Z_U // 1024), _rows(dgact)], [P["s5_d"]],
                                           [(1024, F32)] * 2, acc_outs=[(1, 1024)])
    dus, g["s5"] = [], []
    for d in range(2):
        du, g_b, g_c, da, res = _s5_bwd(nm(f"d_s5_scan{d}"), z, dysum, sv["xs"][d], P["s5"][d], P["perm"], reverse=(d == 0),
                                        side=side if d == 0 else None)
        side_res = res if d == 0 else side_res
        dus.append(du)
        g["s5"].append((da[:1], g_b, g_c))
    dz, = _tile_call(nm("d_s5_u"), lambda r, c, _: ([r[0] + r[1] + r[2]], []), n, tm,
                     [_rows(du_part), _rows(dus[0]), _rows(dus[1])], [], [], alias=(dz, 1024, Z_U // 1024))

    o = sv["ys"][1]
    dq, dkv, dkr = _attn_bwd(nm("d_attn"), sv["q"], sv["kv"], sv["kr"], o, sv["lse"], dys[1])
    dqraw, = _tile_call(nm("d_q_rope"), lambda r, c, _: ([_mla_q(r[0], r[1], r[2], True)], []), n, tm,
                        [_rows(dq), _rows(cos_m), _rows(sin_m)], [], [(2048, BF16)])

    def kr_bwd(r, c, _):
        tot = r[0][:, :128]
        for h in range(1, MLA_HEADS):
            tot = tot + r[0][:, 128 * h:128 * (h + 1)]
        return [_rope_t(tot, r[1], r[2])], []

    dz, = _tile_call(nm("d_k_rope"), kr_bwd, n, tm, [_rows(dkr), _rows(cos_m), _rows(sin_m)], [], [],
                     alias=(dz, 128, Z_KR // 128))
    dcqn = _mm(nm("d_cqn"), dqraw, W["uq"], tb=True)
    g["uq"] = _mm(nm("dw_uq"), sv["cqn"], dqraw, ta=True)
    dckvn = _mm(nm("d_ckvn"), dkv, W["ukv"], tb=True)
    g["ukv"] = _mm(nm("dw_ukv"), sv["ckvn"], dkv, ta=True)

    def qn_bwd(r, c, _):
        _, vjp = jax.vjp(_f_rms, r[0][:, :MLA_Q_LORA], c[0])
        da, dg = vjp(r[1])
        return [jnp.concatenate([da, jnp.zeros((da.shape[0], 512 - MLA_Q_LORA), F32)], axis=1)], [dg]

    dz, g["mla_q_norm_g"] = _tile_call(nm("d_q_norm"), qn_bwd, n, tm, [(z, 512, Z_CQ // 512), _rows(dcqn)], [P["mla_q_norm_g"]],
                                       [], acc_outs=[(1, MLA_Q_LORA)], alias=(dz, 512, Z_CQ // 512))

    def kvn_bwd(r, c, _):
        _, vjp = jax.vjp(_f_rms, r[0], c[0])
        da, dg = vjp(r[1])
        return [da], [dg]

    dz, g["mla_kv_norm_g"] = _tile_call(nm("d_kv_norm"), kvn_bwd, n, tm, [(z, 256, Z_CKV // 256), _rows(dckvn)],
                                        [P["mla_kv_norm_g"]], [], acc_outs=[(1, MLA_KV_LORA)], alias=(dz, 256, Z_CKV // 256))

    def gn_bwd(r, c, _):
        drg, dy, dg = [], [], []
        for i in range(RET_HEADS):
            sl = slice(RET_V * i, RET_V * (i + 1))
            _, vjp = jax.vjp(_f_gn_gate, r[0][:, sl], r[1][:, sl], c[0][:, sl])
            a, b, e = vjp(r[2][:, sl])
            dy.append(a)
            drg.append(b)
            dg.append(e)
        return [jnp.concatenate(drg, axis=1), jnp.concatenate(dy, axis=1)], [jnp.concatenate(dg, axis=1)]

    dz, dyraw, g["ret_gn_g"] = _tile_call(nm("d_ret_gn"), gn_bwd, n, tm,
                                          [_rows(sv["yraw"]), (z, 1024, Z_RG // 1024), _rows(dys[0])], [P["ret_gn_g"]],
                                          [(1024, F32)], acc_outs=[(1, 1024)], alias=(dz, 1024, Z_RG // 1024))
    dz, drd_f = _ret_bwd(nm("d_ret_f"), z, cos_r, sin_r, P["rd"][0], sv["stf"], dyraw, dz, reverse=False)
    dz, drd_b = _ret_bwd(nm("d_ret_b"), z, cos_r, sin_r, P["rd"][1], sv["stb"], dyraw, dz, reverse=True)
    g["ret_decay"] = jnp.stack([drd_f[:, 0, 0], drd_b[:, 0, 0]], axis=0)

    dh = _mm(nm("d_h"), dz, W["in"], tb=True)
    g["in"] = _mm(nm("dw_in"), sv["h"], dz, ta=True)
    dx, g["norm1_g"] = _tile_call(nm("d_norm1"), _norm_bwd, n, tm, [_rows(sv["x"]), _rows(dh), _rows(dx1)], [P["norm1_g"]],
                                  [(D_MODEL, F32)], acc_outs=[(1, D_MODEL)])
    return dx, g, side_res


INPUT_NAMES = ("x",) + WEIGHTS + ("loss_target",) + tuple("m_" + n for n in WEIGHTS) + tuple("v_" + n for n in WEIGHTS)
S5_NAMES = ("s5_a_re", "s5_a_im", "s5_log_dt", "s5_b_re", "s5_b_im", "s5_c_re", "s5_c_im")


def _pack_rows(arrays, tile_rows):
    pieces = []
    for a in arrays:
        flat = a.reshape(-1)
        pad = -flat.shape[0] % LANES
        if pad:
            flat = jnp.concatenate([flat, jnp.zeros((pad,), flat.dtype)])
        pieces.append(flat.reshape(-1, LANES))
    pad = -sum(p.shape[0] for p in pieces) % tile_rows
    if pad:
        pieces.append(jnp.zeros((pad, LANES), pieces[0].dtype))
    return jnp.concatenate(pieces, axis=0)


def _unpack_rows(packed, shapes):
    out, r0 = [], 0
    for s in shapes:
        size = math.prod(s)
        rows = -(-size // LANES)
        out.append(packed[r0:r0 + rows].reshape(-1)[:size].reshape(s))
        r0 += rows
    return out


def _layer_weights(full):
    return dict(
        **{"in": _perm(full["w_in"], _in_pieces())}, uq=_perm(full["mla_w_uq"], _uq_pieces()), ukv=full["mla_w_ukv"],
        glu_a=full["s5_w_glu"][:, :1024], glu_b=full["s5_w_glu"][:, 1024:],
        branch=[full["w_branch"][i] for i in range(3)], out=full["w_out"],
        ffn_g=full["ffn_w_gu"][:, :FFN_HIDDEN], ffn_u=full["ffn_w_gu"][:, FFN_HIDDEN:], ffn_down=full["ffn_w_down"])


def _weight_grads(g):
    return dict(
        w_in=_unperm(g["in"], _in_pieces()), mla_w_uq=_unperm(g["uq"], _uq_pieces()), mla_w_ukv=g["ukv"],
        s5_w_glu=jnp.concatenate([g["glu_a"], g["glu_b"]], axis=1), w_branch=jnp.stack(g["branch"], axis=0), w_out=g["out"],
        ffn_w_gu=jnp.concatenate([g["ffn_g"], g["ffn_u"]], axis=1), ffn_w_down=g["ffn_down"])


def _local_step(inp, weights_of_layer, x, target, fwd_side=None, bwd_side=None):
    n = x.shape[0]
    tables = _rope_tables(n)
    Ps, s5_vjps = [], []
    for l in range(DEPTH):
        s5, vjps = [], []
        for d in range(2):
            (arow, b_map, c_map), vjp = jax.vjp(_s5_maps, *[inp[k][l, d] for k in S5_NAMES])
            arow = lax.stop_gradient(arow)
            seg = _s5_tile(n) // SUBLANES
            s5.append(dict(b_map=b_map, c_map=c_map, b_map_t=b_map.transpose(0, 2, 1), c_map_t=c_map.transpose(0, 2, 1),
                           tab=_power_tables(arow, False, d == 1, seg), tab_adj=_power_tables(arow, True, d == 0, seg)))
            vjps.append(vjp)
        s5_vjps.append(vjps)
        Ps.append(dict(
            norm1_g=inp["norm1_g"][l][None], norm2_g=inp["norm2_g"][l][None], ret_gn_g=inp["ret_gn_g"][l][None],
            mla_q_norm_g=inp["mla_q_norm_g"][l][None], mla_kv_norm_g=inp["mla_kv_norm_g"][l][None], s5_d=inp["s5_d"][l][None],
            rd=[jnp.broadcast_to(inp["ret_decay"][l, d][:, None, None], (RET_HEADS, 1, LANES)) for d in range(2)], s5=s5,
            perm=_perm_matrices(_s5_tile(n))))

    h, saved, Ws, side_res = x, [], [], None
    for l in range(DEPTH):
        Ws.append(_layer_weights(weights_of_layer(l, side_res)))
        h, sv, side_res = _layer_fwd(l, h, Ws[l], Ps[l], tables, side=fwd_side if l == 0 else None)
        saved.append(sv)

    def loss_bwd(r, c, _):
        loss, vjp = jax.vjp(lambda a, gain: _f_loss(a, gain, r[1]), r[0], c[0])
        da, dg = vjp(jnp.ones((), F32))
        return [da], [dg, jnp.broadcast_to(loss, (1, LANES))]

    dh, g_final, loss_row = _tile_call("loss", loss_bwd, n, _pick(n, 256), [_rows(h), _rows(target)], [inp["final_g"][None]],
                                       [(D_MODEL, F32)], acc_outs=[(1, D_MODEL), (1, LANES)])
    layer_g, wgrads, bwd_res = [None] * DEPTH, [None] * DEPTH, None
    for l in reversed(range(DEPTH)):
        side = bwd_side(wgrads[1]) if (bwd_side is not None and l == 0) else None
        dh, layer_g[l], res = _layer_bwd(l, dh, saved[l], Ws[l], Ps[l], tables, side=side)
        wgrads[l] = _weight_grads(layer_g[l])
        bwd_res = res if res is not None else bwd_res

    def stack(f):
        return jnp.stack([f(layer_g[l], l) for l in range(DEPTH)], axis=0)

    grads = dict(
        **{k: jnp.stack([wgrads[l][k] for l in range(DEPTH)], axis=0) for k in SHARDED},
        norm1_g=stack(lambda g, l: g["norm1_g"][0]), norm2_g=stack(lambda g, l: g["norm2_g"][0]),
        ret_gn_g=stack(lambda g, l: g["ret_gn_g"][0]), mla_q_norm_g=stack(lambda g, l: g["mla_q_norm_g"][0]),
        mla_kv_norm_g=stack(lambda g, l: g["mla_kv_norm_g"][0]), s5_d=stack(lambda g, l: g["s5_d"][0]),
        ret_decay=stack(lambda g, l: g["ret_decay"]), final_g=g_final[0])
    s5_grads = [[s5_vjps[l][d](layer_g[l]["s5"][d]) for d in range(2)] for l in range(DEPTH)]
    for i, k in enumerate(S5_NAMES):
        grads[k] = jnp.stack([jnp.stack([s5_grads[l][d][i] for d in range(2)], axis=0) for l in range(DEPTH)], axis=0)
    return loss_row[0, 0], dh, grads, wgrads, bwd_res


def kernel(*args):
    inp = dict(zip(INPUT_NAMES, args))
    kinds = ("grad_", "delta_", "new_m_", "new_v_")

    def local_weights(l):
        return [inp[k][l].astype(BF16) for k in SHARDED]

    def parts(wg):
        return [_shard_split(wg[k], SHARD_AXIS[k] - 1).astype(BF16) for k in SHARDED]

    gathered0 = _all_gather("gather_weights0", local_weights(0))

    def weights_of_layer(l, gathered1):
        return {k: _unshard(g, SHARD_AXIS[k] - 1) for k, g in zip(SHARDED, gathered0 if l == 0 else gathered1)}

    loss, dh, grads, wgrads, landed1 = _local_step(
        inp, weights_of_layer, inp["x"][0], inp["loss_target"][0], fwd_side=("gather", local_weights(1)),
        bwd_side=lambda wg: ("exchange", parts(wg)))
    loss = lax.psum(loss, ("x", "y", "c"))

    landed0 = _all_to_all("exchange_grads0", parts(wgrads[0]))
    out = {}
    for k, l0, l1 in zip(SHARDED, landed0, landed1):
        land = jnp.stack([l0, l1], axis=1)
        shape = inp[k].shape
        rows, cols = math.prod(shape[:-1]), shape[-1]
        res = _adamw("adamw_" + k, land.reshape(N_DEV, rows, cols), *[inp[p + k].reshape(rows, cols) for p in ("", "m_", "v_")],
                     _row_tile(rows))
        for kind, t in zip(kinds, res):
            out[kind + k] = t.reshape(shape)

    small_tr = 512
    partial = _pack_rows([grads[k] for k in SMALL], small_tr)
    packed = [_pack_rows([inp[p + k] for k in SMALL], small_tr) for p in ("", "m_", "v_")]
    res_small = _adamw("adamw_small", _all_gather("gather_small_grads", [partial])[0], *packed, small_tr)
    for kind, b in zip(kinds, res_small):
        for k, t in zip(SMALL, _unpack_rows(b, [inp[k].shape for k in SMALL])):
            out[kind + k] = t
    return (loss, dh[None]) + tuple(out[kind + k] for kind in kinds for k in WEIGHTS)
```

```python
import functools
import math

import jax
import jax.numpy as jnp
from jax import lax
from jax.experimental import pallas as pl
from jax.experimental.pallas import tpu as pltpu

F32 = jnp.float32
BF16 = jnp.bfloat16

D_MODEL = 1024
DEPTH = 2
RMS_EPS = 1e-6
GN_EPS = 1e-5
ROPE_THETA = 10000.0
RET_HEADS, RET_QK, RET_V, RET_CHUNK = 4, 128, 256, 128
MLA_HEADS, MLA_Q_LORA, MLA_KV_LORA, MLA_NOPE, MLA_ROPE, MLA_V = 8, 384, 256, 128, 64, 128
S5_GROUPS, S5_GROUP, S5_STATE = 64, 16, 64
S5_BLOCKS = 8
FFN_HIDDEN = 2816
N_DEV = 8
ADAM_LR, ADAM_B1, ADAM_B2, ADAM_EPS, ADAM_WD, ADAM_STEP = 0.001, 0.9, 0.999, 1e-08, 0.01, 10

LANES = 128
SUBLANES = 8
VMEM_LIMIT = 48 * 1024 * 1024

ZW = 8192
Z_RET = 0
Z_RG = 2048
Z_U = 3072
Z_GATE = 4096
Z_CQ = 7168
Z_CKV = 7680
Z_KR = 7936
IN_SPLITS = (512, 512, 1024, 1024, 384, 256, 64, 1024, 3072)

SHARDED = ("w_in", "mla_w_uq", "mla_w_ukv", "s5_w_glu", "w_branch", "w_out", "ffn_w_gu", "ffn_w_down")
SHARD_AXIS = {"w_in": 2, "mla_w_uq": 2, "mla_w_ukv": 2, "s5_w_glu": 2, "w_branch": 2, "w_out": 1,
              "ffn_w_gu": 2, "ffn_w_down": 1}
SMALL = ("norm1_g", "ret_decay", "ret_gn_g", "mla_q_norm_g", "mla_kv_norm_g", "s5_a_re", "s5_a_im", "s5_log_dt",
         "s5_b_re", "s5_b_im", "s5_c_re", "s5_c_im", "s5_d", "norm2_g", "final_g")
WEIGHTS = ("norm1_g", "w_in", "ret_decay", "ret_gn_g", "mla_q_norm_g", "mla_w_uq", "mla_kv_norm_g", "mla_w_ukv",
           "s5_a_re", "s5_a_im", "s5_log_dt", "s5_b_re", "s5_b_im", "s5_c_re", "s5_c_im", "s5_d", "s5_w_glu",
           "w_branch", "w_out", "norm2_g", "ffn_w_gu", "ffn_w_down", "final_g")


def _params(sem=None):
    return pltpu.CompilerParams(dimension_semantics=sem, vmem_limit_bytes=VMEM_LIMIT)


def _pick(n, cap):
    if n <= cap:
        return n
    t = cap - cap % LANES
    while t >= LANES:
        if n % t == 0:
            return t
        t -= LANES
    return n


@functools.partial(jax.custom_vjp, nondiff_argnums=(2, 3))
def _bdot(a, b, ca, cb):
    return lax.dot_general(a.astype(BF16), b.astype(BF16), (((ca,), (cb,)), ((), ())), preferred_element_type=F32)


def _bdot_fwd(a, b, ca, cb):
    return _bdot(a, b, ca, cb), (a, b)


def _bdot_bwd(ca, cb, res, g):
    a, b = res
    da = _bdot(g, b, 1, 1 - cb) if ca == 1 else _bdot(b, g, 1 - cb, 1)
    db = _bdot(a, g, 1 - ca, 0) if cb == 0 else _bdot(g, a, 0, 1 - ca)
    return da, db


_bdot.defvjp(_bdot_fwd, _bdot_bwd)


@jax.custom_vjp
def _swap_halves(x):
    return pltpu.roll(x, LANES // 2, 1)


_swap_halves.defvjp(lambda x: (_swap_halves(x), None), lambda _, g: (_swap_halves(g),))


def _rope(x, cosf, sinf):
    return x * cosf + _swap_halves(x) * sinf


def _f_rms(x, g):
    return x * lax.rsqrt(jnp.mean(x * x, axis=-1, keepdims=True) + RMS_EPS) * g


def _rope_t(g, cosf, sinf):
    return g * cosf + _swap_halves(g * sinf)


def _f_gn_gate(yh, rgh, gh):
    mu = jnp.mean(yh, axis=-1, keepdims=True)
    var = jnp.mean(jnp.square(yh - mu), axis=-1, keepdims=True)
    return jax.nn.silu(rgh) * ((yh - mu) * lax.rsqrt(var + GN_EPS) * gh)


MLA_SCALE = (MLA_NOPE + MLA_ROPE) ** -0.5


def _mla_q(qraw, cosf, sinf, transpose):
    parts = []
    for h in range(MLA_HEADS):
        parts.append(qraw[:, 256 * h:256 * h + 128] * MLA_SCALE)
        r = qraw[:, 256 * h + 128:256 * h + 256]
        parts.append(_rope_t(r * MLA_SCALE, cosf, sinf) if transpose else _rope(r, cosf, sinf) * MLA_SCALE)
    return jnp.concatenate(parts, axis=1)


def _f_s5_act(ysum, u, d):
    return jax.nn.gelu(ysum + d * u)


def _f_glu(ga, gb):
    return ga * jax.nn.sigmoid(gb)


def _f_gate(zg, proj):
    return jax.nn.sigmoid(zg) * proj


def _f_swiglu(gp, up):
    return jax.nn.silu(gp) * up


def _f_loss(x, g, target):
    y = _f_rms(x, g)
    err = jnp.square(y - target)
    return 0.5 * jnp.sum(jnp.mean(err, axis=-1))


def _tile_call(name, fn, n_rows, tm, row_ins, consts, row_outs, acc_outs=(), alias=None):
    n_row_in, n_const = len(row_ins), len(consts)
    args = [a for a, _, _ in row_ins] + list(consts)
    in_specs = [pl.BlockSpec((tm, w), lambda i, cb=cb: (i, cb)) for _, w, cb in row_ins]
    in_specs += [pl.BlockSpec(c.shape, lambda i: (0, 0)) for c in consts]
    out_shape, out_specs, aliases = [], [], {}
    if alias is not None:
        arr, w, cb = alias
        in_specs.append(pl.BlockSpec((tm, w), lambda i, cb=cb: (i, cb)))
        aliases[len(args)] = 0
        args.append(arr)
        out_shape.append(jax.ShapeDtypeStruct(arr.shape, arr.dtype))
        out_specs.append(pl.BlockSpec((tm, w), lambda i, cb=cb: (i, cb)))
    for w, dt in row_outs:
        out_shape.append(jax.ShapeDtypeStruct((n_rows, w), dt))
        out_specs.append(pl.BlockSpec((tm, w), lambda i: (i, 0)))
    n_row_out = len(out_shape)
    for r, w in acc_outs:
        out_shape.append(jax.ShapeDtypeStruct((r, w), F32))
        out_specs.append(pl.BlockSpec((r, w), lambda i: (0, 0)))
    n_in = len(args)

    def body(*refs):
        ins, outs = refs[:n_in], refs[n_in:]
        rows = [r[...] for r in ins[:n_row_in]]
        cons = [r[...] for r in ins[n_row_in:n_row_in + n_const]]
        prev = ins[-1][...] if alias is not None else None
        res_rows, res_accs = fn(rows, cons, prev)
        for o, r in zip(outs[:n_row_out], res_rows):
            o[...] = r.astype(o.dtype)
        if acc_outs:
            @pl.when(pl.program_id(0) == 0)
            def _():
                for o in outs[n_row_out:]:
                    o[...] = jnp.zeros(o.shape, F32)
            for o, r in zip(outs[n_row_out:], res_accs):
                o[...] += r

    return pl.pallas_call(
        body, name=name, grid=(n_rows // tm,), in_specs=in_specs, out_specs=out_specs, out_shape=out_shape,
        input_output_aliases=aliases, compiler_params=_params(("arbitrary",)))(*args)


def _mm(name, a, b, *, ta=False, tb=False, add=None, out_dtype=F32):
    (K, M) = a.shape if ta else a.shape[::-1]
    (N, K2) = b.shape if tb else b.shape[::-1]
    assert K == K2, (name, a.shape, b.shape)
    tm, tn, tk = _pick(M, 1536), _pick(N, 1536), _pick(K, 1024)
    gi, gj, nk = M // tm, N // tn, K // tk
    a_bytes, b_bytes = a.size * a.dtype.itemsize, b.size * b.dtype.itemsize
    j_outer = nk == 1 and b_bytes + a_bytes * gj < a_bytes + b_bytes * gi

    def im(f):
        return (lambda g0, g1, k: f(g1, g0, k)) if j_outer else f

    a_spec = pl.BlockSpec((tk, tm), im(lambda i, j, k: (k, i))) if ta else pl.BlockSpec((tm, tk), im(lambda i, j, k: (i, k)))
    b_spec = pl.BlockSpec((tn, tk), im(lambda i, j, k: (j, k))) if tb else pl.BlockSpec((tk, tn), im(lambda i, j, k: (k, j)))
    o_spec = pl.BlockSpec((tm, tn), im(lambda i, j, k: (i, j)))
    dn = (((0 if ta else 1,), (1 if tb else 0,)), ((), ()))
    has_add = add is not None

    def body(*refs):
        if has_add:
            a_ref, b_ref, add_ref, o_ref, acc = refs
        else:
            a_ref, b_ref, o_ref, acc = refs
        k = pl.program_id(2)

        @pl.when(k == 0)
        def _():
            acc[...] = jnp.zeros(acc.shape, F32)

        acc[...] += lax.dot_general(a_ref[...].astype(BF16), b_ref[...].astype(BF16), dn, preferred_element_type=F32)

        @pl.when(k == nk - 1)
        def _():
            r = acc[...]
            if has_add:
                r = r + add_ref[...]
            o_ref[...] = r.astype(out_dtype)

    args, specs = [a, b], [a_spec, b_spec]
    if has_add:
        args.append(add)
        specs.append(o_spec)
    return pl.pallas_call(
        body, name=name, grid=(gj, gi, nk) if j_outer else (gi, gj, nk), in_specs=specs, out_specs=o_spec,
        out_shape=jax.ShapeDtypeStruct((M, N), out_dtype), scratch_shapes=[pltpu.VMEM((tm, tn), F32)],
        compiler_params=_params(("parallel", "parallel", "arbitrary")))(*args)


S5_BW = 2 * S5_STATE * (S5_GROUPS // S5_BLOCKS)


def _scan_tile(buf, ak_ref, ap_ref, carry, *, reverse, x_ref=None, da_ref=None):
    tt, bw = buf.shape
    hw = bw // 2
    ng = tt // SUBLANES
    rowid = lax.broadcasted_iota(jnp.int32, (SUBLANES, hw), 0)
    steps = [(SUBLANES - k if reverse else k, SUBLANES * n) for n, k in enumerate((1, 2, 4))]
    apr, api = ap_ref[:, :hw], ap_ref[:, hw:]
    first = (rowid == SUBLANES - 1) if reverse else (rowid == 0)

    def group(gi, c):
        cr, ci = c
        r0 = pl.multiple_of(((ng - 1 - gi) if reverse else gi) * SUBLANES, SUBLANES)
        xr, xi = buf[pl.ds(r0, SUBLANES), :hw], buf[pl.ds(r0, SUBLANES), hw:]
        for sh, a0 in steps:
            kr, ki = ak_ref[a0:a0 + SUBLANES, :hw], ak_ref[a0:a0 + SUBLANES, hw:]
            sr, si = pltpu.roll(xr, sh, 0), pltpu.roll(xi, sh, 0)
            xr, xi = xr + kr * sr - ki * si, xi + kr * si + ki * sr
        xr, xi = xr + apr * cr - api * ci, xi + apr * ci + api * cr
        buf[pl.ds(r0, SUBLANES), :hw] = xr
        buf[pl.ds(r0, SUBLANES), hw:] = xi
        if x_ref is not None:
            sh1 = SUBLANES - 1 if reverse else 1
            pr = jnp.where(first, cr, pltpu.roll(xr, sh1, 0))
            pi = jnp.where(first, ci, pltpu.roll(xi, sh1, 0))
            sr, si = x_ref[pl.ds(r0, SUBLANES), :hw], x_ref[pl.ds(r0, SUBLANES), hw:]
            da_ref[:, :hw] += pr * sr + pi * si
            da_ref[:, hw:] += pi * sr - pr * si
        last = 0 if reverse else SUBLANES - 1
        return (jnp.broadcast_to(xr[last:last + 1], (SUBLANES, hw)), jnp.broadcast_to(xi[last:last + 1], (SUBLANES, hw)))

    cr, ci = lax.fori_loop(0, ng, group, (carry[:, :hw], carry[:, hw:]))
    carry[:, :hw] = cr
    carry[:, hw:] = ci


def _s5_specs(n_rows, reverse):
    tt = _pick(n_rows, 256)
    nt = n_rows // tt

    def rows(width, off):
        return pl.BlockSpec((tt, width), lambda j, t: ((nt - 1 - t) if reverse else t, off + j))

    def per_block(r, c):
        return pl.BlockSpec((None, r, c), lambda j, t: (j, 0, 0))

    def par(r):
        return pl.BlockSpec((r, S5_BW), lambda j, t: (0, j))

    return tt, nt, rows, per_block, par


def _s5_call(name, core, args, in_specs, out_specs, out_shape, scratch, nt, side):
    n_out = len(out_shape)
    if side is None:
        body, sem = core, ("parallel", "arbitrary")
    else:
        make, shapes, sems = _exchange_parts(side)
        n, n_in, n_sc = len(shapes), len(args), len(scratch)
        hbm = pl.BlockSpec(memory_space=pl.ANY)

        def body(*refs):
            ins, xs = refs[:n_in], refs[n_in:n_in + n]
            outs, lands = refs[n_in + n:n_in + n + n_out], refs[n_in + n + n_out:n_in + 2 * n + n_out]
            sc = refs[n_in + 2 * n + n_out:n_in + 2 * n + n_out + n_sc]
            start, forward, finish = make(xs, lands, *refs[-3:])
            j, t = pl.program_id(0), pl.program_id(1)
            pl.when((j == 0) & (t == 0))(start)
            pl.when((j == S5_BLOCKS // 2) & (t == 0))(forward)
            core(*ins, *outs, *sc)
            pl.when((j == S5_BLOCKS - 1) & (t == nt - 1))(finish)

        args, in_specs = list(args) + list(side[1]), list(in_specs) + [hbm] * n
        out_specs, out_shape = list(out_specs) + [hbm] * n, list(out_shape) + shapes
        scratch, sem = list(scratch) + sems, ("arbitrary", "arbitrary")
    res = pl.pallas_call(body, name=name, grid=(S5_BLOCKS, nt), in_specs=in_specs, out_specs=out_specs, out_shape=out_shape,
                         scratch_shapes=scratch, compiler_params=_params(sem))(*args)
    return list(res[:n_out]) + [None if side is None else list(res[n_out:])]


def _s5_fwd(name, z, m, *, reverse, side=None):
    n_rows = z.shape[0]
    tt, nt, rows, per_block, par = _s5_specs(n_rows, reverse)

    def body(u_ref, b_ref, c_ref, ak_ref, ap_ref, x_ref, y_ref, carry):
        @pl.when(pl.program_id(1) == 0)
        def _():
            carry[...] = jnp.zeros(carry.shape, F32)

        x_ref[...] = jnp.dot(u_ref[...].astype(BF16), b_ref[...].astype(BF16), preferred_element_type=F32)
        _scan_tile(x_ref, ak_ref, ap_ref, carry, reverse=reverse)
        y_ref[...] = jnp.dot(x_ref[...].astype(BF16), c_ref[...].astype(BF16), preferred_element_type=F32)

    return _s5_call(
        name, body, [z, m["b_map"], m["c_map"], m["ak"], m["apow"]],
        [rows(LANES, Z_U // LANES), per_block(LANES, S5_BW), per_block(S5_BW, LANES), par(3 * SUBLANES), par(SUBLANES)],
        [rows(S5_BW, 0), rows(LANES, 0)],
        [jax.ShapeDtypeStruct((n_rows, S5_BLOCKS * S5_BW), F32), jax.ShapeDtypeStruct((n_rows, S5_BLOCKS * LANES), F32)],
        [pltpu.VMEM((SUBLANES, S5_BW), F32)], nt, side)


def _s5_bwd(name, z, dy, xs, m, *, reverse, side=None):
    n_rows = z.shape[0]
    tt, nt, rows, per_block, par = _s5_specs(n_rows, reverse)

    def body(u_ref, dy_ref, x_ref, bt_ref, ct_ref, ak_ref, ap_ref, du_ref, db_ref, dc_ref, da_ref, lam, carry):
        t = pl.program_id(1)

        @pl.when(t == 0)
        def _():
            carry[...] = jnp.zeros(carry.shape, F32)
            db_ref[...] = jnp.zeros(db_ref.shape, F32)
            dc_ref[...] = jnp.zeros(dc_ref.shape, F32)
            da_ref[...] = jnp.zeros(da_ref.shape, F32)

        dy_b = dy_ref[...].astype(BF16)
        lam[...] = jnp.dot(dy_b, ct_ref[...].astype(BF16), preferred_element_type=F32)
        _scan_tile(lam, ak_ref, ap_ref, carry, reverse=reverse, x_ref=x_ref, da_ref=da_ref)
        lam_b = lam[...].astype(BF16)
        du_ref[...] = jnp.dot(lam_b, bt_ref[...].astype(BF16), preferred_element_type=F32)
        db_ref[...] += lax.dot_general(u_ref[...].astype(BF16), lam_b, _TN, preferred_element_type=F32)
        dc_ref[...] += lax.dot_general(x_ref[...].astype(BF16), dy_b, _TN, preferred_element_type=F32)

        @pl.when(t == nt - 1)
        def _():
            da_ref[...] = jnp.broadcast_to(jnp.sum(da_ref[...], axis=0, keepdims=True), da_ref.shape)

    return _s5_call(
        name, body, [z, dy, xs, m["b_map_t"], m["c_map_t"], m["ak_adj"], m["apow_adj"]],
        [rows(LANES, Z_U // LANES), rows(LANES, 0), rows(S5_BW, 0), per_block(S5_BW, LANES), per_block(LANES, S5_BW),
         par(3 * SUBLANES), par(SUBLANES)],
        [rows(LANES, 0), per_block(LANES, S5_BW), per_block(S5_BW, LANES), par(SUBLANES)],
        [jax.ShapeDtypeStruct((n_rows, S5_BLOCKS * LANES), F32), jax.ShapeDtypeStruct((S5_BLOCKS, LANES, S5_BW), F32),
         jax.ShapeDtypeStruct((S5_BLOCKS, S5_BW, LANES), F32), jax.ShapeDtypeStruct((SUBLANES, S5_BLOCKS * S5_BW), F32)],
        [pltpu.VMEM((tt, S5_BW), F32), pltpu.VMEM((SUBLANES, S5_BW), F32)], nt, side)


def _ret_chunk(zq, zk, v, cosf, sinf, state, rd, reverse):
    c = RET_CHUNK
    lg = jax.nn.log_sigmoid(rd)
    lg1 = jnp.max(lg, axis=1, keepdims=True)
    q = _rope(zq, cosf, sinf) * (RET_QK ** -0.5)
    k = _rope(zk, cosf, sinf)
    pi = lax.broadcasted_iota(jnp.int32, (c, c), 0).astype(F32)
    pj = lax.broadcasted_iota(jnp.int32, (c, c), 1).astype(F32)
    pcol = lax.broadcasted_iota(jnp.int32, (c, 1), 0).astype(F32)
    if reverse:
        diff, mask, pos = pj - pi, pj > pi, (c - 1) - pcol
    else:
        diff, mask, pos = pi - pj, pi >= pj, pcol
    decay_in = jnp.where(mask, jnp.exp(jnp.where(mask, diff, 0.0) * lg), 0.0)
    scores = _bdot(q, k, 1, 1) * decay_in
    inner = _bdot(scores, v, 1, 0)
    k_w = jnp.exp((c - 1 - pos) * lg1)
    kv = _bdot(k * k_w, v, 0, 0)
    q_w = jnp.exp((pos + 1) * lg1)
    cross = _bdot(q, state, 1, 0) * q_w
    new_state = jnp.exp(c * lg1) * state + kv
    return inner + cross, new_state


RET_ZW = RET_HEADS * (2 * RET_QK + RET_V)


def _ret_specs(n_chunks, reverse_order):
    cmap = (lambda n: n_chunks - 1 - n) if reverse_order else (lambda n: n)
    z_spec = pl.BlockSpec((RET_CHUNK, RET_ZW), lambda n: (cmap(n), 0))
    t_spec = pl.BlockSpec((RET_CHUNK, LANES), lambda n: (cmap(n), 0))
    rd_spec = pl.BlockSpec((RET_HEADS, 1, LANES), lambda n: (0, 0, 0))
    o_spec = pl.BlockSpec((RET_CHUNK, RET_HEADS * RET_V), lambda n: (cmap(n), 0))
    st_spec = pl.BlockSpec((RET_HEADS, None, RET_QK, RET_V), lambda n: (0, cmap(n), 0, 0))
    return z_spec, t_spec, rd_spec, o_spec, st_spec


def _ret_head(zt, h):
    b = h * (2 * RET_QK + RET_V)
    return zt[:, b:b + RET_QK], zt[:, b + RET_QK:b + 2 * RET_QK], zt[:, b + 2 * RET_QK:b + 2 * RET_QK + RET_V]


def _ret_fwd(name, z, cosf, sinf, rd, *, reverse):
    n_rows = z.shape[0]
    n_chunks = n_rows // RET_CHUNK
    z_spec, t_spec, rd_spec, o_spec, st_spec = _ret_specs(n_chunks, reverse)

    def body(z_ref, cos_ref, sin_ref, rd_ref, o_ref, st_ref, state):
        @pl.when(pl.program_id(0) == 0)
        def _():
            state[...] = jnp.zeros(state.shape, F32)

        zt = z_ref[...]
        cosv, sinv = cos_ref[...], sin_ref[...]
        for h in range(RET_HEADS):
            st = state[h]
            st_ref[h] = st
            out, new = _ret_chunk(*_ret_head(zt, h), cosv, sinv, st, rd_ref[h], reverse)
            o_ref[:, RET_V * h:RET_V * (h + 1)] = out
            state[h] = new

    return pl.pallas_call(
        body, name=name, grid=(n_chunks,), in_specs=[z_spec, t_spec, t_spec, rd_spec],
        out_specs=[o_spec, st_spec],
        out_shape=[jax.ShapeDtypeStruct((n_rows, RET_HEADS * RET_V), F32),
                   jax.ShapeDtypeStruct((RET_HEADS, n_chunks, RET_QK, RET_V), F32)],
        scratch_shapes=[pltpu.VMEM((RET_HEADS, RET_QK, RET_V), F32)], compiler_params=_params(("arbitrary",)))(z, cosf, sinf, rd)


def _ret_bwd(name, z, cosf, sinf, rd, states, dout, dz, *, reverse):
    n_rows = z.shape[0]
    n_chunks = n_rows // RET_CHUNK
    z_spec, t_spec, rd_spec, o_spec, st_spec = _ret_specs(n_chunks, not reverse)

    def body(z_ref, cos_ref, sin_ref, rd_ref, st_ref, do_ref, dzin_ref, dz_ref, drd_ref, dstate):
        @pl.when(pl.program_id(0) == 0)
        def _():
            dstate[...] = jnp.zeros(dstate.shape, F32)
            drd_ref[...] = jnp.zeros(drd_ref.shape, F32)

        zt = z_ref[...]
        cosv, sinv = cos_ref[...], sin_ref[...]
        parts = []
        for h in range(RET_HEADS):
            _, vjp = jax.vjp(lambda a, b, c, s, r: _ret_chunk(a, b, c, cosv, sinv, s, r, reverse),
                             *_ret_head(zt, h), st_ref[h], rd_ref[h])
            dq, dk, dv, dst, drd = vjp((do_ref[:, RET_V * h:RET_V * (h + 1)], dstate[h]))
            parts += [dq, dk, dv]
            dstate[h] = dst
            drd_ref[h] += jnp.sum(drd, axis=1, keepdims=True)
        dz_ref[...] = (dzin_ref[...].astype(F32) + jnp.concatenate(parts, axis=1)).astype(dz_ref.dtype)

    return pl.pallas_call(
        body, name=name, grid=(n_chunks,),
        in_specs=[z_spec, t_spec, t_spec, rd_spec, st_spec, o_spec, z_spec],
        out_specs=[z_spec, rd_spec],
        out_shape=[jax.ShapeDtypeStruct(dz.shape, dz.dtype), jax.ShapeDtypeStruct((RET_HEADS, 1, LANES), F32)],
        input_output_aliases={6: 0}, scratch_shapes=[pltpu.VMEM((RET_HEADS, RET_QK, RET_V), F32)],
        compiler_params=_params(("arbitrary",)))(z, cosf, sinf, rd, states, dout, dz)


_NT = (((1,), (1,)), ((), ()))
_TN = (((0,), (0,)), ((), ()))


def _attn_tiles(n_rows, tq_cap):
    tq, tk = _pick(n_rows, tq_cap), _pick(n_rows, 2048)
    return tq, max(tq // 2, LANES), tk, min(tk, 1024)


def _attn_fwd(name, q, kv, kr):
    n_rows = q.shape[0]
    tq, hq, tk, sub = _attn_tiles(n_rows, 512)
    nk = n_rows // tk

    def body(q_ref, kn_ref, v_ref, kr_ref, o_ref, lse_ref, m_sc, acc):
        j = pl.program_id(2)

        @pl.when(j == 0)
        def _():
            m_sc[...] = jnp.full(m_sc.shape, -jnp.inf, F32)
            acc[...] = jnp.zeros(acc.shape, F32)

        for c in range(tk // sub):
            rows = slice(c * sub, (c + 1) * sub)
            k = jnp.concatenate([kn_ref[rows, :], kr_ref[rows, :]], axis=1)
            v1 = jnp.concatenate([v_ref[rows, :], jnp.ones((sub, LANES), BF16)], axis=1)
            for part in range(tq // hq):
                qr = slice(part * hq, (part + 1) * hq)
                s = lax.dot_general(q_ref[qr, :], k, _NT, preferred_element_type=F32)
                m_prev = m_sc[qr, :]
                m_new = jnp.maximum(m_prev, jnp.max(s, axis=1, keepdims=True))
                p = jnp.exp(s - m_new)
                acc[qr, :] = jnp.exp(m_prev - m_new) * acc[qr, :] + jnp.dot(p.astype(BF16), v1, preferred_element_type=F32)
                m_sc[qr, :] = m_new

        @pl.when(j == nk - 1)
        def _():
            l = acc[:, LANES:]
            o_ref[...] = acc[:, :LANES] / l
            lse_ref[...] = m_sc[...] + jnp.log(l)

    return pl.pallas_call(
        body, name=name, grid=(MLA_HEADS, n_rows // tq, nk),
        in_specs=[pl.BlockSpec((tq, 256), lambda h, i, j: (i, h)),
                  pl.BlockSpec((tk, 128), lambda h, i, j: (j, 2 * h)),
                  pl.BlockSpec((tk, 128), lambda h, i, j: (j, 2 * h + 1)),
                  pl.BlockSpec((tk, 128), lambda h, i, j: (j, 0))],
        out_specs=[pl.BlockSpec((tq, 128), lambda h, i, j: (i, h)),
                   pl.BlockSpec((None, tq, 128), lambda h, i, j: (h, i, 0))],
        out_shape=[jax.ShapeDtypeStruct((n_rows, MLA_HEADS * MLA_V), F32),
                   jax.ShapeDtypeStruct((MLA_HEADS, n_rows, LANES), F32)],
        scratch_shapes=[pltpu.VMEM((tq, 1), F32), pltpu.VMEM((tq, 2 * LANES), F32)],
        compiler_params=_params(("parallel", "parallel", "arbitrary")))(q, kv, kv, kr)


def _attn_bwd(name, q, kv, kr, o, lse, do):
    n_rows = q.shape[0]
    tq, hq, tk, sub = _attn_tiles(n_rows, 1024)
    nq = n_rows // tq

    def body(q_ref, kn_ref, v_ref, kr_ref, o_ref, lse_ref, do_ref, dq_ref, dkv_ref, dkr_ref, dk_acc, dv_acc):
        j, i = pl.program_id(1), pl.program_id(2)

        @pl.when(i == 0)
        def _():
            dk_acc[...] = jnp.zeros(dk_acc.shape, F32)
            dv_acc[...] = jnp.zeros(dv_acc.shape, F32)

        @pl.when((i == 0) & (j == 0))
        def _():
            dq_ref[...] = jnp.zeros(dq_ref.shape, F32)

        for part in range(tq // hq):
            qr = slice(part * hq, (part + 1) * hq)
            qv = q_ref[qr, :]
            do = do_ref[qr, :]
            do_b = do.astype(BF16)
            delta = jnp.sum(do * o_ref[qr, :], axis=1, keepdims=True)
            lse_col = lse_ref[qr, :][:, :1]
            dq = None
            for c in range(tk // sub):
                rows = slice(c * sub, (c + 1) * sub)
                k = jnp.concatenate([kn_ref[rows, :], kr_ref[rows, :]], axis=1)
                s = lax.dot_general(qv, k, _NT, preferred_element_type=F32)
                p = jnp.exp(s - lse_col)
                dp = lax.dot_general(do_b, v_ref[rows, :], _NT, preferred_element_type=F32)
                ds = (p * (dp - delta)).astype(BF16)
                dv_acc[rows, :] += lax.dot_general(p.astype(BF16), do_b, _TN, preferred_element_type=F32)
                dk_acc[rows, :] += lax.dot_general(ds, qv, _TN, preferred_element_type=F32)
                t = jnp.dot(ds, k, preferred_element_type=F32)
                dq = t if dq is None else dq + t
            r0 = pl.multiple_of(i * tq + part * hq, hq)
            dq_ref[pl.ds(r0, hq), :] += dq

        @pl.when(i == nq - 1)
        def _():
            dkv_ref[...] = jnp.concatenate([dk_acc[:, :128], dv_acc[...]], axis=1).astype(dkv_ref.dtype)
            dkr_ref[...] = dk_acc[:, 128:]

    return pl.pallas_call(
        body, name=name, grid=(MLA_HEADS, n_rows // tk, nq),
        in_specs=[pl.BlockSpec((tq, 256), lambda h, j, i: (i, h)),
                  pl.BlockSpec((tk, 128), lambda h, j, i: (j, 2 * h)),
                  pl.BlockSpec((tk, 128), lambda h, j, i: (j, 2 * h + 1)),
                  pl.BlockSpec((tk, 128), lambda h, j, i: (j, 0)),
                  pl.BlockSpec((tq, 128), lambda h, j, i: (i, h)),
                  pl.BlockSpec((None, tq, 128), lambda h, j, i: (h, i, 0)),
                  pl.BlockSpec((tq, 128), lambda h, j, i: (i, h))],
        out_specs=[pl.BlockSpec((n_rows, 256), lambda h, j, i: (0, h)),
                   pl.BlockSpec((tk, 256), lambda h, j, i: (j, h)),
                   pl.BlockSpec((tk, 128), lambda h, j, i: (j, h))],
        out_shape=[jax.ShapeDtypeStruct((n_rows, MLA_HEADS * 256), F32), jax.ShapeDtypeStruct((n_rows, MLA_HEADS * 256), BF16),
                   jax.ShapeDtypeStruct((n_rows, MLA_HEADS * 128), F32)],
        scratch_shapes=[pltpu.VMEM((tk, 256), F32), pltpu.VMEM((tk, 128), F32)],
        compiler_params=_params(("parallel", "arbitrary", "arbitrary")))(q, kv, kv, kr, o, lse, do)


_MESH = pl.DeviceIdType.MESH


def _gather_stages(x_refs, out_refs, send_sems, recv_sems, local_sems):
    n = len(x_refs)
    mx, my, mc = lax.axis_index("x"), lax.axis_index("y"), lax.axis_index("c")
    me, sibling = (mx, my, mc), (mx, my, 1 - mc)
    chips = [(1 - mx, my), (mx, 1 - my), (1 - mx, 1 - my)]

    def copy(a, k, block, to, src=None):
        dst = out_refs[a].at[4 * block[0] + 2 * block[1] + block[2]]
        return pltpu.make_async_remote_copy(
            src_ref=dst if src is None else src, dst_ref=dst, send_sem=send_sems.at[7 * a + k],
            recv_sem=recv_sems.at[7 * a + k], device_id=to, device_id_type=_MESH)

    def mine():
        return [pltpu.make_async_copy(x_refs[a], out_refs[a].at[4 * mx + 2 * my + mc], local_sems.at[a]) for a in range(n)]

    def first():
        out = []
        for a in range(n):
            out.append(copy(a, 0, me, sibling, src=x_refs[a]))
            out += [copy(a, 1 + j, me, (*chip, mc), src=x_refs[a]) for j, chip in enumerate(chips)]
        return out

    def passed():
        return [copy(a, 4 + j, (*chip, mc), sibling) for j, chip in enumerate(chips) for a in range(n)]

    def start():
        for cp in mine() + first():
            cp.start()

    def forward():
        for j, chip in enumerate(chips):
            for a in range(n):
                copy(a, 1 + j, (*chip, mc), me).wait_recv()
        for cp in passed():
            cp.start()

    def finish():
        for a in range(n):
            copy(a, 0, sibling, me).wait_recv()
            for j, chip in enumerate(chips):
                copy(a, 4 + j, (*chip, 1 - mc), me).wait_recv()
        for cp in first() + passed():
            cp.wait_send()
        for cp in mine():
            cp.wait()

    return start, forward, finish


def _exchange_stages(g_refs, land_refs, send_sems, recv_sems, local_sems):
    n = len(g_refs)
    mx, my, mc = lax.axis_index("x"), lax.axis_index("y"), lax.axis_index("c")
    me = 4 * mx + 2 * my + mc

    def mine():
        return [pltpu.make_async_copy(g_refs[a].at[me], land_refs[a].at[me], local_sems.at[a]) for a in range(n)]

    def copies():
        out = []
        for k in range(1, N_DEV):
            px = 1 - mx if k & 4 else mx
            py = 1 - my if k & 2 else my
            pc = 1 - mc if k & 1 else mc
            peer = 4 * px + 2 * py + pc
            for a in range(n):
                sems = dict(send_sem=send_sems.at[7 * a + k - 1], recv_sem=recv_sems.at[7 * a + k - 1],
                            device_id=(px, py, pc), device_id_type=_MESH)
                out.append((pltpu.make_async_remote_copy(src_ref=g_refs[a].at[peer], dst_ref=land_refs[a].at[me], **sems),
                            pltpu.make_async_remote_copy(src_ref=g_refs[a].at[peer], dst_ref=land_refs[a].at[peer], **sems)))
        return out

    def start():
        for cp in mine():
            cp.start()
        for send, _ in copies():
            send.start()

    def finish():
        both = copies()
        for _, recv in both:
            recv.wait_recv()
        for send, _ in both:
            send.wait_send()
        for cp in mine():
            cp.wait()

    return start, lambda: None, finish


def _exchange_parts(side):
    kind, arrays = side
    n = len(arrays)
    if kind == "gather":
        make, shapes = _gather_stages, [jax.ShapeDtypeStruct((N_DEV,) + a.shape, a.dtype) for a in arrays]
    else:
        make, shapes = _exchange_stages, [jax.ShapeDtypeStruct(a.shape, a.dtype) for a in arrays]
    sems = [pltpu.SemaphoreType.DMA((7 * n,)), pltpu.SemaphoreType.DMA((7 * n,)), pltpu.SemaphoreType.DMA((n,))]
    return make, shapes, sems


def _exchange(name, side):
    make, shapes, sems = _exchange_parts(side)
    n = len(shapes)

    def body(*refs):
        start, forward, finish = make(refs[:n], refs[n:2 * n], *refs[2 * n:])
        start()
        forward()
        finish()

    hbm = pl.BlockSpec(memory_space=pl.ANY)
    return list(pl.pallas_call(body, name=name, out_shape=shapes, in_specs=[hbm] * n, out_specs=[hbm] * n,
                               scratch_shapes=sems)(*side[1]))


def _all_gather(name, xs):
    return _exchange(name, ("gather", xs))


def _all_to_all(name, gs):
    return _exchange(name, ("exchange", gs))


def _adamw(name, parts, w, m, v, tr):
    rows, cols = w.shape

    def body(p_ref, w_ref, m_ref, v_ref, g_ref, d_ref, nm_ref, nv_ref):
        g = p_ref[0].astype(F32)
        for d in range(1, N_DEV):
            g = g + p_ref[d].astype(F32)
        nm = ADAM_B1 * m_ref[...] + (1.0 - ADAM_B1) * g
        nv = ADAM_B2 * v_ref[...] + (1.0 - ADAM_B2) * jnp.square(g)
        m_hat = nm / (1.0 - ADAM_B1 ** ADAM_STEP)
        v_hat = nv / (1.0 - ADAM_B2 ** ADAM_STEP)
        g_ref[...] = g
        d_ref[...] = -ADAM_LR * (m_hat / (jnp.sqrt(v_hat) + ADAM_EPS) + ADAM_WD * w_ref[...])
        nm_ref[...] = nm
        nv_ref[...] = nv

    spec = pl.BlockSpec((tr, cols), lambda i: (i, 0))
    return pl.pallas_call(
        body, name=name, grid=(rows // tr,),
        in_specs=[pl.BlockSpec((N_DEV, tr, cols), lambda i: (0, i, 0)), spec, spec, spec],
        out_specs=[spec] * 4, out_shape=[jax.ShapeDtypeStruct((rows, cols), F32)] * 4,
        compiler_params=_params(("parallel",)))(parts, w, m, v)


def _in_pieces():
    p = []
    for h in range(RET_HEADS):
        p += [(128 * h, 128 * h + 128), (512 + 128 * h, 512 + 128 * h + 128), (1024 + 256 * h, 1024 + 256 * h + 256)]
    p += [(2048, 3072), (3776, 4800), (4800, 7872), (3072, 3456), 128, (3456, 3712),
          (3712, 3744), 32, (3744, 3776), 32, 128]
    return p


def _uq_pieces():
    p = []
    for h in range(MLA_HEADS):
        b = 192 * h
        p += [(b, b + 128), (b + 128, b + 160), 32, (b + 160, b + 192), 32]
    return p


def _perm(w, pieces):
    cols = [jnp.zeros(w.shape[:-1] + (p,), w.dtype) if isinstance(p, int) else w[..., p[0]:p[1]] for p in pieces]
    return jnp.concatenate(cols, axis=-1)


def _unperm(dw, pieces):
    found, off = [], 0
    for p in pieces:
        if isinstance(p, int):
            off += p
        else:
            found.append((p[0], dw[..., off:off + p[1] - p[0]]))
            off += p[1] - p[0]
    return jnp.concatenate([t for _, t in sorted(found, key=lambda s: s[0])], axis=-1)


def _unshard(blocks, axis):
    return jnp.concatenate([blocks[p] for p in range(N_DEV)], axis=axis)


def _shard_split(full, axis):
    return jnp.stack(jnp.split(full, N_DEV, axis=axis), axis=0)


def _row_tile(rows, cap=256, unit=16):
    return max(t for t in range(unit, cap + 1, unit) if rows % t == 0)


def _rope_tables(seq):
    pos = jnp.arange(seq, dtype=F32)[:, None]

    def table(dim):
        inv = 1.0 / (ROPE_THETA ** (jnp.arange(0, dim, 2, dtype=F32) / dim))
        ang = pos * inv[None, :]
        return jnp.cos(ang), jnp.sin(ang)

    cr, sr = table(RET_QK)
    cm, sm = table(MLA_ROPE)
    z = jnp.zeros_like(cm)
    return (jnp.concatenate([cr, cr], 1), jnp.concatenate([-sr, sr], 1),
            jnp.concatenate([cm, z, cm, z], 1), jnp.concatenate([-sm, z, sm, z], 1))


def _s5_maps(a_re, a_im, log_dt, b_re, b_im, c_re, c_im):
    dt = jnp.exp(log_dt)[:, None]
    ar = jnp.minimum(a_re, -1e-4)
    mag = jnp.exp(dt * ar)
    abar_re = mag * jnp.cos(dt * a_im)
    abar_im = mag * jnp.sin(dt * a_im)
    den = ar * ar + a_im * a_im
    nr = abar_re - 1.0
    ni = abar_im
    coef_re = (nr * ar + ni * a_im) / den
    coef_im = (ni * ar - nr * a_im) / den
    bb_re = coef_re[..., None] * b_re - coef_im[..., None] * b_im
    bb_im = coef_re[..., None] * b_im + coef_im[..., None] * b_re
    eye = jnp.eye(S5_BLOCKS, dtype=F32)

    def in_blocks(bb):
        t = bb.transpose(0, 2, 1).reshape(S5_BLOCKS, 8, S5_GROUP, S5_STATE)
        return jnp.einsum('jgcp,gh->jgchp', t, eye).reshape(S5_BLOCKS, 128, 512)

    def out_blocks(cc):
        t = cc.transpose(0, 2, 1).reshape(S5_BLOCKS, 8, S5_STATE, S5_GROUP)
        return jnp.einsum('jgpc,gh->jgphc', t, eye).reshape(S5_BLOCKS, 512, 128)

    arow = jnp.concatenate([abar_re.reshape(S5_BLOCKS, 512), abar_im.reshape(S5_BLOCKS, 512)], axis=1).reshape(1, -1)
    b_map = jnp.concatenate([in_blocks(bb_re), in_blocks(bb_im)], axis=2)
    c_map = jnp.concatenate([out_blocks(c_re), -out_blocks(c_im)], axis=1)
    return arow, b_map, c_map


def _power_tables(arow, conj, reverse):
    a = arow.reshape(S5_BLOCKS, 2, 512)
    ar, ai = a[:, 0], (-a[:, 1] if conj else a[:, 1])
    pw = [(ar, ai)]
    for _ in range(SUBLANES - 1):
        pr, pi = pw[-1]
        pw.append((pr * ar - pi * ai, pr * ai + pi * ar))

    def rows(sel):
        return jnp.stack([jnp.stack(list(pw[i]), axis=1) for i in sel], axis=0).reshape(len(sel), -1)

    rowid = jnp.arange(SUBLANES)[:, None]
    ak = jnp.concatenate([jnp.where((rowid < SUBLANES - k) if reverse else (rowid >= k), rows([k - 1]), 0.0)
                          for k in (1, 2, 4)], axis=0)
    order = list(range(SUBLANES))
    apow = rows(order[::-1] if reverse else order)
    return ak, apow


def _rows(arr):
    return (arr, arr.shape[1], 0)


def _vjp_rows(f, n_prim):
    def fn(r, c, _):
        _, vjp = jax.vjp(f, *r[:n_prim])
        return list(vjp(r[n_prim])), []
    return fn


def _norm_bwd(r, c, _):
    _, vjp = jax.vjp(_f_rms, r[0], c[0])
    dx, dg = vjp(r[1])
    return [dx + r[2]], [dg]


def _layer_fwd(l, x, W_first, W_rest, P, T, sides):
    W = dict(W_first)
    n = x.shape[0]
    tm, tmw = _pick(n, 256), _pick(n, 128)
    cos_r, sin_r, cos_m, sin_m = T

    def nm(s):
        return f"l{l}_{s}"

    def one(name, f, rows, consts, width, dtype=BF16, tile=tm):
        return _tile_call(nm(name), lambda r, c, _: ([f(r, c)], []), n, tile, rows, consts, [(width, dtype)])[0]

    h = one("norm1", lambda r, c: _f_rms(r[0], c[0]), [_rows(x)], [P["norm1_g"]], D_MODEL)
    z = _mm(nm("in_proj"), h, W["in"])
    of, stf = _ret_fwd(nm("ret_f"), z, cos_r, sin_r, P["rd"][0], reverse=False)
    ob, stb = _ret_fwd(nm("ret_b"), z, cos_r, sin_r, P["rd"][1], reverse=True)

    def gn(r, c, _):
        yraw = r[0] + r[1]
        ys = [_f_gn_gate(yraw[:, RET_V * i:RET_V * (i + 1)], r[2][:, RET_V * i:RET_V * (i + 1)],
                         c[0][:, RET_V * i:RET_V * (i + 1)]) for i in range(RET_HEADS)]
        return [yraw, jnp.concatenate(ys, axis=1)], []

    yraw, yret = _tile_call(nm("ret_gn"), gn, n, tm, [_rows(of), _rows(ob), (z, 1024, Z_RG // 1024)], [P["ret_gn_g"]],
                            [(1024, F32), (1024, BF16)])

    cqn = one("q_norm", lambda r, c: _f_rms(r[0][:, :MLA_Q_LORA], c[0]), [(z, 512, Z_CQ // 512)], [P["mla_q_norm_g"]], MLA_Q_LORA)
    ckvn = one("kv_norm", lambda r, c: _f_rms(r[0], c[0]), [(z, 256, Z_CKV // 256)], [P["mla_kv_norm_g"]], MLA_KV_LORA)
    qraw = _mm(nm("q_up"), cqn, W["uq"])
    kv = _mm(nm("kv_up"), ckvn, W["ukv"], out_dtype=BF16)
    q = one("q_rope", lambda r, c: _mla_q(r[0], r[1], r[2], False), [_rows(qraw), _rows(cos_m), _rows(sin_m)], [], 2048)
    kr = one("k_rope", lambda r, c: _rope(r[0], r[1], r[2]), [(z, 128, Z_KR // 128), _rows(cos_m), _rows(sin_m)], [], 128)
    o, lse = _attn_fwd(nm("attn"), q, kv, kr)

    xs, y_dir, side_res = [], [], {}
    for d in range(2):
        x_d, y_d, side_res[d] = _s5_fwd(nm(f"s5_scan{d}"), z, P["s5"][d], reverse=(d == 1), side=sides.get(d))
        xs.append(x_d)
        y_dir.append(y_d)
    W.update(W_rest(side_res))

    def s5_act(r, c, _):
        ysum = r[0] + r[1]
        return [ysum, _f_s5_act(ysum, r[2], c[0])], []

    ysum, gact = _tile_call(nm("s5_act"), s5_act, n, tm, [_rows(y_dir[0]), _rows(y_dir[1]), (z, 1024, Z_U // 1024)], [P["s5_d"]],
                            [(1024, F32), (1024, BF16)])
    ga = _mm(nm("glu_a"), gact, W["glu_a"])
    gb = _mm(nm("glu_b"), gact, W["glu_b"])
    ys5 = one("glu", lambda r, c: _f_glu(r[0], r[1]), [_rows(ga), _rows(gb)], [], 1024)

    ys = (yret, o, ys5)
    ps = [_mm(nm(f"branch{i}"), ys[i], W["branch"][i]) for i in range(3)]
    mix = one("mix", lambda r, c: _f_gate(r[0], r[3]) + _f_gate(r[1], r[4]) + _f_gate(r[2], r[5]),
              [(z, 1024, Z_GATE // 1024 + i) for i in range(3)] + [_rows(p) for p in ps], [], 1024)
    x1 = _mm(nm("out_proj"), mix, W["out"], add=x)
    h2 = one("norm2", lambda r, c: _f_rms(r[0], c[0]), [_rows(x1)], [P["norm2_g"]], D_MODEL)
    gp = _mm(nm("ffn_g"), h2, W["ffn_g"])
    up = _mm(nm("ffn_u"), h2, W["ffn_u"])
    act = one("swiglu", lambda r, c: _f_swiglu(r[0], r[1]), [_rows(gp), _rows(up)], [], FFN_HIDDEN, tile=tmw)
    x2 = _mm(nm("ffn_down"), act, W["ffn_down"], add=x1)
    saved = dict(x=x, h=h, z=z, stf=stf, stb=stb, yraw=yraw, ys=ys, cqn=cqn, ckvn=ckvn, qraw=qraw, q=q, kv=kv, kr=kr,
                 lse=lse, xs=xs, ysum=ysum, gact=gact, ga=ga, gb=gb, ps=ps, mix=mix, x1=x1, h2=h2, gp=gp, up=up, act=act, W=W)
    return x2, saved, side_res


def _layer_bwd(l, dx2, sv, P, T, side_of):
    n = dx2.shape[0]
    tm, tmw = _pick(n, 256), _pick(n, 128)
    cos_r, sin_r, cos_m, sin_m = T
    z, W = sv["z"], sv["W"]
    g = {}

    def nm(s):
        return f"l{l}_{s}"

    dact = _mm(nm("d_act"), dx2, W["ffn_down"], tb=True)
    g["ffn_down"] = _mm(nm("dw_ffn_down"), sv["act"], dx2, ta=True)
    dgp, dup = _tile_call(nm("d_swiglu"), _vjp_rows(_f_swiglu, 2), n, tmw, [_rows(sv["gp"]), _rows(sv["up"]), _rows(dact)], [],
                          [(FFN_HIDDEN, BF16)] * 2)
    dh2 = _mm(nm("d_h2_g"), dgp, W["ffn_g"], tb=True)
    dh2 = _mm(nm("d_h2_u"), dup, W["ffn_u"], tb=True, add=dh2)
    g["ffn_g"] = _mm(nm("dw_ffn_g"), sv["h2"], dgp, ta=True)
    g["ffn_u"] = _mm(nm("dw_ffn_u"), sv["h2"], dup, ta=True)
    dx1, g["norm2_g"] = _tile_call(nm("d_norm2"), _norm_bwd, n, tm, [_rows(sv["x1"]), _rows(dh2), _rows(dx2)], [P["norm2_g"]],
                                   [(D_MODEL, F32)], acc_outs=[(1, D_MODEL)])

    dmix = _mm(nm("d_mix"), dx1, W["out"], tb=True)
    g["out"] = _mm(nm("dw_out"), sv["mix"], dx1, ta=True)
    dz = jnp.zeros((n, ZW), BF16)
    dys, g["branch"] = [], []
    for i in range(3):
        dz, dp = _tile_call(nm(f"d_gate{i}"), _vjp_rows(_f_gate, 2), n, tm,
                            [(z, 1024, Z_GATE // 1024 + i), _rows(sv["ps"][i]), _rows(dmix)], [], [(1024, BF16)],
                            alias=(dz, 1024, Z_GATE // 1024 + i))
        dys.append(_mm(nm(f"d_branch{i}"), dp, W["branch"][i], tb=True))
        g["branch"].append(_mm(nm(f"dw_branch{i}"), sv["ys"][i], dp, ta=True))

    dga, dgb = _tile_call(nm("d_glu"), _vjp_rows(_f_glu, 2), n, tm, [_rows(sv["ga"]), _rows(sv["gb"]), _rows(dys[2])], [],
                          [(1024, BF16)] * 2)
    dgact = _mm(nm("d_gact_a"), dga, W["glu_a"], tb=True)
    dgact = _mm(nm("d_gact_b"), dgb, W["glu_b"], tb=True, add=dgact)
    g["glu_a"] = _mm(nm("dw_glu_a"), sv["gact"], dga, ta=True)
    g["glu_b"] = _mm(nm("dw_glu_b"), sv["gact"], dgb, ta=True)

    def act_bwd(r, c, _):
        _, vjp = jax.vjp(_f_s5_act, r[0], r[1], c[0])
        dy, du, dd = vjp(r[2])
        return [dy, du], [dd]

    dysum, du_part, g["s5_d"] = _tile_call(nm("d_s5_act"), act_bwd, n, tm,
                                           [_rows(sv["ysum"]), (z, 1024, Z_U // 1024), _rows(dgact)], [P["s5_d"]],
                                           [(1024, F32)] * 2, acc_outs=[(1, 1024)])
    dus, g["s5"] = [], []
    side_res = {}
    for d in range(2):
        du, g_b, g_c, da, side_res[d] = _s5_bwd(nm(f"d_s5_scan{d}"), z, dysum, sv["xs"][d], P["s5"][d], reverse=(d == 0),
                                                side=side_of(d, g))
        dus.append(du)
        g["s5"].append((da[:1], g_b, g_c))
    dz, = _tile_call(nm("d_s5_u"), lambda r, c, _: ([r[0] + r[1] + r[2]], []), n, tm,
                     [_rows(du_part), _rows(dus[0]), _rows(dus[1])], [], [], alias=(dz, 1024, Z_U // 1024))

    o = sv["ys"][1]
    dq, dkv, dkr = _attn_bwd(nm("d_attn"), sv["q"], sv["kv"], sv["kr"], o, sv["lse"], dys[1])
    dqraw, = _tile_call(nm("d_q_rope"), lambda r, c, _: ([_mla_q(r[0], r[1], r[2], True)], []), n, tm,
                        [_rows(dq), _rows(cos_m), _rows(sin_m)], [], [(2048, BF16)])

    def kr_bwd(r, c, _):
        tot = r[0][:, :128]
        for h in range(1, MLA_HEADS):
            tot = tot + r[0][:, 128 * h:128 * (h + 1)]
        return [_rope_t(tot, r[1], r[2])], []

    dz, = _tile_call(nm("d_k_rope"), kr_bwd, n, tm, [_rows(dkr), _rows(cos_m), _rows(sin_m)], [], [],
                     alias=(dz, 128, Z_KR // 128))
    dcqn = _mm(nm("d_cqn"), dqraw, W["uq"], tb=True)
    g["uq"] = _mm(nm("dw_uq"), sv["cqn"], dqraw, ta=True)
    dckvn = _mm(nm("d_ckvn"), dkv, W["ukv"], tb=True)
    g["ukv"] = _mm(nm("dw_ukv"), sv["ckvn"], dkv, ta=True)

    def qn_bwd(r, c, _):
        _, vjp = jax.vjp(_f_rms, r[0][:, :MLA_Q_LORA], c[0])
        da, dg = vjp(r[1])
        return [jnp.concatenate([da, jnp.zeros((da.shape[0], 512 - MLA_Q_LORA), F32)], axis=1)], [dg]

    dz, g["mla_q_norm_g"] = _tile_call(nm("d_q_norm"), qn_bwd, n, tm, [(z, 512, Z_CQ // 512), _rows(dcqn)], [P["mla_q_norm_g"]],
                                       [], acc_outs=[(1, MLA_Q_LORA)], alias=(dz, 512, Z_CQ // 512))

    def kvn_bwd(r, c, _):
        _, vjp = jax.vjp(_f_rms, r[0], c[0])
        da, dg = vjp(r[1])
        return [da], [dg]

    dz, g["mla_kv_norm_g"] = _tile_call(nm("d_kv_norm"), kvn_bwd, n, tm, [(z, 256, Z_CKV // 256), _rows(dckvn)],
                                        [P["mla_kv_norm_g"]], [], acc_outs=[(1, MLA_KV_LORA)], alias=(dz, 256, Z_CKV // 256))

    def gn_bwd(r, c, _):
        drg, dy, dg = [], [], []
        for i in range(RET_HEADS):
            sl = slice(RET_V * i, RET_V * (i + 1))
            _, vjp = jax.vjp(_f_gn_gate, r[0][:, sl], r[1][:, sl], c[0][:, sl])
            a, b, e = vjp(r[2][:, sl])
            dy.append(a)
            drg.append(b)
            dg.append(e)
        return [jnp.concatenate(drg, axis=1), jnp.concatenate(dy, axis=1)], [jnp.concatenate(dg, axis=1)]

    dz, dyraw, g["ret_gn_g"] = _tile_call(nm("d_ret_gn"), gn_bwd, n, tm,
                                          [_rows(sv["yraw"]), (z, 1024, Z_RG // 1024), _rows(dys[0])], [P["ret_gn_g"]],
                                          [(1024, F32)], acc_outs=[(1, 1024)], alias=(dz, 1024, Z_RG // 1024))
    dz, drd_f = _ret_bwd(nm("d_ret_f"), z, cos_r, sin_r, P["rd"][0], sv["stf"], dyraw, dz, reverse=False)
    dz, drd_b = _ret_bwd(nm("d_ret_b"), z, cos_r, sin_r, P["rd"][1], sv["stb"], dyraw, dz, reverse=True)
    g["ret_decay"] = jnp.stack([drd_f[:, 0, 0], drd_b[:, 0, 0]], axis=0)

    dh = _mm(nm("d_h"), dz, W["in"], tb=True)
    g["in"] = _mm(nm("dw_in"), sv["h"], dz, ta=True)
    dx, g["norm1_g"] = _tile_call(nm("d_norm1"), _norm_bwd, n, tm, [_rows(sv["x"]), _rows(dh), _rows(dx1)], [P["norm1_g"]],
                                  [(D_MODEL, F32)], acc_outs=[(1, D_MODEL)])
    return dx, g, side_res


INPUT_NAMES = ("x",) + WEIGHTS + ("loss_target",) + tuple("m_" + n for n in WEIGHTS) + tuple("v_" + n for n in WEIGHTS)
S5_NAMES = ("s5_a_re", "s5_a_im", "s5_log_dt", "s5_b_re", "s5_b_im", "s5_c_re", "s5_c_im")


def _pack_rows(arrays, tile_rows):
    pieces = []
    for a in arrays:
        flat = a.reshape(-1)
        pad = -flat.shape[0] % LANES
        if pad:
            flat = jnp.concatenate([flat, jnp.zeros((pad,), flat.dtype)])
        pieces.append(flat.reshape(-1, LANES))
    pad = -sum(p.shape[0] for p in pieces) % tile_rows
    if pad:
        pieces.append(jnp.zeros((pad, LANES), pieces[0].dtype))
    return jnp.concatenate(pieces, axis=0)


def _unpack_rows(packed, shapes):
    out, r0 = [], 0
    for s in shapes:
        size = math.prod(s)
        rows = -(-size // LANES)
        out.append(packed[r0:r0 + rows].reshape(-1)[:size].reshape(s))
        r0 += rows
    return out


FIRST = ("w_in", "mla_w_uq", "mla_w_ukv")
REST = ("s5_w_glu", "w_branch", "w_out", "ffn_w_gu", "ffn_w_down")


def _weights_first(full):
    return {"in": _perm(full["w_in"], _in_pieces()), "uq": _perm(full["mla_w_uq"], _uq_pieces()), "ukv": full["mla_w_ukv"]}


def _weights_rest(full):
    return dict(glu_a=full["s5_w_glu"][:, :1024], glu_b=full["s5_w_glu"][:, 1024:],
                branch=[full["w_branch"][i] for i in range(3)], out=full["w_out"],
                ffn_g=full["ffn_w_gu"][:, :FFN_HIDDEN], ffn_u=full["ffn_w_gu"][:, FFN_HIDDEN:], ffn_down=full["ffn_w_down"])


def _grads_first(g):
    return dict(w_in=_unperm(g["in"], _in_pieces()), mla_w_uq=_unperm(g["uq"], _uq_pieces()), mla_w_ukv=g["ukv"])


def _grads_rest(g):
    return dict(s5_w_glu=jnp.concatenate([g["glu_a"], g["glu_b"]], axis=1), w_branch=jnp.stack(g["branch"], axis=0),
                w_out=g["out"], ffn_w_gu=jnp.concatenate([g["ffn_g"], g["ffn_u"]], axis=1), ffn_w_down=g["ffn_down"])


def _local_step(inp, x, target, plan):
    n = x.shape[0]
    tables = _rope_tables(n)
    Ps, s5_vjps = [], []
    for l in range(DEPTH):
        s5, vjps = [], []
        for d in range(2):
            (arow, b_map, c_map), vjp = jax.vjp(_s5_maps, *[inp[k][l, d] for k in S5_NAMES])
            arow = lax.stop_gradient(arow)
            ak, apow = _power_tables(arow, False, d == 1)
            ak_adj, apow_adj = _power_tables(arow, True, d == 0)
            s5.append(dict(b_map=b_map, c_map=c_map, b_map_t=b_map.transpose(0, 2, 1), c_map_t=c_map.transpose(0, 2, 1),
                           ak=ak, apow=apow, ak_adj=ak_adj, apow_adj=apow_adj))
            vjps.append(vjp)
        s5_vjps.append(vjps)
        Ps.append(dict(
            norm1_g=inp["norm1_g"][l][None], norm2_g=inp["norm2_g"][l][None], ret_gn_g=inp["ret_gn_g"][l][None],
            mla_q_norm_g=inp["mla_q_norm_g"][l][None], mla_kv_norm_g=inp["mla_kv_norm_g"][l][None], s5_d=inp["s5_d"][l][None],
            rd=[jnp.broadcast_to(inp["ret_decay"][l, d][:, None, None], (RET_HEADS, 1, LANES)) for d in range(2)], s5=s5))

    h, saved, fwd_res = x, [], {}
    for l in range(DEPTH):
        sides = {d: plan["fwd_side"](l, d) for d in range(2)}

        def rest(res, l=l):
            fwd_res.update({(l, d): r for d, r in res.items()})
            return _weights_rest(plan["rest"](l, fwd_res))

        h, sv, _ = _layer_fwd(l, h, _weights_first(plan["first"](l, fwd_res)), rest, Ps[l], tables,
                              {d: s for d, s in sides.items() if s is not None})
        saved.append(sv)

    def loss_bwd(r, c, _):
        loss, vjp = jax.vjp(lambda a, gain: _f_loss(a, gain, r[1]), r[0], c[0])
        da, dg = vjp(jnp.ones((), F32))
        return [da], [dg, jnp.broadcast_to(loss, (1, LANES))]

    dh, g_final, loss_row = _tile_call("loss", loss_bwd, n, _pick(n, 256), [_rows(h), _rows(target)], [inp["final_g"][None]],
                                       [(D_MODEL, F32)], acc_outs=[(1, D_MODEL), (1, LANES)])
    layer_g, wgrads, bwd_res = [None] * DEPTH, [None] * DEPTH, {}
    for l in reversed(range(DEPTH)):
        dh, layer_g[l], res = _layer_bwd(l, dh, saved[l], Ps[l], tables, lambda d, g, l=l: plan["bwd_side"](l, d, g, wgrads))
        wgrads[l] = {**_grads_first(layer_g[l]), **_grads_rest(layer_g[l])}
        bwd_res.update({(l, d): r for d, r in res.items()})

    def stack(f):
        return jnp.stack([f(layer_g[l], l) for l in range(DEPTH)], axis=0)

    grads = dict(
        **{k: jnp.stack([wgrads[l][k] for l in range(DEPTH)], axis=0) for k in SHARDED},
        norm1_g=stack(lambda g, l: g["norm1_g"][0]), norm2_g=stack(lambda g, l: g["norm2_g"][0]),
        ret_gn_g=stack(lambda g, l: g["ret_gn_g"][0]), mla_q_norm_g=stack(lambda g, l: g["mla_q_norm_g"][0]),
        mla_kv_norm_g=stack(lambda g, l: g["mla_kv_norm_g"][0]), s5_d=stack(lambda g, l: g["s5_d"][0]),
        ret_decay=stack(lambda g, l: g["ret_decay"]), final_g=g_final[0])
    s5_grads = [[s5_vjps[l][d](layer_g[l]["s5"][d]) for d in range(2)] for l in range(DEPTH)]
    for i, k in enumerate(S5_NAMES):
        grads[k] = jnp.stack([jnp.stack([s5_grads[l][d][i] for d in range(2)], axis=0) for l in range(DEPTH)], axis=0)
    return loss_row[0, 0], dh, grads, wgrads, bwd_res


def kernel(*args):
    inp = dict(zip(INPUT_NAMES, args))
    kinds = ("grad_", "delta_", "new_m_", "new_v_")

    def local(l, names):
        return [inp[k][l].astype(BF16) for k in names]

    def whole(names, gathered):
        return {k: _unshard(g, SHARD_AXIS[k] - 1) for k, g in zip(names, gathered)}

    def parts(names, wg):
        return ("exchange", [_shard_split(wg[k], SHARD_AXIS[k] - 1).astype(BF16) for k in names])

    first0 = _all_gather("gather_weights0", local(0, FIRST))
    fwd_sides = {(0, 0): ("gather", local(0, REST)), (0, 1): ("gather", local(1, FIRST + REST))}

    def bwd_side(l, d, g, wgrads):
        if (l, d) == (0, 0):
            return parts(FIRST, wgrads[1])
        return parts(REST, _grads_rest(g)) if (l, d) in ((1, 0), (0, 1)) else None

    plan = dict(
        first=lambda l, res: whole(FIRST, first0 if l == 0 else res[(0, 1)][:len(FIRST)]),
        rest=lambda l, res: whole(REST, res[(0, 0)] if l == 0 else res[(0, 1)][len(FIRST):]),
        fwd_side=lambda l, d: fwd_sides.get((l, d)), bwd_side=bwd_side)

    loss, dh, grads, wgrads, landed = _local_step(inp, inp["x"][0], inp["loss_target"][0], plan)
    loss = lax.psum(loss, ("x", "y", "c"))

    landed[(0, 2)] = _exchange("exchange_grads0", parts(FIRST, wgrads[0]))
    by_layer = [dict(zip(FIRST + REST, landed[(0, 2)] + landed[(0, 1)])), dict(zip(FIRST + REST, landed[(0, 0)] + landed[(1, 0)]))]
    out = {}
    for k in SHARDED:
        land = jnp.stack([by_layer[0][k], by_layer[1][k]], axis=1)
        shape = inp[k].shape
        rows, cols = math.prod(shape[:-1]), shape[-1]
        res = _adamw("adamw_" + k, land.reshape(N_DEV, rows, cols), *[inp[p + k].reshape(rows, cols) for p in ("", "m_", "v_")],
                     _row_tile(rows))
        for kind, t in zip(kinds, res):
            out[kind + k] = t.reshape(shape)

    small_tr = 512
    partial = _pack_rows([grads[k] for k in SMALL], small_tr)
    packed = [_pack_rows([inp[p + k] for k in SMALL], small_tr) for p in ("", "m_", "v_")]
    res_small = _adamw("adamw_small", _all_gather("gather_small_grads", [partial])[0], *packed, small_tr)
    for kind, b in zip(kinds, res_small):
        for k, t in zip(SMALL, _unpack_rows(b, [inp[k].shape for k in SMALL])):
            out[kind + k] = t
    return (loss, dh[None]) + tuple(out[kind + k] for kind in kinds for k in WEIGHTS)
```

```python
import functools
import math

import jax
import jax.numpy as jnp
from jax import lax
from jax.experimental import pallas as pl
from jax.experimental.pallas import tpu as pltpu

F32 = jnp.float32
BF16 = jnp.bfloat16

D_MODEL = 1024
DEPTH = 2
RMS_EPS = 1e-6
GN_EPS = 1e-5
ROPE_THETA = 10000.0
RET_HEADS, RET_QK, RET_V, RET_CHUNK = 4, 128, 256, 128
MLA_HEADS, MLA_Q_LORA, MLA_KV_LORA, MLA_NOPE, MLA_ROPE, MLA_V = 8, 384, 256, 128, 64, 128
S5_GROUPS, S5_GROUP, S5_STATE = 64, 16, 64
S5_BLOCKS = 8
FFN_HIDDEN = 2816
N_DEV = 8
ADAM_LR, ADAM_B1, ADAM_B2, ADAM_EPS, ADAM_WD, ADAM_STEP = 0.001, 0.9, 0.999, 1e-08, 0.01, 10

LANES = 128
SUBLANES = 8
VMEM_LIMIT = 48 * 1024 * 1024

ZW = 8192
Z_RET = 0
Z_RG = 2048
Z_U = 3072
Z_GATE = 4096
Z_CQ = 7168
Z_CKV = 7680
Z_KR = 7936
IN_SPLITS = (512, 512, 1024, 1024, 384, 256, 64, 1024, 3072)

SHARDED = ("w_in", "mla_w_uq", "mla_w_ukv", "s5_w_glu", "w_branch", "w_out", "ffn_w_gu", "ffn_w_down")
SHARD_AXIS = {"w_in": 2, "mla_w_uq": 2, "mla_w_ukv": 2, "s5_w_glu": 2, "w_branch": 2, "w_out": 1,
              "ffn_w_gu": 2, "ffn_w_down": 1}
SMALL = ("norm1_g", "ret_decay", "ret_gn_g", "mla_q_norm_g", "mla_kv_norm_g", "s5_a_re", "s5_a_im", "s5_log_dt",
         "s5_b_re", "s5_b_im", "s5_c_re", "s5_c_im", "s5_d", "norm2_g", "final_g")
WEIGHTS = ("norm1_g", "w_in", "ret_decay", "ret_gn_g", "mla_q_norm_g", "mla_w_uq", "mla_kv_norm_g", "mla_w_ukv",
           "s5_a_re", "s5_a_im", "s5_log_dt", "s5_b_re", "s5_b_im", "s5_c_re", "s5_c_im", "s5_d", "s5_w_glu",
           "w_branch", "w_out", "norm2_g", "ffn_w_gu", "ffn_w_down", "final_g")


def _params(sem=None):
    return pltpu.CompilerParams(dimension_semantics=sem, vmem_limit_bytes=VMEM_LIMIT)


def _pick(n, cap):
    if n <= cap:
        return n
    t = cap - cap % LANES
    while t >= LANES:
        if n % t == 0:
            return t
        t -= LANES
    return n


@functools.partial(jax.custom_vjp, nondiff_argnums=(2, 3))
def _bdot(a, b, ca, cb):
    return lax.dot_general(a.astype(BF16), b.astype(BF16), (((ca,), (cb,)), ((), ())), preferred_element_type=F32)


def _bdot_fwd(a, b, ca, cb):
    return _bdot(a, b, ca, cb), (a, b)


def _bdot_bwd(ca, cb, res, g):
    a, b = res
    da = _bdot(g, b, 1, 1 - cb) if ca == 1 else _bdot(b, g, 1 - cb, 1)
    db = _bdot(a, g, 1 - ca, 0) if cb == 0 else _bdot(g, a, 0, 1 - ca)
    return da, db


_bdot.defvjp(_bdot_fwd, _bdot_bwd)


@jax.custom_vjp
def _swap_halves(x):
    return pltpu.roll(x, LANES // 2, 1)


_swap_halves.defvjp(lambda x: (_swap_halves(x), None), lambda _, g: (_swap_halves(g),))


def _rope(x, cosf, sinf):
    return x * cosf + _swap_halves(x) * sinf


def _f_rms(x, g):
    return x * lax.rsqrt(jnp.mean(x * x, axis=-1, keepdims=True) + RMS_EPS) * g


def _rope_t(g, cosf, sinf):
    return g * cosf + _swap_halves(g * sinf)


def _f_gn_gate(yh, rgh, gh):
    mu = jnp.mean(yh, axis=-1, keepdims=True)
    var = jnp.mean(jnp.square(yh - mu), axis=-1, keepdims=True)
    return jax.nn.silu(rgh) * ((yh - mu) * lax.rsqrt(var + GN_EPS) * gh)


MLA_SCALE = (MLA_NOPE + MLA_ROPE) ** -0.5


def _mla_q(qraw, cosf, sinf, transpose):
    parts = []
    for h in range(MLA_HEADS):
        parts.append(qraw[:, 256 * h:256 * h + 128] * MLA_SCALE)
        r = qraw[:, 256 * h + 128:256 * h + 256]
        parts.append(_rope_t(r * MLA_SCALE, cosf, sinf) if transpose else _rope(r, cosf, sinf) * MLA_SCALE)
    return jnp.concatenate(parts, axis=1)


def _f_s5_act(ysum, u, d):
    return jax.nn.gelu(ysum + d * u)


def _f_glu(ga, gb):
    return ga * jax.nn.sigmoid(gb)


def _f_gate(zg, proj):
    return jax.nn.sigmoid(zg) * proj


def _f_swiglu(gp, up):
    return jax.nn.silu(gp) * up


def _f_loss(x, g, target):
    y = _f_rms(x, g)
    err = jnp.square(y - target)
    return 0.5 * jnp.sum(jnp.mean(err, axis=-1))


def _tile_call(name, fn, n_rows, tm, row_ins, consts, row_outs, acc_outs=(), alias=None):
    n_row_in, n_const = len(row_ins), len(consts)
    args = [a for a, _, _ in row_ins] + list(consts)
    in_specs = [pl.BlockSpec((tm, w), lambda i, cb=cb: (i, cb)) for _, w, cb in row_ins]
    in_specs += [pl.BlockSpec(c.shape, lambda i: (0, 0)) for c in consts]
    out_shape, out_specs, aliases = [], [], {}
    if alias is not None:
        arr, w, cb = alias
        in_specs.append(pl.BlockSpec((tm, w), lambda i, cb=cb: (i, cb)))
        aliases[len(args)] = 0
        args.append(arr)
        out_shape.append(jax.ShapeDtypeStruct(arr.shape, arr.dtype))
        out_specs.append(pl.BlockSpec((tm, w), lambda i, cb=cb: (i, cb)))
    for w, dt in row_outs:
        out_shape.append(jax.ShapeDtypeStruct((n_rows, w), dt))
        out_specs.append(pl.BlockSpec((tm, w), lambda i: (i, 0)))
    n_row_out = len(out_shape)
    for r, w in acc_outs:
        out_shape.append(jax.ShapeDtypeStruct((r, w), F32))
        out_specs.append(pl.BlockSpec((r, w), lambda i: (0, 0)))
    n_in = len(args)

    def body(*refs):
        ins, outs = refs[:n_in], refs[n_in:]
        rows = [r[...] for r in ins[:n_row_in]]
        cons = [r[...] for r in ins[n_row_in:n_row_in + n_const]]
        prev = ins[-1][...] if alias is not None else None
        res_rows, res_accs = fn(rows, cons, prev)
        for o, r in zip(outs[:n_row_out], res_rows):
            o[...] = r.astype(o.dtype)
        if acc_outs:
            @pl.when(pl.program_id(0) == 0)
            def _():
                for o in outs[n_row_out:]:
                    o[...] = jnp.zeros(o.shape, F32)
            for o, r in zip(outs[n_row_out:], res_accs):
                o[...] += r

    return pl.pallas_call(
        body, name=name, grid=(n_rows // tm,), in_specs=in_specs, out_specs=out_specs, out_shape=out_shape,
        input_output_aliases=aliases, compiler_params=_params(("arbitrary",)))(*args)


def _mm(name, a, b, *, ta=False, tb=False, add=None, out_dtype=F32):
    (K, M) = a.shape if ta else a.shape[::-1]
    (N, K2) = b.shape if tb else b.shape[::-1]
    assert K == K2, (name, a.shape, b.shape)
    tm, tn, tk = _pick(M, 1536), _pick(N, 1536), _pick(K, 1024)
    gi, gj, nk = M // tm, N // tn, K // tk
    a_bytes, b_bytes = a.size * a.dtype.itemsize, b.size * b.dtype.itemsize
    j_outer = nk == 1 and b_bytes + a_bytes * gj < a_bytes + b_bytes * gi

    def im(f):
        return (lambda g0, g1, k: f(g1, g0, k)) if j_outer else f

    a_spec = pl.BlockSpec((tk, tm), im(lambda i, j, k: (k, i))) if ta else pl.BlockSpec((tm, tk), im(lambda i, j, k: (i, k)))
    b_spec = pl.BlockSpec((tn, tk), im(lambda i, j, k: (j, k))) if tb else pl.BlockSpec((tk, tn), im(lambda i, j, k: (k, j)))
    o_spec = pl.BlockSpec((tm, tn), im(lambda i, j, k: (i, j)))
    dn = (((0 if ta else 1,), (1 if tb else 0,)), ((), ()))
    has_add = add is not None

    def body(*refs):
        if has_add:
            a_ref, b_ref, add_ref, o_ref, acc = refs
        else:
            a_ref, b_ref, o_ref, acc = refs
        k = pl.program_id(2)

        @pl.when(k == 0)
        def _():
            acc[...] = jnp.zeros(acc.shape, F32)

        acc[...] += lax.dot_general(a_ref[...].astype(BF16), b_ref[...].astype(BF16), dn, preferred_element_type=F32)

        @pl.when(k == nk - 1)
        def _():
            r = acc[...]
            if has_add:
                r = r + add_ref[...]
            o_ref[...] = r.astype(out_dtype)

    args, specs = [a, b], [a_spec, b_spec]
    if has_add:
        args.append(add)
        specs.append(o_spec)
    return pl.pallas_call(
        body, name=name, grid=(gj, gi, nk) if j_outer else (gi, gj, nk), in_specs=specs, out_specs=o_spec,
        out_shape=jax.ShapeDtypeStruct((M, N), out_dtype), scratch_shapes=[pltpu.VMEM((tm, tn), F32)],
        compiler_params=_params(("parallel", "parallel", "arbitrary")))(*args)


S5_BW = 2 * S5_STATE * (S5_GROUPS // S5_BLOCKS)


def _scan_tile(buf, ak_ref, ap_ref, carry, *, reverse, x_ref=None, da_ref=None):
    tt, bw = buf.shape
    hw = bw // 2
    ng = tt // SUBLANES
    rowid = lax.broadcasted_iota(jnp.int32, (SUBLANES, hw), 0)
    steps = [(SUBLANES - k if reverse else k, SUBLANES * n) for n, k in enumerate((1, 2, 4))]
    apr, api = ap_ref[:, :hw], ap_ref[:, hw:]
    first = (rowid == SUBLANES - 1) if reverse else (rowid == 0)

    def group(gi, c):
        cr, ci = c
        r0 = pl.multiple_of(((ng - 1 - gi) if reverse else gi) * SUBLANES, SUBLANES)
        xr, xi = buf[pl.ds(r0, SUBLANES), :hw], buf[pl.ds(r0, SUBLANES), hw:]
        for sh, a0 in steps:
            kr, ki = ak_ref[a0:a0 + SUBLANES, :hw], ak_ref[a0:a0 + SUBLANES, hw:]
            sr, si = pltpu.roll(xr, sh, 0), pltpu.roll(xi, sh, 0)
            xr, xi = xr + kr * sr - ki * si, xi + kr * si + ki * sr
        xr, xi = xr + apr * cr - api * ci, xi + apr * ci + api * cr
        buf[pl.ds(r0, SUBLANES), :hw] = xr
        buf[pl.ds(r0, SUBLANES), hw:] = xi
        if x_ref is not None:
            sh1 = SUBLANES - 1 if reverse else 1
            pr = jnp.where(first, cr, pltpu.roll(xr, sh1, 0))
            pi = jnp.where(first, ci, pltpu.roll(xi, sh1, 0))
            sr, si = x_ref[pl.ds(r0, SUBLANES), :hw], x_ref[pl.ds(r0, SUBLANES), hw:]
            da_ref[:, :hw] += pr * sr + pi * si
            da_ref[:, hw:] += pi * sr - pr * si
        last = 0 if reverse else SUBLANES - 1
        return (jnp.broadcast_to(xr[last:last + 1], (SUBLANES, hw)), jnp.broadcast_to(xi[last:last + 1], (SUBLANES, hw)))

    cr, ci = lax.fori_loop(0, ng, group, (carry[:, :hw], carry[:, hw:]))
    carry[:, :hw] = cr
    carry[:, hw:] = ci


def _s5_specs(n_rows, reverse):
    tt = _pick(n_rows, 512)
    nt = n_rows // tt

    def rows(width, off):
        return pl.BlockSpec((tt, width), lambda j, t: ((nt - 1 - t) if reverse else t, off + j))

    def per_block(r, c):
        return pl.BlockSpec((None, r, c), lambda j, t: (j, 0, 0))

    def par(r):
        return pl.BlockSpec((r, S5_BW), lambda j, t: (0, j))

    return tt, nt, rows, per_block, par


def _s5_call(name, core, args, in_specs, out_specs, out_shape, scratch, nt, side):
    n_out = len(out_shape)
    if side is None:
        body, sem = core, ("parallel", "arbitrary")
    else:
        make, shapes, sems = _exchange_parts(side)
        n, n_in, n_sc = len(shapes), len(args), len(scratch)
        hbm = pl.BlockSpec(memory_space=pl.ANY)

        def body(*refs):
            ins, xs = refs[:n_in], refs[n_in:n_in + n]
            outs, lands = refs[n_in + n:n_in + n + n_out], refs[n_in + n + n_out:n_in + 2 * n + n_out]
            sc = refs[n_in + 2 * n + n_out:n_in + 2 * n + n_out + n_sc]
            start, forward, finish = make(xs, lands, *refs[-3:])
            j, t = pl.program_id(0), pl.program_id(1)
            pl.when((j == 0) & (t == 0))(start)
            pl.when((j == S5_BLOCKS // 2) & (t == 0))(forward)
            core(*ins, *outs, *sc)
            pl.when((j == S5_BLOCKS - 1) & (t == nt - 1))(finish)

        args, in_specs = list(args) + list(side[1]), list(in_specs) + [hbm] * n
        out_specs, out_shape = list(out_specs) + [hbm] * n, list(out_shape) + shapes
        scratch, sem = list(scratch) + sems, ("arbitrary", "arbitrary")
    res = pl.pallas_call(body, name=name, grid=(S5_BLOCKS, nt), in_specs=in_specs, out_specs=out_specs, out_shape=out_shape,
                         scratch_shapes=scratch, compiler_params=_params(sem))(*args)
    return list(res[:n_out]) + [None if side is None else list(res[n_out:])]


def _s5_fwd(name, z, m, *, reverse, side=None):
    n_rows = z.shape[0]
    tt, nt, rows, per_block, par = _s5_specs(n_rows, reverse)

    def body(u_ref, b_ref, c_ref, ak_ref, ap_ref, x_ref, y_ref, carry):
        @pl.when(pl.program_id(1) == 0)
        def _():
            carry[...] = jnp.zeros(carry.shape, F32)

        x_ref[...] = jnp.dot(u_ref[...].astype(BF16), b_ref[...].astype(BF16), preferred_element_type=F32)
        _scan_tile(x_ref, ak_ref, ap_ref, carry, reverse=reverse)
        y_ref[...] = jnp.dot(x_ref[...].astype(BF16), c_ref[...].astype(BF16), preferred_element_type=F32)

    return _s5_call(
        name, body, [z, m["b_map"], m["c_map"], m["ak"], m["apow"]],
        [rows(LANES, Z_U // LANES), per_block(LANES, S5_BW), per_block(S5_BW, LANES), par(3 * SUBLANES), par(SUBLANES)],
        [rows(S5_BW, 0), rows(LANES, 0)],
        [jax.ShapeDtypeStruct((n_rows, S5_BLOCKS * S5_BW), F32), jax.ShapeDtypeStruct((n_rows, S5_BLOCKS * LANES), F32)],
        [pltpu.VMEM((SUBLANES, S5_BW), F32)], nt, side)


def _s5_bwd(name, z, dy, xs, m, *, reverse, side=None):
    n_rows = z.shape[0]
    tt, nt, rows, per_block, par = _s5_specs(n_rows, reverse)

    def body(u_ref, dy_ref, x_ref, bt_ref, ct_ref, ak_ref, ap_ref, du_ref, db_ref, dc_ref, da_ref, lam, carry):
        t = pl.program_id(1)

        @pl.when(t == 0)
        def _():
            carry[...] = jnp.zeros(carry.shape, F32)
            db_ref[...] = jnp.zeros(db_ref.shape, F32)
            dc_ref[...] = jnp.zeros(dc_ref.shape, F32)
            da_ref[...] = jnp.zeros(da_ref.shape, F32)

        dy_b = dy_ref[...].astype(BF16)
        lam[...] = jnp.dot(dy_b, ct_ref[...].astype(BF16), preferred_element_type=F32)
        _scan_tile(lam, ak_ref, ap_ref, carry, reverse=reverse, x_ref=x_ref, da_ref=da_ref)
        lam_b = lam[...].astype(BF16)
        du_ref[...] = jnp.dot(lam_b, bt_ref[...].astype(BF16), preferred_element_type=F32)
        db_ref[...] += lax.dot_general(u_ref[...].astype(BF16), lam_b, _TN, preferred_element_type=F32)
        dc_ref[...] += lax.dot_general(x_ref[...].astype(BF16), dy_b, _TN, preferred_element_type=F32)

        @pl.when(t == nt - 1)
        def _():
            da_ref[...] = jnp.broadcast_to(jnp.sum(da_ref[...], axis=0, keepdims=True), da_ref.shape)

    return _s5_call(
        name, body, [z, dy, xs, m["b_map_t"], m["c_map_t"], m["ak_adj"], m["apow_adj"]],
        [rows(LANES, Z_U // LANES), rows(LANES, 0), rows(S5_BW, 0), per_block(S5_BW, LANES), per_block(LANES, S5_BW),
         par(3 * SUBLANES), par(SUBLANES)],
        [rows(LANES, 0), per_block(LANES, S5_BW), per_block(S5_BW, LANES), par(SUBLANES)],
        [jax.ShapeDtypeStruct((n_rows, S5_BLOCKS * LANES), F32), jax.ShapeDtypeStruct((S5_BLOCKS, LANES, S5_BW), F32),
         jax.ShapeDtypeStruct((S5_BLOCKS, S5_BW, LANES), F32), jax.ShapeDtypeStruct((SUBLANES, S5_BLOCKS * S5_BW), F32)],
        [pltpu.VMEM((tt, S5_BW), F32), pltpu.VMEM((SUBLANES, S5_BW), F32)], nt, side)


def _ret_chunk(zq, zk, v, cosf, sinf, state, rd, reverse):
    c = RET_CHUNK
    lg = jax.nn.log_sigmoid(rd)
    lg1 = jnp.max(lg, axis=1, keepdims=True)
    q = _rope(zq, cosf, sinf) * (RET_QK ** -0.5)
    k = _rope(zk, cosf, sinf)
    pi = lax.broadcasted_iota(jnp.int32, (c, c), 0).astype(F32)
    pj = lax.broadcasted_iota(jnp.int32, (c, c), 1).astype(F32)
    pcol = lax.broadcasted_iota(jnp.int32, (c, 1), 0).astype(F32)
    if reverse:
        diff, mask, pos = pj - pi, pj > pi, (c - 1) - pcol
    else:
        diff, mask, pos = pi - pj, pi >= pj, pcol
    decay_in = jnp.where(mask, jnp.exp(jnp.where(mask, diff, 0.0) * lg), 0.0)
    scores = _bdot(q, k, 1, 1) * decay_in
    inner = _bdot(scores, v, 1, 0)
    k_w = jnp.exp((c - 1 - pos) * lg1)
    kv = _bdot(k * k_w, v, 0, 0)
    q_w = jnp.exp((pos + 1) * lg1)
    cross = _bdot(q, state, 1, 0) * q_w
    new_state = jnp.exp(c * lg1) * state + kv
    return inner + cross, new_state


RET_ZW = RET_HEADS * (2 * RET_QK + RET_V)


def _ret_specs(n_chunks, reverse_order):
    cmap = (lambda n: n_chunks - 1 - n) if reverse_order else (lambda n: n)
    z_spec = pl.BlockSpec((RET_CHUNK, RET_ZW), lambda n: (cmap(n), 0))
    t_spec = pl.BlockSpec((RET_CHUNK, LANES), lambda n: (cmap(n), 0))
    rd_spec = pl.BlockSpec((RET_HEADS, 1, LANES), lambda n: (0, 0, 0))
    o_spec = pl.BlockSpec((RET_CHUNK, RET_HEADS * RET_V), lambda n: (cmap(n), 0))
    st_spec = pl.BlockSpec((RET_HEADS, None, RET_QK, RET_V), lambda n: (0, cmap(n), 0, 0))
    return z_spec, t_spec, rd_spec, o_spec, st_spec


def _ret_head(zt, h):
    b = h * (2 * RET_QK + RET_V)
    return zt[:, b:b + RET_QK], zt[:, b + RET_QK:b + 2 * RET_QK], zt[:, b + 2 * RET_QK:b + 2 * RET_QK + RET_V]


def _ret_fwd(name, z, cosf, sinf, rd, *, reverse):
    n_rows = z.shape[0]
    n_chunks = n_rows // RET_CHUNK
    z_spec, t_spec, rd_spec, o_spec, st_spec = _ret_specs(n_chunks, reverse)

    def body(z_ref, cos_ref, sin_ref, rd_ref, o_ref, st_ref, state):
        @pl.when(pl.program_id(0) == 0)
        def _():
            state[...] = jnp.zeros(state.shape, F32)

        zt = z_ref[...]
        cosv, sinv = cos_ref[...], sin_ref[...]
        for h in range(RET_HEADS):
            st = state[h]
            st_ref[h] = st
            out, new = _ret_chunk(*_ret_head(zt, h), cosv, sinv, st, rd_ref[h], reverse)
            o_ref[:, RET_V * h:RET_V * (h + 1)] = out
            state[h] = new

    return pl.pallas_call(
        body, name=name, grid=(n_chunks,), in_specs=[z_spec, t_spec, t_spec, rd_spec],
        out_specs=[o_spec, st_spec],
        out_shape=[jax.ShapeDtypeStruct((n_rows, RET_HEADS * RET_V), F32),
                   jax.ShapeDtypeStruct((RET_HEADS, n_chunks, RET_QK, RET_V), F32)],
        scratch_shapes=[pltpu.VMEM((RET_HEADS, RET_QK, RET_V), F32)], compiler_params=_params(("arbitrary",)))(z, cosf, sinf, rd)


def _ret_bwd(name, z, cosf, sinf, rd, states, dout, dz, *, reverse):
    n_rows = z.shape[0]
    n_chunks = n_rows // RET_CHUNK
    z_spec, t_spec, rd_spec, o_spec, st_spec = _ret_specs(n_chunks, not reverse)

    def body(z_ref, cos_ref, sin_ref, rd_ref, st_ref, do_ref, dzin_ref, dz_ref, drd_ref, dstate):
        @pl.when(pl.program_id(0) == 0)
        def _():
            dstate[...] = jnp.zeros(dstate.shape, F32)
            drd_ref[...] = jnp.zeros(drd_ref.shape, F32)

        zt = z_ref[...]
        cosv, sinv = cos_ref[...], sin_ref[...]
        parts = []
        for h in range(RET_HEADS):
            _, vjp = jax.vjp(lambda a, b, c, s, r: _ret_chunk(a, b, c, cosv, sinv, s, r, reverse),
                             *_ret_head(zt, h), st_ref[h], rd_ref[h])
            dq, dk, dv, dst, drd = vjp((do_ref[:, RET_V * h:RET_V * (h + 1)], dstate[h]))
            parts += [dq, dk, dv]
            dstate[h] = dst
            drd_ref[h] += jnp.sum(drd, axis=1, keepdims=True)
        dz_ref[...] = (dzin_ref[...].astype(F32) + jnp.concatenate(parts, axis=1)).astype(dz_ref.dtype)

    return pl.pallas_call(
        body, name=name, grid=(n_chunks,),
        in_specs=[z_spec, t_spec, t_spec, rd_spec, st_spec, o_spec, z_spec],
        out_specs=[z_spec, rd_spec],
        out_shape=[jax.ShapeDtypeStruct(dz.shape, dz.dtype), jax.ShapeDtypeStruct((RET_HEADS, 1, LANES), F32)],
        input_output_aliases={6: 0}, scratch_shapes=[pltpu.VMEM((RET_HEADS, RET_QK, RET_V), F32)],
        compiler_params=_params(("arbitrary",)))(z, cosf, sinf, rd, states, dout, dz)


_NT = (((1,), (1,)), ((), ()))
_TN = (((0,), (0,)), ((), ()))


def _attn_tiles(n_rows, tq_cap):
    tq, tk = _pick(n_rows, tq_cap), _pick(n_rows, 2048)
    return tq, max(tq // 2, LANES), tk, min(tk, 1024)


def _attn_fwd(name, q, kv, kr):
    n_rows = q.shape[0]
    tq, hq, tk, sub = _attn_tiles(n_rows, 512)
    nk = n_rows // tk

    def body(q_ref, kn_ref, v_ref, kr_ref, o_ref, lse_ref, m_sc, acc):
        j = pl.program_id(2)

        @pl.when(j == 0)
        def _():
            m_sc[...] = jnp.full(m_sc.shape, -jnp.inf, F32)
            acc[...] = jnp.zeros(acc.shape, F32)

        for c in range(tk // sub):
            rows = slice(c * sub, (c + 1) * sub)
            k = jnp.concatenate([kn_ref[rows, :], kr_ref[rows, :]], axis=1)
            v1 = jnp.concatenate([v_ref[rows, :], jnp.ones((sub, LANES), BF16)], axis=1)
            for part in range(tq // hq):
                qr = slice(part * hq, (part + 1) * hq)
                s = lax.dot_general(q_ref[qr, :], k, _NT, preferred_element_type=F32)
                m_prev = m_sc[qr, :]
                m_new = jnp.maximum(m_prev, jnp.max(s, axis=1, keepdims=True))
                p = jnp.exp(s - m_new)
                acc[qr, :] = jnp.exp(m_prev - m_new) * acc[qr, :] + jnp.dot(p.astype(BF16), v1, preferred_element_type=F32)
                m_sc[qr, :] = m_new

        @pl.when(j == nk - 1)
        def _():
            l = acc[:, LANES:]
            o_ref[...] = acc[:, :LANES] / l
            lse_ref[...] = m_sc[...] + jnp.log(l)

    return pl.pallas_call(
        body, name=name, grid=(MLA_HEADS, n_rows // tq, nk),
        in_specs=[pl.BlockSpec((tq, 256), lambda h, i, j: (i, h)),
                  pl.BlockSpec((tk, 128), lambda h, i, j: (j, 2 * h)),
                  pl.BlockSpec((tk, 128), lambda h, i, j: (j, 2 * h + 1)),
                  pl.BlockSpec((tk, 128), lambda h, i, j: (j, 0))],
        out_specs=[pl.BlockSpec((tq, 128), lambda h, i, j: (i, h)),
                   pl.BlockSpec((None, tq, 128), lambda h, i, j: (h, i, 0))],
        out_shape=[jax.ShapeDtypeStruct((n_rows, MLA_HEADS * MLA_V), F32),
                   jax.ShapeDtypeStruct((MLA_HEADS, n_rows, LANES), F32)],
        scratch_shapes=[pltpu.VMEM((tq, 1), F32), pltpu.VMEM((tq, 2 * LANES), F32)],
        compiler_params=_params(("parallel", "parallel", "arbitrary")))(q, kv, kv, kr)


def _attn_bwd(name, q, kv, kr, o, lse, do):
    n_rows = q.shape[0]
    tq, hq, tk, sub = _attn_tiles(n_rows, 1024)
    nq = n_rows // tq

    def body(q_ref, kn_ref, v_ref, kr_ref, o_ref, lse_ref, do_ref, dq_ref, dkv_ref, dkr_ref, dk_acc, dv_acc):
        j, i = pl.program_id(1), pl.program_id(2)

        @pl.when(i == 0)
        def _():
            dk_acc[...] = jnp.zeros(dk_acc.shape, F32)
            dv_acc[...] = jnp.zeros(dv_acc.shape, F32)

        @pl.when((i == 0) & (j == 0))
        def _():
            dq_ref[...] = jnp.zeros(dq_ref.shape, F32)

        for part in range(tq // hq):
            qr = slice(part * hq, (part + 1) * hq)
            qv = q_ref[qr, :]
            do = do_ref[qr, :]
            do_b = do.astype(BF16)
            delta = jnp.sum(do * o_ref[qr, :], axis=1, keepdims=True)
            lse_col = lse_ref[qr, :][:, :1]
            dq = None
            for c in range(tk // sub):
                rows = slice(c * sub, (c + 1) * sub)
                k = jnp.concatenate([kn_ref[rows, :], kr_ref[rows, :]], axis=1)
                s = lax.dot_general(qv, k, _NT, preferred_element_type=F32)
                p = jnp.exp(s - lse_col)
                dp = lax.dot_general(do_b, v_ref[rows, :], _NT, preferred_element_type=F32)
                ds = (p * (dp - delta)).astype(BF16)
                dv_acc[rows, :] += lax.dot_general(p.astype(BF16), do_b, _TN, preferred_element_type=F32)
                dk_acc[rows, :] += lax.dot_general(ds, qv, _TN, preferred_element_type=F32)
                t = jnp.dot(ds, k, preferred_element_type=F32)
                dq = t if dq is None else dq + t
            r0 = pl.multiple_of(i * tq + part * hq, hq)
            dq_ref[pl.ds(r0, hq), :] += dq

        @pl.when(i == nq - 1)
        def _():
            dkv_ref[...] = jnp.concatenate([dk_acc[:, :128], dv_acc[...]], axis=1).astype(dkv_ref.dtype)
            dkr_ref[...] = dk_acc[:, 128:]

    return pl.pallas_call(
        body, name=name, grid=(MLA_HEADS, n_rows // tk, nq),
        in_specs=[pl.BlockSpec((tq, 256), lambda h, j, i: (i, h)),
                  pl.BlockSpec((tk, 128), lambda h, j, i: (j, 2 * h)),
                  pl.BlockSpec((tk, 128), lambda h, j, i: (j, 2 * h + 1)),
                  pl.BlockSpec((tk, 128), lambda h, j, i: (j, 0)),
                  pl.BlockSpec((tq, 128), lambda h, j, i: (i, h)),
                  pl.BlockSpec((None, tq, 128), lambda h, j, i: (h, i, 0)),
                  pl.BlockSpec((tq, 128), lambda h, j, i: (i, h))],
        out_specs=[pl.BlockSpec((n_rows, 256), lambda h, j, i: (0, h)),
                   pl.BlockSpec((tk, 256), lambda h, j, i: (j, h)),
                   pl.BlockSpec((tk, 128), lambda h, j, i: (j, h))],
        out_shape=[jax.ShapeDtypeStruct((n_rows, MLA_HEADS * 256), F32), jax.ShapeDtypeStruct((n_rows, MLA_HEADS * 256), BF16),
                   jax.ShapeDtypeStruct((n_rows, MLA_HEADS * 128), F32)],
        scratch_shapes=[pltpu.VMEM((tk, 256), F32), pltpu.VMEM((tk, 128), F32)],
        compiler_params=_params(("parallel", "arbitrary", "arbitrary")))(q, kv, kv, kr, o, lse, do)


_MESH = pl.DeviceIdType.MESH


def _gather_stages(x_refs, out_refs, send_sems, recv_sems, local_sems):
    n = len(x_refs)
    mx, my, mc = lax.axis_index("x"), lax.axis_index("y"), lax.axis_index("c")
    me, sibling = (mx, my, mc), (mx, my, 1 - mc)
    chips = [(1 - mx, my), (mx, 1 - my), (1 - mx, 1 - my)]

    def copy(a, k, block, to, src=None):
        dst = out_refs[a].at[4 * block[0] + 2 * block[1] + block[2]]
        return pltpu.make_async_remote_copy(
            src_ref=dst if src is None else src, dst_ref=dst, send_sem=send_sems.at[7 * a + k],
            recv_sem=recv_sems.at[7 * a + k], device_id=to, device_id_type=_MESH)

    def mine():
        return [pltpu.make_async_copy(x_refs[a], out_refs[a].at[4 * mx + 2 * my + mc], local_sems.at[a]) for a in range(n)]

    def first():
        out = []
        for a in range(n):
            out.append(copy(a, 0, me, sibling, src=x_refs[a]))
            out += [copy(a, 1 + j, me, (*chip, mc), src=x_refs[a]) for j, chip in enumerate(chips)]
        return out

    def passed():
        return [copy(a, 4 + j, (*chip, mc), sibling) for j, chip in enumerate(chips) for a in range(n)]

    def start():
        for cp in mine() + first():
            cp.start()

    def forward():
        for j, chip in enumerate(chips):
            for a in range(n):
                copy(a, 1 + j, (*chip, mc), me).wait_recv()
        for cp in passed():
            cp.start()

    def finish():
        for a in range(n):
            copy(a, 0, sibling, me).wait_recv()
            for j, chip in enumerate(chips):
                copy(a, 4 + j, (*chip, 1 - mc), me).wait_recv()
        for cp in first() + passed():
            cp.wait_send()
        for cp in mine():
            cp.wait()

    return start, forward, finish


def _exchange_stages(g_refs, land_refs, send_sems, recv_sems, local_sems):
    n = len(g_refs)
    mx, my, mc = lax.axis_index("x"), lax.axis_index("y"), lax.axis_index("c")
    me = 4 * mx + 2 * my + mc

    def mine():
        return [pltpu.make_async_copy(g_refs[a].at[me], land_refs[a].at[me], local_sems.at[a]) for a in range(n)]

    def copies():
        out = []
        for k in range(1, N_DEV):
            px = 1 - mx if k & 4 else mx
            py = 1 - my if k & 2 else my
            pc = 1 - mc if k & 1 else mc
            peer = 4 * px + 2 * py + pc
            for a in range(n):
                sems = dict(send_sem=send_sems.at[7 * a + k - 1], recv_sem=recv_sems.at[7 * a + k - 1],
                            device_id=(px, py, pc), device_id_type=_MESH)
                out.append((pltpu.make_async_remote_copy(src_ref=g_refs[a].at[peer], dst_ref=land_refs[a].at[me], **sems),
                            pltpu.make_async_remote_copy(src_ref=g_refs[a].at[peer], dst_ref=land_refs[a].at[peer], **sems)))
        return out

    def start():
        for cp in mine():
            cp.start()
        for send, _ in copies():
            send.start()

    def finish():
        both = copies()
        for _, recv in both:
            recv.wait_recv()
        for send, _ in both:
            send.wait_send()
        for cp in mine():
            cp.wait()

    return start, lambda: None, finish


def _exchange_parts(side):
    kind, arrays = side
    n = len(arrays)
    if kind == "gather":
        make, shapes = _gather_stages, [jax.ShapeDtypeStruct((N_DEV,) + a.shape, a.dtype) for a in arrays]
    else:
        make, shapes = _exchange_stages, [jax.ShapeDtypeStruct(a.shape, a.dtype) for a in arrays]
    sems = [pltpu.SemaphoreType.DMA((7 * n,)), pltpu.SemaphoreType.DMA((7 * n,)), pltpu.SemaphoreType.DMA((n,))]
    return make, shapes, sems


def _exchange(name, side):
    make, shapes, sems = _exchange_parts(side)
    n = len(shapes)

    def body(*refs):
        start, forward, finish = make(refs[:n], refs[n:2 * n], *refs[2 * n:])
        start()
        forward()
        finish()

    hbm = pl.BlockSpec(memory_space=pl.ANY)
    return list(pl.pallas_call(body, name=name, out_shape=shapes, in_specs=[hbm] * n, out_specs=[hbm] * n,
                               scratch_shapes=sems)(*side[1]))


def _all_gather(name, xs):
    return _exchange(name, ("gather", xs))


def _all_to_all(name, gs):
    return _exchange(name, ("exchange", gs))


def _adamw(name, parts, w, m, v, tr):
    rows, cols = w.shape

    def body(p_ref, w_ref, m_ref, v_ref, g_ref, d_ref, nm_ref, nv_ref):
        g = p_ref[0].astype(F32)
        for d in range(1, N_DEV):
            g = g + p_ref[d].astype(F32)
        nm = ADAM_B1 * m_ref[...] + (1.0 - ADAM_B1) * g
        nv = ADAM_B2 * v_ref[...] + (1.0 - ADAM_B2) * jnp.square(g)
        m_hat = nm / (1.0 - ADAM_B1 ** ADAM_STEP)
        v_hat = nv / (1.0 - ADAM_B2 ** ADAM_STEP)
        g_ref[...] = g
        d_ref[...] = -ADAM_LR * (m_hat / (jnp.sqrt(v_hat) + ADAM_EPS) + ADAM_WD * w_ref[...])
        nm_ref[...] = nm
        nv_ref[...] = nv

    spec = pl.BlockSpec((tr, cols), lambda i: (i, 0))
    return pl.pallas_call(
        body, name=name, grid=(rows // tr,),
        in_specs=[pl.BlockSpec((N_DEV, tr, cols), lambda i: (0, i, 0)), spec, spec, spec],
        out_specs=[spec] * 4, out_shape=[jax.ShapeDtypeStruct((rows, cols), F32)] * 4,
        compiler_params=_params(("parallel",)))(parts, w, m, v)


def _in_pieces():
    p = []
    for h in range(RET_HEADS):
        p += [(128 * h, 128 * h + 128), (512 + 128 * h, 512 + 128 * h + 128), (1024 + 256 * h, 1024 + 256 * h + 256)]
    p += [(2048, 3072), (3776, 4800), (4800, 7872), (3072, 3456), 128, (3456, 3712),
          (3712, 3744), 32, (3744, 3776), 32, 128]
    return p


def _uq_pieces():
    p = []
    for h in range(MLA_HEADS):
        b = 192 * h
        p += [(b, b + 128), (b + 128, b + 160), 32, (b + 160, b + 192), 32]
    return p


def _perm(w, pieces):
    cols = [jnp.zeros(w.shape[:-1] + (p,), w.dtype) if isinstance(p, int) else w[..., p[0]:p[1]] for p in pieces]
    return jnp.concatenate(cols, axis=-1)


def _unperm(dw, pieces):
    found, off = [], 0
    for p in pieces:
        if isinstance(p, int):
            off += p
        else:
            found.append((p[0], dw[..., off:off + p[1] - p[0]]))
            off += p[1] - p[0]
    return jnp.concatenate([t for _, t in sorted(found, key=lambda s: s[0])], axis=-1)


def _unshard(blocks, axis):
    return jnp.concatenate([blocks[p] for p in range(N_DEV)], axis=axis)


def _shard_split(full, axis):
    return jnp.stack(jnp.split(full, N_DEV, axis=axis), axis=0)


def _row_tile(rows, cap=256, unit=16):
    return max(t for t in range(unit, cap + 1, unit) if rows % t == 0)


def _rope_tables(seq):
    pos = jnp.arange(seq, dtype=F32)[:, None]

    def table(dim):
        inv = 1.0 / (ROPE_THETA ** (jnp.arange(0, dim, 2, dtype=F32) / dim))
        ang = pos * inv[None, :]
        return jnp.cos(ang), jnp.sin(ang)

    cr, sr = table(RET_QK)
    cm, sm = table(MLA_ROPE)
    z = jnp.zeros_like(cm)
    return (jnp.concatenate([cr, cr], 1), jnp.concatenate([-sr, sr], 1),
            jnp.concatenate([cm, z, cm, z], 1), jnp.concatenate([-sm, z, sm, z], 1))


def _s5_maps(a_re, a_im, log_dt, b_re, b_im, c_re, c_im):
    dt = jnp.exp(log_dt)[:, None]
    ar = jnp.minimum(a_re, -1e-4)
    mag = jnp.exp(dt * ar)
    abar_re = mag * jnp.cos(dt * a_im)
    abar_im = mag * jnp.sin(dt * a_im)
    den = ar * ar + a_im * a_im
    nr = abar_re - 1.0
    ni = abar_im
    coef_re = (nr * ar + ni * a_im) / den
    coef_im = (ni * ar - nr * a_im) / den
    bb_re = coef_re[..., None] * b_re - coef_im[..., None] * b_im
    bb_im = coef_re[..., None] * b_im + coef_im[..., None] * b_re
    eye = jnp.eye(S5_BLOCKS, dtype=F32)

    def in_blocks(bb):
        t = bb.transpose(0, 2, 1).reshape(S5_BLOCKS, 8, S5_GROUP, S5_STATE)
        return jnp.einsum('jgcp,gh->jgchp', t, eye).reshape(S5_BLOCKS, 128, 512)

    def out_blocks(cc):
        t = cc.transpose(0, 2, 1).reshape(S5_BLOCKS, 8, S5_STATE, S5_GROUP)
        return jnp.einsum('jgpc,gh->jgphc', t, eye).reshape(S5_BLOCKS, 512, 128)

    arow = jnp.concatenate([abar_re.reshape(S5_BLOCKS, 512), abar_im.reshape(S5_BLOCKS, 512)], axis=1).reshape(1, -1)
    b_map = jnp.concatenate([in_blocks(bb_re), in_blocks(bb_im)], axis=2)
    c_map = jnp.concatenate([out_blocks(c_re), -out_blocks(c_im)], axis=1)
    return arow, b_map, c_map


def _power_tables(arow, conj, reverse):
    a = arow.reshape(S5_BLOCKS, 2, 512)
    ar, ai = a[:, 0], (-a[:, 1] if conj else a[:, 1])
    pw = [(ar, ai)]
    for _ in range(SUBLANES - 1):
        pr, pi = pw[-1]
        pw.append((pr * ar - pi * ai, pr * ai + pi * ar))

    def rows(sel):
        return jnp.stack([jnp.stack(list(pw[i]), axis=1) for i in sel], axis=0).reshape(len(sel), -1)

    rowid = jnp.arange(SUBLANES)[:, None]
    ak = jnp.concatenate([jnp.where((rowid < SUBLANES - k) if reverse else (rowid >= k), rows([k - 1]), 0.0)
                          for k in (1, 2, 4)], axis=0)
    order = list(range(SUBLANES))
    apow = rows(order[::-1] if reverse else order)
    return ak, apow


def _rows(arr):
    return (arr, arr.shape[1], 0)


def _vjp_rows(f, n_prim):
    def fn(r, c, _):
        _, vjp = jax.vjp(f, *r[:n_prim])
        return list(vjp(r[n_prim])), []
    return fn


def _norm_bwd(r, c, _):
    _, vjp = jax.vjp(_f_rms, r[0], c[0])
    dx, dg = vjp(r[1])
    return [dx + r[2]], [dg]


def _layer_fwd(l, x, W_first, W_rest, P, T, sides):
    W = dict(W_first)
    n = x.shape[0]
    tm, tmw = _pick(n, 256), _pick(n, 128)
    cos_r, sin_r, cos_m, sin_m = T

    def nm(s):
        return f"l{l}_{s}"

    def one(name, f, rows, consts, width, dtype=BF16, tile=tm):
        return _tile_call(nm(name), lambda r, c, _: ([f(r, c)], []), n, tile, rows, consts, [(width, dtype)])[0]

    h = one("norm1", lambda r, c: _f_rms(r[0], c[0]), [_rows(x)], [P["norm1_g"]], D_MODEL)
    z = _mm(nm("in_proj"), h, W["in"])
    of, stf = _ret_fwd(nm("ret_f"), z, cos_r, sin_r, P["rd"][0], reverse=False)
    ob, stb = _ret_fwd(nm("ret_b"), z, cos_r, sin_r, P["rd"][1], reverse=True)

    def gn(r, c, _):
        yraw = r[0] + r[1]
        ys = [_f_gn_gate(yraw[:, RET_V * i:RET_V * (i + 1)], r[2][:, RET_V * i:RET_V * (i + 1)],
                         c[0][:, RET_V * i:RET_V * (i + 1)]) for i in range(RET_HEADS)]
        return [yraw, jnp.concatenate(ys, axis=1)], []

    yraw, yret = _tile_call(nm("ret_gn"), gn, n, tm, [_rows(of), _rows(ob), (z, 1024, Z_RG // 1024)], [P["ret_gn_g"]],
                            [(1024, F32), (1024, BF16)])

    cqn = one("q_norm", lambda r, c: _f_rms(r[0][:, :MLA_Q_LORA], c[0]), [(z, 512, Z_CQ // 512)], [P["mla_q_norm_g"]], MLA_Q_LORA)
    ckvn = one("kv_norm", lambda r, c: _f_rms(r[0], c[0]), [(z, 256, Z_CKV // 256)], [P["mla_kv_norm_g"]], MLA_KV_LORA)
    qraw = _mm(nm("q_up"), cqn, W["uq"])
    kv = _mm(nm("kv_up"), ckvn, W["ukv"], out_dtype=BF16)
    q = one("q_rope", lambda r, c: _mla_q(r[0], r[1], r[2], False), [_rows(qraw), _rows(cos_m), _rows(sin_m)], [], 2048)
    kr = one("k_rope", lambda r, c: _rope(r[0], r[1], r[2]), [(z, 128, Z_KR // 128), _rows(cos_m), _rows(sin_m)], [], 128)
    o, lse = _attn_fwd(nm("attn"), q, kv, kr)

    xs, y_dir, side_res = [], [], {}
    for d in range(2):
        x_d, y_d, side_res[d] = _s5_fwd(nm(f"s5_scan{d}"), z, P["s5"][d], reverse=(d == 1), side=sides.get(d))
        xs.append(x_d)
        y_dir.append(y_d)
    W.update(W_rest(side_res))

    def s5_act(r, c, _):
        ysum = r[0] + r[1]
        return [ysum, _f_s5_act(ysum, r[2], c[0])], []

    ysum, gact = _tile_call(nm("s5_act"), s5_act, n, tm, [_rows(y_dir[0]), _rows(y_dir[1]), (z, 1024, Z_U // 1024)], [P["s5_d"]],
                            [(1024, F32), (1024, BF16)])
    ga = _mm(nm("glu_a"), gact, W["glu_a"])
    gb = _mm(nm("glu_b"), gact, W["glu_b"])
    ys5 = one("glu", lambda r, c: _f_glu(r[0], r[1]), [_rows(ga), _rows(gb)], [], 1024)

    ys = (yret, o, ys5)
    ps = [_mm(nm(f"branch{i}"), ys[i], W["branch"][i]) for i in range(3)]
    mix = one("mix", lambda r, c: _f_gate(r[0], r[3]) + _f_gate(r[1], r[4]) + _f_gate(r[2], r[5]),
              [(z, 1024, Z_GATE // 1024 + i) for i in range(3)] + [_rows(p) for p in ps], [], 1024)
    x1 = _mm(nm("out_proj"), mix, W["out"], add=x)
    h2 = one("norm2", lambda r, c: _f_rms(r[0], c[0]), [_rows(x1)], [P["norm2_g"]], D_MODEL)
    gp = _mm(nm("ffn_g"), h2, W["ffn_g"])
    up = _mm(nm("ffn_u"), h2, W["ffn_u"])
    act = one("swiglu", lambda r, c: _f_swiglu(r[0], r[1]), [_rows(gp), _rows(up)], [], FFN_HIDDEN, tile=tmw)
    x2 = _mm(nm("ffn_down"), act, W["ffn_down"], add=x1)
    saved = dict(x=x, h=h, z=z, stf=stf, stb=stb, yraw=yraw, ys=ys, cqn=cqn, ckvn=ckvn, qraw=qraw, q=q, kv=kv, kr=kr,
                 lse=lse, xs=xs, ysum=ysum, gact=gact, ga=ga, gb=gb, ps=ps, mix=mix, x1=x1, h2=h2, gp=gp, up=up, act=act, W=W)
    return x2, saved, side_res


def _layer_bwd(l, dx2, sv, P, T, side_of):
    n = dx2.shape[0]
    tm, tmw = _pick(n, 256), _pick(n, 128)
    cos_r, sin_r, cos_m, sin_m = T
    z, W = sv["z"], sv["W"]
    g = {}

    def nm(s):
        return f"l{l}_{s}"

    dact = _mm(nm("d_act"), dx2, W["ffn_down"], tb=True)
    g["ffn_down"] = _mm(nm("dw_ffn_down"), sv["act"], dx2, ta=True)
    dgp, dup = _tile_call(nm("d_swiglu"), _vjp_rows(_f_swiglu, 2), n, tmw, [_rows(sv["gp"]), _rows(sv["up"]), _rows(dact)], [],
                          [(FFN_HIDDEN, BF16)] * 2)
    dh2 = _mm(nm("d_h2_g"), dgp, W["ffn_g"], tb=True)
    dh2 = _mm(nm("d_h2_u"), dup, W["ffn_u"], tb=True, add=dh2)
    g["ffn_g"] = _mm(nm("dw_ffn_g"), sv["h2"], dgp, ta=True)
    g["ffn_u"] = _mm(nm("dw_ffn_u"), sv["h2"], dup, ta=True)
    dx1, g["norm2_g"] = _tile_call(nm("d_norm2"), _norm_bwd, n, tm, [_rows(sv["x1"]), _rows(dh2), _rows(dx2)], [P["norm2_g"]],
                                   [(D_MODEL, F32)], acc_outs=[(1, D_MODEL)])

    dmix = _mm(nm("d_mix"), dx1, W["out"], tb=True)
    g["out"] = _mm(nm("dw_out"), sv["mix"], dx1, ta=True)
    dz = jnp.zeros((n, ZW), BF16)
    dys, g["branch"] = [], []
    for i in range(3):
        dz, dp = _tile_call(nm(f"d_gate{i}"), _vjp_rows(_f_gate, 2), n, tm,
                            [(z, 1024, Z_GATE // 1024 + i), _rows(sv["ps"][i]), _rows(dmix)], [], [(1024, BF16)],
                            alias=(dz, 1024, Z_GATE // 1024 + i))
        dys.append(_mm(nm(f"d_branch{i}"), dp, W["branch"][i], tb=True))
        g["branch"].append(_mm(nm(f"dw_branch{i}"), sv["ys"][i], dp, ta=True))

    dga, dgb = _tile_call(nm("d_glu"), _vjp_rows(_f_glu, 2), n, tm, [_rows(sv["ga"]), _rows(sv["gb"]), _rows(dys[2])], [],
                          [(1024, BF16)] * 2)
    dgact = _mm(nm("d_gact_a"), dga, W["glu_a"], tb=True)
    dgact = _mm(nm("d_gact_b"), dgb, W["glu_b"], tb=True, add=dgact)
    g["glu_a"] = _mm(nm("dw_glu_a"), sv["gact"], dga, ta=True)
    g["glu_b"] = _mm(nm("dw_glu_b"), sv["gact"], dgb, ta=True)

    def act_bwd(r, c, _):
        _, vjp = jax.vjp(_f_s5_act, r[0], r[1], c[0])
        dy, du, dd = vjp(r[2])
        return [dy, du], [dd]

    dysum, du_part, g["s5_d"] = _tile_call(nm("d_s5_act"), act_bwd, n, tm,
                                           [_rows(sv["ysum"]), (z, 1024, Z_U // 1024), _rows(dgact)], [P["s5_d"]],
                                           [(1024, F32)] * 2, acc_outs=[(1, 1024)])
    dus, g["s5"] = [], []
    side_res = {}
    for d in range(2):
        du, g_b, g_c, da, side_res[d] = _s5_bwd(nm(f"d_s5_scan{d}"), z, dysum, sv["xs"][d], P["s5"][d], reverse=(d == 0),
                                                side=side_of(d, g))
        dus.append(du)
        g["s5"].append((da[:1], g_b, g_c))
    dz, = _tile_call(nm("d_s5_u"), lambda r, c, _: ([r[0] + r[1] + r[2]], []), n, tm,
                     [_rows(du_part), _rows(dus[0]), _rows(dus[1])], [], [], alias=(dz, 1024, Z_U // 1024))

    o = sv["ys"][1]
    dq, dkv, dkr = _attn_bwd(nm("d_attn"), sv["q"], sv["kv"], sv["kr"], o, sv["lse"], dys[1])
    dqraw, = _tile_call(nm("d_q_rope"), lambda r, c, _: ([_mla_q(r[0], r[1], r[2], True)], []), n, tm,
                        [_rows(dq), _rows(cos_m), _rows(sin_m)], [], [(2048, BF16)])

    def kr_bwd(r, c, _):
        tot = r[0][:, :128]
        for h in range(1, MLA_HEADS):
            tot = tot + r[0][:, 128 * h:128 * (h + 1)]
        return [_rope_t(tot, r[1], r[2])], []

    dz, = _tile_call(nm("d_k_rope"), kr_bwd, n, tm, [_rows(dkr), _rows(cos_m), _rows(sin_m)], [], [],
                     alias=(dz, 128, Z_KR // 128))
    dcqn = _mm(nm("d_cqn"), dqraw, W["uq"], tb=True)
    g["uq"] = _mm(nm("dw_uq"), sv["cqn"], dqraw, ta=True)
    dckvn = _mm(nm("d_ckvn"), dkv, W["ukv"], tb=True)
    g["ukv"] = _mm(nm("dw_ukv"), sv["ckvn"], dkv, ta=True)

    def qn_bwd(r, c, _):
        _, vjp = jax.vjp(_f_rms, r[0][:, :MLA_Q_LORA], c[0])
        da, dg = vjp(r[1])
        return [jnp.concatenate([da, jnp.zeros((da.shape[0], 512 - MLA_Q_LORA), F32)], axis=1)], [dg]

    dz, g["mla_q_norm_g"] = _tile_call(nm("d_q_norm"), qn_bwd, n, tm, [(z, 512, Z_CQ // 512), _rows(dcqn)], [P["mla_q_norm_g"]],
                                       [], acc_outs=[(1, MLA_Q_LORA)], alias=(dz, 512, Z_CQ // 512))

    def kvn_bwd(r, c, _):
        _, vjp = jax.vjp(_f_rms, r[0], c[0])
        da, dg = vjp(r[1])
        return [da], [dg]

    dz, g["mla_kv_norm_g"] = _tile_call(nm("d_kv_norm"), kvn_bwd, n, tm, [(z, 256, Z_CKV // 256), _rows(dckvn)],
                                        [P["mla_kv_norm_g"]], [], acc_outs=[(1, MLA_KV_LORA)], alias=(dz, 256, Z_CKV // 256))

    def gn_bwd(r, c, _):
        drg, dy, dg = [], [], []
        for i in range(RET_HEADS):
            sl = slice(RET_V * i, RET_V * (i + 1))
            _, vjp = jax.vjp(_f_gn_gate, r[0][:, sl], r[1][:, sl], c[0][:, sl])
            a, b, e = vjp(r[2][:, sl])
            dy.append(a)
            drg.append(b)
            dg.append(e)
        return [jnp.concatenate(drg, axis=1), jnp.concatenate(dy, axis=1)], [jnp.concatenate(dg, axis=1)]

    dz, dyraw, g["ret_gn_g"] = _tile_call(nm("d_ret_gn"), gn_bwd, n, tm,
                                          [_rows(sv["yraw"]), (z, 1024, Z_RG // 1024), _rows(dys[0])], [P["ret_gn_g"]],
                                          [(1024, F32)], acc_outs=[(1, 1024)], alias=(dz, 1024, Z_RG // 1024))
    dz, drd_f = _ret_bwd(nm("d_ret_f"), z, cos_r, sin_r, P["rd"][0], sv["stf"], dyraw, dz, reverse=False)
    dz, drd_b = _ret_bwd(nm("d_ret_b"), z, cos_r, sin_r, P["rd"][1], sv["stb"], dyraw, dz, reverse=True)
    g["ret_decay"] = jnp.stack([drd_f[:, 0, 0], drd_b[:, 0, 0]], axis=0)

    dh = _mm(nm("d_h"), dz, W["in"], tb=True)
    g["in"] = _mm(nm("dw_in"), sv["h"], dz, ta=True)
    dx, g["norm1_g"] = _tile_call(nm("d_norm1"), _norm_bwd, n, tm, [_rows(sv["x"]), _rows(dh), _rows(dx1)], [P["norm1_g"]],
                                  [(D_MODEL, F32)], acc_outs=[(1, D_MODEL)])
    return dx, g, side_res


INPUT_NAMES = ("x",) + WEIGHTS + ("loss_target",) + tuple("m_" + n for n in WEIGHTS) + tuple("v_" + n for n in WEIGHTS)
S5_NAMES = ("s5_a_re", "s5_a_im", "s5_log_dt", "s5_b_re", "s5_b_im", "s5_c_re", "s5_c_im")


def _pack_rows(arrays, tile_rows):
    pieces = []
    for a in arrays:
        flat = a.reshape(-1)
        pad = -flat.shape[0] % LANES
        if pad:
            flat = jnp.concatenate([flat, jnp.zeros((pad,), flat.dtype)])
        pieces.append(flat.reshape(-1, LANES))
    pad = -sum(p.shape[0] for p in pieces) % tile_rows
    if pad:
        pieces.append(jnp.zeros((pad, LANES), pieces[0].dtype))
    return jnp.concatenate(pieces, axis=0)


def _unpack_rows(packed, shapes):
    out, r0 = [], 0
    for s in shapes:
        size = math.prod(s)
        rows = -(-size // LANES)
        out.append(packed[r0:r0 + rows].reshape(-1)[:size].reshape(s))
        r0 += rows
    return out


FIRST = ("w_in", "mla_w_uq", "mla_w_ukv")
REST = ("s5_w_glu", "w_branch", "w_out", "ffn_w_gu", "ffn_w_down")


def _weights_first(full):
    return {"in": _perm(full["w_in"], _in_pieces()), "uq": _perm(full["mla_w_uq"], _uq_pieces()), "ukv": full["mla_w_ukv"]}


def _weights_rest(full):
    return dict(glu_a=full["s5_w_glu"][:, :1024], glu_b=full["s5_w_glu"][:, 1024:],
                branch=[full["w_branch"][i] for i in range(3)], out=full["w_out"],
                ffn_g=full["ffn_w_gu"][:, :FFN_HIDDEN], ffn_u=full["ffn_w_gu"][:, FFN_HIDDEN:], ffn_down=full["ffn_w_down"])


def _grads_first(g):
    return dict(w_in=_unperm(g["in"], _in_pieces()), mla_w_uq=_unperm(g["uq"], _uq_pieces()), mla_w_ukv=g["ukv"])


def _grads_rest(g):
    return dict(s5_w_glu=jnp.concatenate([g["glu_a"], g["glu_b"]], axis=1), w_branch=jnp.stack(g["branch"], axis=0),
                w_out=g["out"], ffn_w_gu=jnp.concatenate([g["ffn_g"], g["ffn_u"]], axis=1), ffn_w_down=g["ffn_down"])


def _local_step(inp, x, target, plan):
    n = x.shape[0]
    tables = _rope_tables(n)
    Ps, s5_vjps = [], []
    for l in range(DEPTH):
        s5, vjps = [], []
        for d in range(2):
            (arow, b_map, c_map), vjp = jax.vjp(_s5_maps, *[inp[k][l, d] for k in S5_NAMES])
            arow = lax.stop_gradient(arow)
            ak, apow = _power_tables(arow, False, d == 1)
            ak_adj, apow_adj = _power_tables(arow, True, d == 0)
            s5.append(dict(b_map=b_map, c_map=c_map, b_map_t=b_map.transpose(0, 2, 1), c_map_t=c_map.transpose(0, 2, 1),
                           ak=ak, apow=apow, ak_adj=ak_adj, apow_adj=apow_adj))
            vjps.append(vjp)
        s5_vjps.append(vjps)
        Ps.append(dict(
            norm1_g=inp["norm1_g"][l][None], norm2_g=inp["norm2_g"][l][None], ret_gn_g=inp["ret_gn_g"][l][None],
            mla_q_norm_g=inp["mla_q_norm_g"][l][None], mla_kv_norm_g=inp["mla_kv_norm_g"][l][None], s5_d=inp["s5_d"][l][None],
            rd=[jnp.broadcast_to(inp["ret_decay"][l, d][:, None, None], (RET_HEADS, 1, LANES)) for d in range(2)], s5=s5))

    h, saved, fwd_res = x, [], {}
    for l in range(DEPTH):
        sides = {d: plan["fwd_side"](l, d) for d in range(2)}

        def rest(res, l=l):
            fwd_res.update({(l, d): r for d, r in res.items()})
            return _weights_rest(plan["rest"](l, fwd_res))

        h, sv, _ = _layer_fwd(l, h, _weights_first(plan["first"](l, fwd_res)), rest, Ps[l], tables,
                              {d: s for d, s in sides.items() if s is not None})
        saved.append(sv)

    def loss_bwd(r, c, _):
        loss, vjp = jax.vjp(lambda a, gain: _f_loss(a, gain, r[1]), r[0], c[0])
        da, dg = vjp(jnp.ones((), F32))
        return [da], [dg, jnp.broadcast_to(loss, (1, LANES))]

    dh, g_final, loss_row = _tile_call("loss", loss_bwd, n, _pick(n, 256), [_rows(h), _rows(target)], [inp["final_g"][None]],
                                       [(D_MODEL, F32)], acc_outs=[(1, D_MODEL), (1, LANES)])
    layer_g, wgrads, bwd_res = [None] * DEPTH, [None] * DEPTH, {}
    for l in reversed(range(DEPTH)):
        dh, layer_g[l], res = _layer_bwd(l, dh, saved[l], Ps[l], tables, lambda d, g, l=l: plan["bwd_side"](l, d, g, wgrads))
        wgrads[l] = {**_grads_first(layer_g[l]), **_grads_rest(layer_g[l])}
        bwd_res.update({(l, d): r for d, r in res.items()})

    def stack(f):
        return jnp.stack([f(layer_g[l], l) for l in range(DEPTH)], axis=0)

    grads = dict(
        **{k: jnp.stack([wgrads[l][k] for l in range(DEPTH)], axis=0) for k in SHARDED},
        norm1_g=stack(lambda g, l: g["norm1_g"][0]), norm2_g=stack(lambda g, l: g["norm2_g"][0]),
        ret_gn_g=stack(lambda g, l: g["ret_gn_g"][0]), mla_q_norm_g=stack(lambda g, l: g["mla_q_norm_g"][0]),
        mla_kv_norm_g=stack(lambda g, l: g["mla_kv_norm_g"][0]), s5_d=stack(lambda g, l: g["s5_d"][0]),
        ret_decay=stack(lambda g, l: g["ret_decay"]), final_g=g_final[0])
    s5_grads = [[s5_vjps[l][d](layer_g[l]["s5"][d]) for d in range(2)] for l in range(DEPTH)]
    for i, k in enumerate(S5_NAMES):
        grads[k] = jnp.stack([jnp.stack([s5_grads[l][d][i] for d in range(2)], axis=0) for l in range(DEPTH)], axis=0)
    return loss_row[0, 0], dh, grads, wgrads, bwd_res


def kernel(*args):
    inp = dict(zip(INPUT_NAMES, args))
    kinds = ("grad_", "delta_", "new_m_", "new_v_")

    def local(l, names):
        return [inp[k][l].astype(BF16) for k in names]

    def whole(names, gathered):
        return {k: _unshard(g, SHARD_AXIS[k] - 1) for k, g in zip(names, gathered)}

    def parts(names, wg):
        return ("exchange", [_shard_split(wg[k], SHARD_AXIS[k] - 1).astype(BF16) for k in names])

    first0 = _all_gather("gather_weights0", local(0, FIRST))
    fwd_sides = {(0, 0): ("gather", local(0, REST)), (0, 1): ("gather", local(1, FIRST + REST))}

    def bwd_side(l, d, g, wgrads):
        if (l, d) == (0, 0):
            return parts(FIRST, wgrads[1])
        return parts(REST, _grads_rest(g)) if (l, d) in ((1, 0), (0, 1)) else None

    plan = dict(
        first=lambda l, res: whole(FIRST, first0 if l == 0 else res[(0, 1)][:len(FIRST)]),
        rest=lambda l, res: whole(REST, res[(0, 0)] if l == 0 else res[(0, 1)][len(FIRST):]),
        fwd_side=lambda l, d: fwd_sides.get((l, d)), bwd_side=bwd_side)

    loss, dh, grads, wgrads, landed = _local_step(inp, inp["x"][0], inp["loss_target"][0], plan)
    loss = lax.psum(loss, ("x", "y", "c"))

    landed[(0, 2)] = _exchange("exchange_grads0", parts(FIRST, wgrads[0]))
    by_layer = [dict(zip(FIRST + REST, landed[(0, 2)] + landed[(0, 1)])), dict(zip(FIRST + REST, landed[(0, 0)] + landed[(1, 0)]))]
    out = {}
    for k in SHARDED:
        land = jnp.stack([by_layer[0][k], by_layer[1][k]], axis=1)
        shape = inp[k].shape
        rows, cols = math.prod(shape[:-1]), shape[-1]
        res = _adamw("adamw_" + k, land.reshape(N_DEV, rows, cols), *[inp[p + k].reshape(rows, cols) for p in ("", "m_", "v_")],
                     _row_tile(rows))
        for kind, t in zip(kinds, res):
            out[kind + k] = t.reshape(shape)

    small_tr = 512
    partial = _pack_rows([grads[k] for k in SMALL], small_tr)
    packed = [_pack_rows([inp[p + k] for k in SMALL], small_tr) for p in ("", "m_", "v_")]
    res_small = _adamw("adamw_small", _all_gather("gather_small_grads", [partial])[0], *packed, small_tr)
    for kind, b in zip(kinds, res_small):
        for k, t in zip(SMALL, _unpack_rows(b, [inp[k].shape for k in SMALL])):
            out[kind + k] = t
    return (loss, dh[None]) + tuple(out[kind + k] for kind in kinds for k in WEIGHTS)
```

```python
import functools
import math

import jax
import jax.numpy as jnp
from jax import lax
from jax.experimental import pallas as pl
from jax.experimental.pallas import tpu as pltpu

F32 = jnp.float32
BF16 = jnp.bfloat16

D_MODEL = 1024
DEPTH = 2
RMS_EPS = 1e-6
GN_EPS = 1e-5
ROPE_THETA = 10000.0
RET_HEADS, RET_QK, RET_V, RET_CHUNK = 4, 128, 256, 128
MLA_HEADS, MLA_Q_LORA, MLA_KV_LORA, MLA_NOPE, MLA_ROPE, MLA_V = 8, 384, 256, 128, 64, 128
S5_GROUPS, S5_GROUP, S5_STATE = 64, 16, 64
S5_BLOCKS = 8
FFN_HIDDEN = 2816
N_DEV = 8
ADAM_LR, ADAM_B1, ADAM_B2, ADAM_EPS, ADAM_WD, ADAM_STEP = 0.001, 0.9, 0.999, 1e-08, 0.01, 10

LANES = 128
SUBLANES = 8
VMEM_LIMIT = 48 * 1024 * 1024

ZW = 8192
Z_RET = 0
Z_RG = 2048
Z_U = 3072
Z_GATE = 4096
Z_CQ = 7168
Z_CKV = 7680
Z_KR = 7936
IN_SPLITS = (512, 512, 1024, 1024, 384, 256, 64, 1024, 3072)

SHARDED = ("w_in", "mla_w_uq", "mla_w_ukv", "s5_w_glu", "w_branch", "w_out", "ffn_w_gu", "ffn_w_down")
SHARD_AXIS = {"w_in": 2, "mla_w_uq": 2, "mla_w_ukv": 2, "s5_w_glu": 2, "w_branch": 2, "w_out": 1,
              "ffn_w_gu": 2, "ffn_w_down": 1}
SMALL = ("norm1_g", "ret_decay", "ret_gn_g", "mla_q_norm_g", "mla_kv_norm_g", "s5_a_re", "s5_a_im", "s5_log_dt",
         "s5_b_re", "s5_b_im", "s5_c_re", "s5_c_im", "s5_d", "norm2_g", "final_g")
WEIGHTS = ("norm1_g", "w_in", "ret_decay", "ret_gn_g", "mla_q_norm_g", "mla_w_uq", "mla_kv_norm_g", "mla_w_ukv",
           "s5_a_re", "s5_a_im", "s5_log_dt", "s5_b_re", "s5_b_im", "s5_c_re", "s5_c_im", "s5_d", "s5_w_glu",
           "w_branch", "w_out", "norm2_g", "ffn_w_gu", "ffn_w_down", "final_g")


def _params(sem=None):
    return pltpu.CompilerParams(dimension_semantics=sem, vmem_limit_bytes=VMEM_LIMIT)


def _pick(n, cap):
    if n <= cap:
        return n
    t = cap - cap % LANES
    while t >= LANES:
        if n % t == 0:
            return t
        t -= LANES
    return n


@functools.partial(jax.custom_vjp, nondiff_argnums=(2, 3))
def _bdot(a, b, ca, cb):
    return lax.dot_general(a.astype(BF16), b.astype(BF16), (((ca,), (cb,)), ((), ())), preferred_element_type=F32)


def _bdot_fwd(a, b, ca, cb):
    return _bdot(a, b, ca, cb), (a, b)


def _bdot_bwd(ca, cb, res, g):
    a, b = res
    da = _bdot(g, b, 1, 1 - cb) if ca == 1 else _bdot(b, g, 1 - cb, 1)
    db = _bdot(a, g, 1 - ca, 0) if cb == 0 else _bdot(g, a, 0, 1 - ca)
    return da, db


_bdot.defvjp(_bdot_fwd, _bdot_bwd)


@jax.custom_vjp
def _swap_halves(x):
    return pltpu.roll(x, LANES // 2, 1)


_swap_halves.defvjp(lambda x: (_swap_halves(x), None), lambda _, g: (_swap_halves(g),))


def _rope(x, cosf, sinf):
    return x * cosf + _swap_halves(x) * sinf


def _f_rms(x, g):
    return x * lax.rsqrt(jnp.mean(x * x, axis=-1, keepdims=True) + RMS_EPS) * g


def _rope_t(g, cosf, sinf):
    return g * cosf + _swap_halves(g * sinf)


def _f_gn_gate(yh, rgh, gh):
    mu = jnp.mean(yh, axis=-1, keepdims=True)
    var = jnp.mean(jnp.square(yh - mu), axis=-1, keepdims=True)
    return jax.nn.silu(rgh) * ((yh - mu) * lax.rsqrt(var + GN_EPS) * gh)


MLA_SCALE = (MLA_NOPE + MLA_ROPE) ** -0.5


def _mla_q(qraw, cosf, sinf, transpose):
    parts = []
    for h in range(MLA_HEADS):
        parts.append(qraw[:, 256 * h:256 * h + 128] * MLA_SCALE)
        r = qraw[:, 256 * h + 128:256 * h + 256]
        parts.append(_rope_t(r * MLA_SCALE, cosf, sinf) if transpose else _rope(r, cosf, sinf) * MLA_SCALE)
    return jnp.concatenate(parts, axis=1)


def _f_s5_act(ysum, u, d):
    return jax.nn.gelu(ysum + d * u)


def _f_glu(ga, gb):
    return ga * jax.nn.sigmoid(gb)


def _f_gate(zg, proj):
    return jax.nn.sigmoid(zg) * proj


def _f_swiglu(gp, up):
    return jax.nn.silu(gp) * up


def _f_loss(x, g, target):
    y = _f_rms(x, g)
    err = jnp.square(y - target)
    return 0.5 * jnp.sum(jnp.mean(err, axis=-1))


def _tile_call(name, fn, n_rows, tm, row_ins, consts, row_outs, acc_outs=(), alias=None):
    n_row_in, n_const = len(row_ins), len(consts)
    args = [a for a, _, _ in row_ins] + list(consts)
    in_specs = [pl.BlockSpec((tm, w), lambda i, cb=cb: (i, cb)) for _, w, cb in row_ins]
    in_specs += [pl.BlockSpec(c.shape, lambda i: (0, 0)) for c in consts]
    out_shape, out_specs, aliases = [], [], {}
    if alias is not None:
        arr, w, cb = alias
        in_specs.append(pl.BlockSpec((tm, w), lambda i, cb=cb: (i, cb)))
        aliases[len(args)] = 0
        args.append(arr)
        out_shape.append(jax.ShapeDtypeStruct(arr.shape, arr.dtype))
        out_specs.append(pl.BlockSpec((tm, w), lambda i, cb=cb: (i, cb)))
    for w, dt in row_outs:
        out_shape.append(jax.ShapeDtypeStruct((n_rows, w), dt))
        out_specs.append(pl.BlockSpec((tm, w), lambda i: (i, 0)))
    n_row_out = len(out_shape)
    for r, w in acc_outs:
        out_shape.append(jax.ShapeDtypeStruct((r, w), F32))
        out_specs.append(pl.BlockSpec((r, w), lambda i: (0, 0)))
    n_in = len(args)

    def body(*refs):
        ins, outs = refs[:n_in], refs[n_in:]
        rows = [r[...] for r in ins[:n_row_in]]
        cons = [r[...] for r in ins[n_row_in:n_row_in + n_const]]
        prev = ins[-1][...] if alias is not None else None
        res_rows, res_accs = fn(rows, cons, prev)
        for o, r in zip(outs[:n_row_out], res_rows):
            o[...] = r.astype(o.dtype)
        if acc_outs:
            @pl.when(pl.program_id(0) == 0)
            def _():
                for o in outs[n_row_out:]:
                    o[...] = jnp.zeros(o.shape, F32)
            for o, r in zip(outs[n_row_out:], res_accs):
                o[...] += r

    return pl.pallas_call(
        body, name=name, grid=(n_rows // tm,), in_specs=in_specs, out_specs=out_specs, out_shape=out_shape,
        input_output_aliases=aliases, compiler_params=_params(("arbitrary",)))(*args)


def _mm(name, a, b, *, ta=False, tb=False, add=None, out_dtype=F32):
    (K, M) = a.shape if ta else a.shape[::-1]
    (N, K2) = b.shape if tb else b.shape[::-1]
    assert K == K2, (name, a.shape, b.shape)
    tm, tn, tk = _pick(M, 1536), _pick(N, 1536), _pick(K, 1024)
    gi, gj, nk = M // tm, N // tn, K // tk
    a_bytes, b_bytes = a.size * a.dtype.itemsize, b.size * b.dtype.itemsize
    j_outer = nk == 1 and b_bytes + a_bytes * gj < a_bytes + b_bytes * gi

    def im(f):
        return (lambda g0, g1, k: f(g1, g0, k)) if j_outer else f

    a_spec = pl.BlockSpec((tk, tm), im(lambda i, j, k: (k, i))) if ta else pl.BlockSpec((tm, tk), im(lambda i, j, k: (i, k)))
    b_spec = pl.BlockSpec((tn, tk), im(lambda i, j, k: (j, k))) if tb else pl.BlockSpec((tk, tn), im(lambda i, j, k: (k, j)))
    o_spec = pl.BlockSpec((tm, tn), im(lambda i, j, k: (i, j)))
    dn = (((0 if ta else 1,), (1 if tb else 0,)), ((), ()))
    has_add = add is not None

    def body(*refs):
        if has_add:
            a_ref, b_ref, add_ref, o_ref, acc = refs
        else:
            a_ref, b_ref, o_ref, acc = refs
        k = pl.program_id(2)

        @pl.when(k == 0)
        def _():
            acc[...] = jnp.zeros(acc.shape, F32)

        acc[...] += lax.dot_general(a_ref[...].astype(BF16), b_ref[...].astype(BF16), dn, preferred_element_type=F32)

        @pl.when(k == nk - 1)
        def _():
            r = acc[...]
            if has_add:
                r = r + add_ref[...]
            o_ref[...] = r.astype(out_dtype)

    args, specs = [a, b], [a_spec, b_spec]
    if has_add:
        args.append(add)
        specs.append(o_spec)
    return pl.pallas_call(
        body, name=name, grid=(gj, gi, nk) if j_outer else (gi, gj, nk), in_specs=specs, out_specs=o_spec,
        out_shape=jax.ShapeDtypeStruct((M, N), out_dtype), scratch_shapes=[pltpu.VMEM((tm, tn), F32)],
        compiler_params=_params(("parallel", "parallel", "arbitrary")))(*args)


S5_BW = 2 * S5_STATE * (S5_GROUPS // S5_BLOCKS)


def _scan_tile(buf, ak_ref, ap_ref, carry, *, reverse, x_ref=None, da_ref=None):
    tt, bw = buf.shape
    hw = bw // 2
    ng = tt // SUBLANES
    rowid = lax.broadcasted_iota(jnp.int32, (SUBLANES, hw), 0)
    steps = [(SUBLANES - k if reverse else k, SUBLANES * n) for n, k in enumerate((1, 2, 4))]
    apr, api = ap_ref[:, :hw], ap_ref[:, hw:]
    first = (rowid == SUBLANES - 1) if reverse else (rowid == 0)

    def group(gi, c):
        cr, ci = c
        r0 = pl.multiple_of(((ng - 1 - gi) if reverse else gi) * SUBLANES, SUBLANES)
        xr, xi = buf[pl.ds(r0, SUBLANES), :hw], buf[pl.ds(r0, SUBLANES), hw:]
        for sh, a0 in steps:
            kr, ki = ak_ref[a0:a0 + SUBLANES, :hw], ak_ref[a0:a0 + SUBLANES, hw:]
            sr, si = pltpu.roll(xr, sh, 0), pltpu.roll(xi, sh, 0)
            xr, xi = xr + kr * sr - ki * si, xi + kr * si + ki * sr
        xr, xi = xr + apr * cr - api * ci, xi + apr * ci + api * cr
        buf[pl.ds(r0, SUBLANES), :hw] = xr
        buf[pl.ds(r0, SUBLANES), hw:] = xi
        if x_ref is not None:
            sh1 = SUBLANES - 1 if reverse else 1
            pr = jnp.where(first, cr, pltpu.roll(xr, sh1, 0))
            pi = jnp.where(first, ci, pltpu.roll(xi, sh1, 0))
            sr, si = x_ref[pl.ds(r0, SUBLANES), :hw], x_ref[pl.ds(r0, SUBLANES), hw:]
            da_ref[:, :hw] += pr * sr + pi * si
            da_ref[:, hw:] += pi * sr - pr * si
        last = 0 if reverse else SUBLANES - 1
        return (jnp.broadcast_to(xr[last:last + 1], (SUBLANES, hw)), jnp.broadcast_to(xi[last:last + 1], (SUBLANES, hw)))

    cr, ci = lax.fori_loop(0, ng, group, (carry[:, :hw], carry[:, hw:]))
    carry[:, :hw] = cr
    carry[:, hw:] = ci


def _s5_specs(n_rows, reverse):
    tt = _pick(n_rows, 1024)
    nt = n_rows // tt

    def rows(width, off):
        return pl.BlockSpec((tt, width), lambda j, t: ((nt - 1 - t) if reverse else t, off + j))

    def per_block(r, c):
        return pl.BlockSpec((None, r, c), lambda j, t: (j, 0, 0))

    def par(r):
        return pl.BlockSpec((r, S5_BW), lambda j, t: (0, j))

    return tt, nt, rows, per_block, par


def _s5_call(name, core, args, in_specs, out_specs, out_shape, scratch, nt, side):
    n_out = len(out_shape)
    if side is None:
        body, sem = core, ("parallel", "arbitrary")
    else:
        make, shapes, sems = _exchange_parts(side)
        n, n_in, n_sc = len(shapes), len(args), len(scratch)
        hbm = pl.BlockSpec(memory_space=pl.ANY)

        def body(*refs):
            ins, xs = refs[:n_in], refs[n_in:n_in + n]
            outs, lands = refs[n_in + n:n_in + n + n_out], refs[n_in + n + n_out:n_in + 2 * n + n_out]
            sc = refs[n_in + 2 * n + n_out:n_in + 2 * n + n_out + n_sc]
            start, forward, finish = make(xs, lands, *refs[-3:])
            j, t = pl.program_id(0), pl.program_id(1)
            pl.when((j == 0) & (t == 0))(start)
            pl.when((j == S5_BLOCKS // 2) & (t == 0))(forward)
            core(*ins, *outs, *sc)
            pl.when((j == S5_BLOCKS - 1) & (t == nt - 1))(finish)

        args, in_specs = list(args) + list(side[1]), list(in_specs) + [hbm] * n
        out_specs, out_shape = list(out_specs) + [hbm] * n, list(out_shape) + shapes
        scratch, sem = list(scratch) + sems, ("arbitrary", "arbitrary")
    res = pl.pallas_call(body, name=name, grid=(S5_BLOCKS, nt), in_specs=in_specs, out_specs=out_specs, out_shape=out_shape,
                         scratch_shapes=scratch, compiler_params=_params(sem))(*args)
    return list(res[:n_out]) + [None if side is None else list(res[n_out:])]


def _s5_fwd(name, z, m, *, reverse, side=None):
    n_rows = z.shape[0]
    tt, nt, rows, per_block, par = _s5_specs(n_rows, reverse)

    def body(u_ref, b_ref, c_ref, ak_ref, ap_ref, x_ref, y_ref, carry):
        @pl.when(pl.program_id(1) == 0)
        def _():
            carry[...] = jnp.zeros(carry.shape, F32)

        x_ref[...] = jnp.dot(u_ref[...].astype(BF16), b_ref[...].astype(BF16), preferred_element_type=F32)
        _scan_tile(x_ref, ak_ref, ap_ref, carry, reverse=reverse)
        y_ref[...] = jnp.dot(x_ref[...].astype(BF16), c_ref[...].astype(BF16), preferred_element_type=F32)

    return _s5_call(
        name, body, [z, m["b_map"], m["c_map"], m["ak"], m["apow"]],
        [rows(LANES, Z_U // LANES), per_block(LANES, S5_BW), per_block(S5_BW, LANES), par(3 * SUBLANES), par(SUBLANES)],
        [rows(S5_BW, 0), rows(LANES, 0)],
        [jax.ShapeDtypeStruct((n_rows, S5_BLOCKS * S5_BW), F32), jax.ShapeDtypeStruct((n_rows, S5_BLOCKS * LANES), F32)],
        [pltpu.VMEM((SUBLANES, S5_BW), F32)], nt, side)


def _s5_bwd(name, z, dy, xs, m, *, reverse, side=None):
    n_rows = z.shape[0]
    tt, nt, rows, per_block, par = _s5_specs(n_rows, reverse)

    def body(u_ref, dy_ref, x_ref, bt_ref, ct_ref, ak_ref, ap_ref, du_ref, db_ref, dc_ref, da_ref, lam, carry):
        t = pl.program_id(1)

        @pl.when(t == 0)
        def _():
            carry[...] = jnp.zeros(carry.shape, F32)
            db_ref[...] = jnp.zeros(db_ref.shape, F32)
            dc_ref[...] = jnp.zeros(dc_ref.shape, F32)
            da_ref[...] = jnp.zeros(da_ref.shape, F32)

        dy_b = dy_ref[...].astype(BF16)
        lam[...] = jnp.dot(dy_b, ct_ref[...].astype(BF16), preferred_element_type=F32)
        _scan_tile(lam, ak_ref, ap_ref, carry, reverse=reverse, x_ref=x_ref, da_ref=da_ref)
        lam_b = lam[...].astype(BF16)
        du_ref[...] = jnp.dot(lam_b, bt_ref[...].astype(BF16), preferred_element_type=F32)
        db_ref[...] += lax.dot_general(u_ref[...].astype(BF16), lam_b, _TN, preferred_element_type=F32)
        dc_ref[...] += lax.dot_general(x_ref[...].astype(BF16), dy_b, _TN, preferred_element_type=F32)

        @pl.when(t == nt - 1)
        def _():
            da_ref[...] = jnp.broadcast_to(jnp.sum(da_ref[...], axis=0, keepdims=True), da_ref.shape)

    return _s5_call(
        name, body, [z, dy, xs, m["b_map_t"], m["c_map_t"], m["ak_adj"], m["apow_adj"]],
        [rows(LANES, Z_U // LANES), rows(LANES, 0), rows(S5_BW, 0), per_block(S5_BW, LANES), per_block(LANES, S5_BW),
         par(3 * SUBLANES), par(SUBLANES)],
        [rows(LANES, 0), per_block(LANES, S5_BW), per_block(S5_BW, LANES), par(SUBLANES)],
        [jax.ShapeDtypeStruct((n_rows, S5_BLOCKS * LANES), F32), jax.ShapeDtypeStruct((S5_BLOCKS, LANES, S5_BW), F32),
         jax.ShapeDtypeStruct((S5_BLOCKS, S5_BW, LANES), F32), jax.ShapeDtypeStruct((SUBLANES, S5_BLOCKS * S5_BW), F32)],
        [pltpu.VMEM((tt, S5_BW), F32), pltpu.VMEM((SUBLANES, S5_BW), F32)], nt, side)


def _ret_chunk(zq, zk, v, cosf, sinf, state, rd, reverse):
    c = RET_CHUNK
    lg = jax.nn.log_sigmoid(rd)
    lg1 = jnp.max(lg, axis=1, keepdims=True)
    q = _rope(zq, cosf, sinf) * (RET_QK ** -0.5)
    k = _rope(zk, cosf, sinf)
    pi = lax.broadcasted_iota(jnp.int32, (c, c), 0).astype(F32)
    pj = lax.broadcasted_iota(jnp.int32, (c, c), 1).astype(F32)
    pcol = lax.broadcasted_iota(jnp.int32, (c, 1), 0).astype(F32)
    if reverse:
        diff, mask, pos = pj - pi, pj > pi, (c - 1) - pcol
    else:
        diff, mask, pos = pi - pj, pi >= pj, pcol
    decay_in = jnp.where(mask, jnp.exp(jnp.where(mask, diff, 0.0) * lg), 0.0)
    scores = _bdot(q, k, 1, 1) * decay_in
    inner = _bdot(scores, v, 1, 0)
    k_w = jnp.exp((c - 1 - pos) * lg1)
    kv = _bdot(k * k_w, v, 0, 0)
    q_w = jnp.exp((pos + 1) * lg1)
    cross = _bdot(q, state, 1, 0) * q_w
    new_state = jnp.exp(c * lg1) * state + kv
    return inner + cross, new_state


RET_ZW = RET_HEADS * (2 * RET_QK + RET_V)


def _ret_specs(n_chunks, reverse_order):
    cmap = (lambda n: n_chunks - 1 - n) if reverse_order else (lambda n: n)
    z_spec = pl.BlockSpec((RET_CHUNK, RET_ZW), lambda n: (cmap(n), 0))
    t_spec = pl.BlockSpec((RET_CHUNK, LANES), lambda n: (cmap(n), 0))
    rd_spec = pl.BlockSpec((RET_HEADS, 1, LANES), lambda n: (0, 0, 0))
    o_spec = pl.BlockSpec((RET_CHUNK, RET_HEADS * RET_V), lambda n: (cmap(n), 0))
    st_spec = pl.BlockSpec((RET_HEADS, None, RET_QK, RET_V), lambda n: (0, cmap(n), 0, 0))
    return z_spec, t_spec, rd_spec, o_spec, st_spec


def _ret_head(zt, h):
    b = h * (2 * RET_QK + RET_V)
    return zt[:, b:b + RET_QK], zt[:, b + RET_QK:b + 2 * RET_QK], zt[:, b + 2 * RET_QK:b + 2 * RET_QK + RET_V]


def _ret_fwd(name, z, cosf, sinf, rd, *, reverse):
    n_rows = z.shape[0]
    n_chunks = n_rows // RET_CHUNK
    z_spec, t_spec, rd_spec, o_spec, st_spec = _ret_specs(n_chunks, reverse)

    def body(z_ref, cos_ref, sin_ref, rd_ref, o_ref, st_ref, state):
        @pl.when(pl.program_id(0) == 0)
        def _():
            state[...] = jnp.zeros(state.shape, F32)

        zt = z_ref[...]
        cosv, sinv = cos_ref[...], sin_ref[...]
        for h in range(RET_HEADS):
            st = state[h]
            st_ref[h] = st
            out, new = _ret_chunk(*_ret_head(zt, h), cosv, sinv, st, rd_ref[h], reverse)
            o_ref[:, RET_V * h:RET_V * (h + 1)] = out
            state[h] = new

    return pl.pallas_call(
        body, name=name, grid=(n_chunks,), in_specs=[z_spec, t_spec, t_spec, rd_spec],
        out_specs=[o_spec, st_spec],
        out_shape=[jax.ShapeDtypeStruct((n_rows, RET_HEADS * RET_V), F32),
                   jax.ShapeDtypeStruct((RET_HEADS, n_chunks, RET_QK, RET_V), F32)],
        scratch_shapes=[pltpu.VMEM((RET_HEADS, RET_QK, RET_V), F32)], compiler_params=_params(("arbitrary",)))(z, cosf, sinf, rd)


def _ret_bwd(name, z, cosf, sinf, rd, states, dout, dz, *, reverse):
    n_rows = z.shape[0]
    n_chunks = n_rows // RET_CHUNK
    z_spec, t_spec, rd_spec, o_spec, st_spec = _ret_specs(n_chunks, not reverse)

    def body(z_ref, cos_ref, sin_ref, rd_ref, st_ref, do_ref, dzin_ref, dz_ref, drd_ref, dstate):
        @pl.when(pl.program_id(0) == 0)
        def _():
            dstate[...] = jnp.zeros(dstate.shape, F32)
            drd_ref[...] = jnp.zeros(drd_ref.shape, F32)

        zt = z_ref[...]
        cosv, sinv = cos_ref[...], sin_ref[...]
        parts = []
        for h in range(RET_HEADS):
            _, vjp = jax.vjp(lambda a, b, c, s, r: _ret_chunk(a, b, c, cosv, sinv, s, r, reverse),
                             *_ret_head(zt, h), st_ref[h], rd_ref[h])
            dq, dk, dv, dst, drd = vjp((do_ref[:, RET_V * h:RET_V * (h + 1)], dstate[h]))
            parts += [dq, dk, dv]
            dstate[h] = dst
            drd_ref[h] += jnp.sum(drd, axis=1, keepdims=True)
        dz_ref[...] = (dzin_ref[...].astype(F32) + jnp.concatenate(parts, axis=1)).astype(dz_ref.dtype)

    return pl.pallas_call(
        body, name=name, grid=(n_chunks,),
        in_specs=[z_spec, t_spec, t_spec, rd_spec, st_spec, o_spec, z_spec],
        out_specs=[z_spec, rd_spec],
        out_shape=[jax.ShapeDtypeStruct(dz.shape, dz.dtype), jax.ShapeDtypeStruct((RET_HEADS, 1, LANES), F32)],
        input_output_aliases={6: 0}, scratch_shapes=[pltpu.VMEM((RET_HEADS, RET_QK, RET_V), F32)],
        compiler_params=_params(("arbitrary",)))(z, cosf, sinf, rd, states, dout, dz)


_NT = (((1,), (1,)), ((), ()))
_TN = (((0,), (0,)), ((), ()))


def _attn_tiles(n_rows, tq_cap):
    tq, tk = _pick(n_rows, tq_cap), _pick(n_rows, 2048)
    return tq, max(tq // 2, LANES), tk, min(tk, 1024)


def _attn_fwd(name, q, kv, kr):
    n_rows = q.shape[0]
    tq, hq, tk, sub = _attn_tiles(n_rows, 512)
    nk = n_rows // tk

    def body(q_ref, kn_ref, v_ref, kr_ref, o_ref, lse_ref, m_sc, acc):
        j = pl.program_id(2)

        @pl.when(j == 0)
        def _():
            m_sc[...] = jnp.full(m_sc.shape, -jnp.inf, F32)
            acc[...] = jnp.zeros(acc.shape, F32)

        for c in range(tk // sub):
            rows = slice(c * sub, (c + 1) * sub)
            k = jnp.concatenate([kn_ref[rows, :], kr_ref[rows, :]], axis=1)
            v1 = jnp.concatenate([v_ref[rows, :], jnp.ones((sub, LANES), BF16)], axis=1)
            for part in range(tq // hq):
                qr = slice(part * hq, (part + 1) * hq)
                s = lax.dot_general(q_ref[qr, :], k, _NT, preferred_element_type=F32)
                m_prev = m_sc[qr, :]
                m_new = jnp.maximum(m_prev, jnp.max(s, axis=1, keepdims=True))
                p = jnp.exp(s - m_new)
                acc[qr, :] = jnp.exp(m_prev - m_new) * acc[qr, :] + jnp.dot(p.astype(BF16), v1, preferred_element_type=F32)
                m_sc[qr, :] = m_new

        @pl.when(j == nk - 1)
        def _():
            l = acc[:, LANES:]
            o_ref[...] = acc[:, :LANES] / l
            lse_ref[...] = m_sc[...] + jnp.log(l)

    return pl.pallas_call(
        body, name=name, grid=(MLA_HEADS, n_rows // tq, nk),
        in_specs=[pl.BlockSpec((tq, 256), lambda h, i, j: (i, h)),
                  pl.BlockSpec((tk, 128), lambda h, i, j: (j, 2 * h)),
                  pl.BlockSpec((tk, 128), lambda h, i, j: (j, 2 * h + 1)),
                  pl.BlockSpec((tk, 128), lambda h, i, j: (j, 0))],
        out_specs=[pl.BlockSpec((tq, 128), lambda h, i, j: (i, h)),
                   pl.BlockSpec((None, tq, 128), lambda h, i, j: (h, i, 0))],
        out_shape=[jax.ShapeDtypeStruct((n_rows, MLA_HEADS * MLA_V), F32),
                   jax.ShapeDtypeStruct((MLA_HEADS, n_rows, LANES), F32)],
        scratch_shapes=[pltpu.VMEM((tq, 1), F32), pltpu.VMEM((tq, 2 * LANES), F32)],
        compiler_params=_params(("parallel", "parallel", "arbitrary")))(q, kv, kv, kr)


def _attn_bwd(name, q, kv, kr, o, lse, do):
    n_rows = q.shape[0]
    tq, hq, tk, sub = _attn_tiles(n_rows, 1024)
    nq = n_rows // tq

    def body(q_ref, kn_ref, v_ref, kr_ref, o_ref, lse_ref, do_ref, dq_ref, dkv_ref, dkr_ref, dk_acc, dv_acc):
        j, i = pl.program_id(1), pl.program_id(2)

        @pl.when(i == 0)
        def _():
            dk_acc[...] = jnp.zeros(dk_acc.shape, F32)
            dv_acc[...] = jnp.zeros(dv_acc.shape, F32)

        @pl.when((i == 0) & (j == 0))
        def _():
            dq_ref[...] = jnp.zeros(dq_ref.shape, F32)

        for part in range(tq // hq):
            qr = slice(part * hq, (part + 1) * hq)
            qv = q_ref[qr, :]
            do = do_ref[qr, :]
            do_b = do.astype(BF16)
            delta = jnp.sum(do * o_ref[qr, :], axis=1, keepdims=True)
            lse_col = lse_ref[qr, :][:, :1]
            dq = None
            for c in range(tk // sub):
                rows = slice(c * sub, (c + 1) * sub)
                k = jnp.concatenate([kn_ref[rows, :], kr_ref[rows, :]], axis=1)
                s = lax.dot_general(qv, k, _NT, preferred_element_type=F32)
                p = jnp.exp(s - lse_col)
                dp = lax.dot_general(do_b, v_ref[rows, :], _NT, preferred_element_type=F32)
                ds = (p * (dp - delta)).astype(BF16)
                dv_acc[rows, :] += lax.dot_general(p.astype(BF16), do_b, _TN, preferred_element_type=F32)
                dk_acc[rows, :] += lax.dot_general(ds, qv, _TN, preferred_element_type=F32)
                t = jnp.dot(ds, k, preferred_element_type=F32)
                dq = t if dq is None else dq + t
            r0 = pl.multiple_of(i * tq + part * hq, hq)
            dq_ref[pl.ds(r0, hq), :] += dq

        @pl.when(i == nq - 1)
        def _():
            dkv_ref[...] = jnp.concatenate([dk_acc[:, :128], dv_acc[...]], axis=1).astype(dkv_ref.dtype)
            dkr_ref[...] = dk_acc[:, 128:]

    return pl.pallas_call(
        body, name=name, grid=(MLA_HEADS, n_rows // tk, nq),
        in_specs=[pl.BlockSpec((tq, 256), lambda h, j, i: (i, h)),
                  pl.BlockSpec((tk, 128), lambda h, j, i: (j, 2 * h)),
                  pl.BlockSpec((tk, 128), lambda h, j, i: (j, 2 * h + 1)),
                  pl.BlockSpec((tk, 128), lambda h, j, i: (j, 0)),
                  pl.BlockSpec((tq, 128), lambda h, j, i: (i, h)),
                  pl.BlockSpec((None, tq, 128), lambda h, j, i: (h, i, 0)),
                  pl.BlockSpec((tq, 128), lambda h, j, i: (i, h))],
        out_specs=[pl.BlockSpec((n_rows, 256), lambda h, j, i: (0, h)),
                   pl.BlockSpec((tk, 256), lambda h, j, i: (j, h)),
                   pl.BlockSpec((tk, 128), lambda h, j, i: (j, h))],
        out_shape=[jax.ShapeDtypeStruct((n_rows, MLA_HEADS * 256), F32), jax.ShapeDtypeStruct((n_rows, MLA_HEADS * 256), BF16),
                   jax.ShapeDtypeStruct((n_rows, MLA_HEADS * 128), F32)],
        scratch_shapes=[pltpu.VMEM((tk, 256), F32), pltpu.VMEM((tk, 128), F32)],
        compiler_params=_params(("parallel", "arbitrary", "arbitrary")))(q, kv, kv, kr, o, lse, do)


_MESH = pl.DeviceIdType.MESH


def _gather_stages(x_refs, out_refs, send_sems, recv_sems, local_sems):
    n = len(x_refs)
    mx, my, mc = lax.axis_index("x"), lax.axis_index("y"), lax.axis_index("c")
    me, sibling = (mx, my, mc), (mx, my, 1 - mc)
    chips = [(1 - mx, my), (mx, 1 - my), (1 - mx, 1 - my)]

    def copy(a, k, block, to, src=None):
        dst = out_refs[a].at[4 * block[0] + 2 * block[1] + block[2]]
        return pltpu.make_async_remote_copy(
            src_ref=dst if src is None else src, dst_ref=dst, send_sem=send_sems.at[7 * a + k],
            recv_sem=recv_sems.at[7 * a + k], device_id=to, device_id_type=_MESH)

    def mine():
        return [pltpu.make_async_copy(x_refs[a], out_refs[a].at[4 * mx + 2 * my + mc], local_sems.at[a]) for a in range(n)]

    def first():
        out = []
        for a in range(n):
            out.append(copy(a, 0, me, sibling, src=x_refs[a]))
            out += [copy(a, 1 + j, me, (*chip, mc), src=x_refs[a]) for j, chip in enumerate(chips)]
        return out

    def passed():
        return [copy(a, 4 + j, (*chip, mc), sibling) for j, chip in enumerate(chips) for a in range(n)]

    def start():
        for cp in mine() + first():
            cp.start()

    def forward():
        for j, chip in enumerate(chips):
            for a in range(n):
                copy(a, 1 + j, (*chip, mc), me).wait_recv()
        for cp in passed():
            cp.start()

    def finish():
        for a in range(n):
            copy(a, 0, sibling, me).wait_recv()
            for j, chip in enumerate(chips):
                copy(a, 4 + j, (*chip, 1 - mc), me).wait_recv()
        for cp in first() + passed():
            cp.wait_send()
        for cp in mine():
            cp.wait()

    return start, forward, finish


def _exchange_stages(g_refs, land_refs, send_sems, recv_sems, local_sems):
    n = len(g_refs)
    mx, my, mc = lax.axis_index("x"), lax.axis_index("y"), lax.axis_index("c")
    me = 4 * mx + 2 * my + mc

    def mine():
        return [pltpu.make_async_copy(g_refs[a].at[me], land_refs[a].at[me], local_sems.at[a]) for a in range(n)]

    def copies():
        out = []
        for k in range(1, N_DEV):
            px = 1 - mx if k & 4 else mx
            py = 1 - my if k & 2 else my
            pc = 1 - mc if k & 1 else mc
            peer = 4 * px + 2 * py + pc
            for a in range(n):
                sems = dict(send_sem=send_sems.at[7 * a + k - 1], recv_sem=recv_sems.at[7 * a + k - 1],
                            device_id=(px, py, pc), device_id_type=_MESH)
                out.append((pltpu.make_async_remote_copy(src_ref=g_refs[a].at[peer], dst_ref=land_refs[a].at[me], **sems),
                            pltpu.make_async_remote_copy(src_ref=g_refs[a].at[peer], dst_ref=land_refs[a].at[peer], **sems)))
        return out

    def start():
        for cp in mine():
            cp.start()
        for send, _ in copies():
            send.start()

    def finish():
        both = copies()
        for _, recv in both:
            recv.wait_recv()
        for send, _ in both:
            send.wait_send()
        for cp in mine():
            cp.wait()

    return start, lambda: None, finish


def _exchange_parts(side):
    kind, arrays = side
    n = len(arrays)
    if kind == "gather":
        make, shapes = _gather_stages, [jax.ShapeDtypeStruct((N_DEV,) + a.shape, a.dtype) for a in arrays]
    else:
        make, shapes = _exchange_stages, [jax.ShapeDtypeStruct(a.shape, a.dtype) for a in arrays]
    sems = [pltpu.SemaphoreType.DMA((7 * n,)), pltpu.SemaphoreType.DMA((7 * n,)), pltpu.SemaphoreType.DMA((n,))]
    return make, shapes, sems


def _exchange(name, side):
    make, shapes, sems = _exchange_parts(side)
    n = len(shapes)

    def body(*refs):
        start, forward, finish = make(refs[:n], refs[n:2 * n], *refs[2 * n:])
        start()
        forward()
        finish()

    hbm = pl.BlockSpec(memory_space=pl.ANY)
    return list(pl.pallas_call(body, name=name, out_shape=shapes, in_specs=[hbm] * n, out_specs=[hbm] * n,
                               scratch_shapes=sems)(*side[1]))


def _all_gather(name, xs):
    return _exchange(name, ("gather", xs))


def _all_to_all(name, gs):
    return _exchange(name, ("exchange", gs))


def _adamw(name, parts, w, m, v, tr):
    rows, cols = w.shape

    def body(p_ref, w_ref, m_ref, v_ref, g_ref, d_ref, nm_ref, nv_ref):
        g = p_ref[0].astype(F32)
        for d in range(1, N_DEV):
            g = g + p_ref[d].astype(F32)
        nm = ADAM_B1 * m_ref[...] + (1.0 - ADAM_B1) * g
        nv = ADAM_B2 * v_ref[...] + (1.0 - ADAM_B2) * jnp.square(g)
        m_hat = nm / (1.0 - ADAM_B1 ** ADAM_STEP)
        v_hat = nv / (1.0 - ADAM_B2 ** ADAM_STEP)
        g_ref[...] = g
        d_ref[...] = -ADAM_LR * (m_hat / (jnp.sqrt(v_hat) + ADAM_EPS) + ADAM_WD * w_ref[...])
        nm_ref[...] = nm
        nv_ref[...] = nv

    spec = pl.BlockSpec((tr, cols), lambda i: (i, 0))
    return pl.pallas_call(
        body, name=name, grid=(rows // tr,),
        in_specs=[pl.BlockSpec((N_DEV, tr, cols), lambda i: (0, i, 0)), spec, spec, spec],
        out_specs=[spec] * 4, out_shape=[jax.ShapeDtypeStruct((rows, cols), F32)] * 4,
        compiler_params=_params(("parallel",)))(parts, w, m, v)


def _in_pieces():
    p = []
    for h in range(RET_HEADS):
        p += [(128 * h, 128 * h + 128), (512 + 128 * h, 512 + 128 * h + 128), (1024 + 256 * h, 1024 + 256 * h + 256)]
    p += [(2048, 3072), (3776, 4800), (4800, 7872), (3072, 3456), 128, (3456, 3712),
          (3712, 3744), 32, (3744, 3776), 32, 128]
    return p


def _uq_pieces():
    p = []
    for h in range(MLA_HEADS):
        b = 192 * h
        p += [(b, b + 128), (b + 128, b + 160), 32, (b + 160, b + 192), 32]
    return p


def _perm(w, pieces):
    cols = [jnp.zeros(w.shape[:-1] + (p,), w.dtype) if isinstance(p, int) else w[..., p[0]:p[1]] for p in pieces]
    return jnp.concatenate(cols, axis=-1)


def _unperm(dw, pieces):
    found, off = [], 0
    for p in pieces:
        if isinstance(p, int):
            off += p
        else:
            found.append((p[0], dw[..., off:off + p[1] - p[0]]))
            off += p[1] - p[0]
    return jnp.concatenate([t for _, t in sorted(found, key=lambda s: s[0])], axis=-1)


def _unshard(blocks, axis):
    return jnp.concatenate([blocks[p] for p in range(N_DEV)], axis=axis)


def _shard_split(full, axis):
    return jnp.stack(jnp.split(full, N_DEV, axis=axis), axis=0)


def _row_tile(rows, cap=256, unit=16):
    return max(t for t in range(unit, cap + 1, unit) if rows % t == 0)


def _rope_tables(seq):
    pos = jnp.arange(seq, dtype=F32)[:, None]

    def table(dim):
        inv = 1.0 / (ROPE_THETA ** (jnp.arange(0, dim, 2, dtype=F32) / dim))
        ang = pos * inv[None, :]
        return jnp.cos(ang), jnp.sin(ang)

    cr, sr = table(RET_QK)
    cm, sm = table(MLA_ROPE)
    z = jnp.zeros_like(cm)
    return (jnp.concatenate([cr, cr], 1), jnp.concatenate([-sr, sr], 1),
            jnp.concatenate([cm, z, cm, z], 1), jnp.concatenate([-sm, z, sm, z], 1))


def _s5_maps(a_re, a_im, log_dt, b_re, b_im, c_re, c_im):
    dt = jnp.exp(log_dt)[:, None]
    ar = jnp.minimum(a_re, -1e-4)
    mag = jnp.exp(dt * ar)
    abar_re = mag * jnp.cos(dt * a_im)
    abar_im = mag * jnp.sin(dt * a_im)
    den = ar * ar + a_im * a_im
    nr = abar_re - 1.0
    ni = abar_im
    coef_re = (nr * ar + ni * a_im) / den
    coef_im = (ni * ar - nr * a_im) / den
    bb_re = coef_re[..., None] * b_re - coef_im[..., None] * b_im
    bb_im = coef_re[..., None] * b_im + coef_im[..., None] * b_re
    eye = jnp.eye(S5_BLOCKS, dtype=F32)

    def in_blocks(bb):
        t = bb.transpose(0, 2, 1).reshape(S5_BLOCKS, 8, S5_GROUP, S5_STATE)
        return jnp.einsum('jgcp,gh->jgchp', t, eye).reshape(S5_BLOCKS, 128, 512)

    def out_blocks(cc):
        t = cc.transpose(0, 2, 1).reshape(S5_BLOCKS, 8, S5_STATE, S5_GROUP)
        return jnp.einsum('jgpc,gh->jgphc', t, eye).reshape(S5_BLOCKS, 512, 128)

    arow = jnp.concatenate([abar_re.reshape(S5_BLOCKS, 512), abar_im.reshape(S5_BLOCKS, 512)], axis=1).reshape(1, -1)
    b_map = jnp.concatenate([in_blocks(bb_re), in_blocks(bb_im)], axis=2)
    c_map = jnp.concatenate([out_blocks(c_re), -out_blocks(c_im)], axis=1)
    return arow, b_map, c_map


def _power_tables(arow, conj, reverse):
    a = arow.reshape(S5_BLOCKS, 2, 512)
    ar, ai = a[:, 0], (-a[:, 1] if conj else a[:, 1])
    pw = [(ar, ai)]
    for _ in range(SUBLANES - 1):
        pr, pi = pw[-1]
        pw.append((pr * ar - pi * ai, pr * ai + pi * ar))

    def rows(sel):
        return jnp.stack([jnp.stack(list(pw[i]), axis=1) for i in sel], axis=0).reshape(len(sel), -1)

    rowid = jnp.arange(SUBLANES)[:, None]
    ak = jnp.concatenate([jnp.where((rowid < SUBLANES - k) if reverse else (rowid >= k), rows([k - 1]), 0.0)
                          for k in (1, 2, 4)], axis=0)
    order = list(range(SUBLANES))
    apow = rows(order[::-1] if reverse else order)
    return ak, apow


def _rows(arr):
    return (arr, arr.shape[1], 0)


def _vjp_rows(f, n_prim):
    def fn(r, c, _):
        _, vjp = jax.vjp(f, *r[:n_prim])
        return list(vjp(r[n_prim])), []
    return fn


def _norm_bwd(r, c, _):
    _, vjp = jax.vjp(_f_rms, r[0], c[0])
    dx, dg = vjp(r[1])
    return [dx + r[2]], [dg]


def _layer_fwd(l, x, W_first, W_rest, P, T, sides):
    W = dict(W_first)
    n = x.shape[0]
    tm, tmw = _pick(n, 256), _pick(n, 128)
    cos_r, sin_r, cos_m, sin_m = T

    def nm(s):
        return f"l{l}_{s}"

    def one(name, f, rows, consts, width, dtype=BF16, tile=tm):
        return _tile_call(nm(name), lambda r, c, _: ([f(r, c)], []), n, tile, rows, consts, [(width, dtype)])[0]

    h = one("norm1", lambda r, c: _f_rms(r[0], c[0]), [_rows(x)], [P["norm1_g"]], D_MODEL)
    z = _mm(nm("in_proj"), h, W["in"])
    of, stf = _ret_fwd(nm("ret_f"), z, cos_r, sin_r, P["rd"][0], reverse=False)
    ob, stb = _ret_fwd(nm("ret_b"), z, cos_r, sin_r, P["rd"][1], reverse=True)

    def gn(r, c, _):
        yraw = r[0] + r[1]
        ys = [_f_gn_gate(yraw[:, RET_V * i:RET_V * (i + 1)], r[2][:, RET_V * i:RET_V * (i + 1)],
                         c[0][:, RET_V * i:RET_V * (i + 1)]) for i in range(RET_HEADS)]
        return [yraw, jnp.concatenate(ys, axis=1)], []

    yraw, yret = _tile_call(nm("ret_gn"), gn, n, tm, [_rows(of), _rows(ob), (z, 1024, Z_RG // 1024)], [P["ret_gn_g"]],
                            [(1024, F32), (1024, BF16)])

    cqn = one("q_norm", lambda r, c: _f_rms(r[0][:, :MLA_Q_LORA], c[0]), [(z, 512, Z_CQ // 512)], [P["mla_q_norm_g"]], MLA_Q_LORA)
    ckvn = one("kv_norm", lambda r, c: _f_rms(r[0], c[0]), [(z, 256, Z_CKV // 256)], [P["mla_kv_norm_g"]], MLA_KV_LORA)
    qraw = _mm(nm("q_up"), cqn, W["uq"])
    kv = _mm(nm("kv_up"), ckvn, W["ukv"], out_dtype=BF16)
    q = one("q_rope", lambda r, c: _mla_q(r[0], r[1], r[2], False), [_rows(qraw), _rows(cos_m), _rows(sin_m)], [], 2048)
    kr = one("k_rope", lambda r, c: _rope(r[0], r[1], r[2]), [(z, 128, Z_KR // 128), _rows(cos_m), _rows(sin_m)], [], 128)
    o, lse = _attn_fwd(nm("attn"), q, kv, kr)

    xs, y_dir, side_res = [], [], {}
    for d in range(2):
        x_d, y_d, side_res[d] = _s5_fwd(nm(f"s5_scan{d}"), z, P["s5"][d], reverse=(d == 1), side=sides.get(d))
        xs.append(x_d)
        y_dir.append(y_d)
    W.update(W_rest(side_res))

    def s5_act(r, c, _):
        ysum = r[0] + r[1]
        return [ysum, _f_s5_act(ysum, r[2], c[0])], []

    ysum, gact = _tile_call(nm("s5_act"), s5_act, n, tm, [_rows(y_dir[0]), _rows(y_dir[1]), (z, 1024, Z_U // 1024)], [P["s5_d"]],
                            [(1024, F32), (1024, BF16)])
    ga = _mm(nm("glu_a"), gact, W["glu_a"])
    gb = _mm(nm("glu_b"), gact, W["glu_b"])
    ys5 = one("glu", lambda r, c: _f_glu(r[0], r[1]), [_rows(ga), _rows(gb)], [], 1024)

    ys = (yret, o, ys5)
    ps = [_mm(nm(f"branch{i}"), ys[i], W["branch"][i]) for i in range(3)]
    mix = one("mix", lambda r, c: _f_gate(r[0], r[3]) + _f_gate(r[1], r[4]) + _f_gate(r[2], r[5]),
              [(z, 1024, Z_GATE // 1024 + i) for i in range(3)] + [_rows(p) for p in ps], [], 1024)
    x1 = _mm(nm("out_proj"), mix, W["out"], add=x)
    h2 = one("norm2", lambda r, c: _f_rms(r[0], c[0]), [_rows(x1)], [P["norm2_g"]], D_MODEL)
    gp = _mm(nm("ffn_g"), h2, W["ffn_g"])
    up = _mm(nm("ffn_u"), h2, W["ffn_u"])
    act = one("swiglu", lambda r, c: _f_swiglu(r[0], r[1]), [_rows(gp), _rows(up)], [], FFN_HIDDEN, tile=tmw)
    x2 = _mm(nm("ffn_down"), act, W["ffn_down"], add=x1)
    saved = dict(x=x, h=h, z=z, stf=stf, stb=stb, yraw=yraw, ys=ys, cqn=cqn, ckvn=ckvn, qraw=qraw, q=q, kv=kv, kr=kr,
                 lse=lse, xs=xs, ysum=ysum, gact=gact, ga=ga, gb=gb, ps=ps, mix=mix, x1=x1, h2=h2, gp=gp, up=up, act=act, W=W)
    return x2, saved, side_res


def _layer_bwd(l, dx2, sv, P, T, side_of):
    n = dx2.shape[0]
    tm, tmw = _pick(n, 256), _pick(n, 128)
    cos_r, sin_r, cos_m, sin_m = T
    z, W = sv["z"], sv["W"]
    g = {}

    def nm(s):
        return f"l{l}_{s}"

    dact = _mm(nm("d_act"), dx2, W["ffn_down"], tb=True)
    g["ffn_down"] = _mm(nm("dw_ffn_down"), sv["act"], dx2, ta=True)
    dgp, dup = _tile_call(nm("d_swiglu"), _vjp_rows(_f_swiglu, 2), n, tmw, [_rows(sv["gp"]), _rows(sv["up"]), _rows(dact)], [],
                          [(FFN_HIDDEN, BF16)] * 2)
    dh2 = _mm(nm("d_h2_g"), dgp, W["ffn_g"], tb=True)
    dh2 = _mm(nm("d_h2_u"), dup, W["ffn_u"], tb=True, add=dh2)
    g["ffn_g"] = _mm(nm("dw_ffn_g"), sv["h2"], dgp, ta=True)
    g["ffn_u"] = _mm(nm("dw_ffn_u"), sv["h2"], dup, ta=True)
    dx1, g["norm2_g"] = _tile_call(nm("d_norm2"), _norm_bwd, n, tm, [_rows(sv["x1"]), _rows(dh2), _rows(dx2)], [P["norm2_g"]],
                                   [(D_MODEL, F32)], acc_outs=[(1, D_MODEL)])

    dmix = _mm(nm("d_mix"), dx1, W["out"], tb=True)
    g["out"] = _mm(nm("dw_out"), sv["mix"], dx1, ta=True)
    dz = jnp.zeros((n, ZW), BF16)
    dys, g["branch"] = [], []
    for i in range(3):
        dz, dp = _tile_call(nm(f"d_gate{i}"), _vjp_rows(_f_gate, 2), n, tm,
                            [(z, 1024, Z_GATE // 1024 + i), _rows(sv["ps"][i]), _rows(dmix)], [], [(1024, BF16)],
                            alias=(dz, 1024, Z_GATE // 1024 + i))
        dys.append(_mm(nm(f"d_branch{i}"), dp, W["branch"][i], tb=True))
        g["branch"].append(_mm(nm(f"dw_branch{i}"), sv["ys"][i], dp, ta=True))

    dga, dgb = _tile_call(nm("d_glu"), _vjp_rows(_f_glu, 2), n, tm, [_rows(sv["ga"]), _rows(sv["gb"]), _rows(dys[2])], [],
                          [(1024, BF16)] * 2)
    dgact = _mm(nm("d_gact_a"), dga, W["glu_a"], tb=True)
    dgact = _mm(nm("d_gact_b"), dgb, W["glu_b"], tb=True, add=dgact)
    g["glu_a"] = _mm(nm("dw_glu_a"), sv["gact"], dga, ta=True)
    g["glu_b"] = _mm(nm("dw_glu_b"), sv["gact"], dgb, ta=True)

    def act_bwd(r, c, _):
        _, vjp = jax.vjp(_f_s5_act, r[0], r[1], c[0])
        dy, du, dd = vjp(r[2])
        return [dy, du], [dd]

    dysum, du_part, g["s5_d"] = _tile_call(nm("d_s5_act"), act_bwd, n, tm,
                                           [_rows(sv["ysum"]), (z, 1024, Z_U // 1024), _rows(dgact)], [P["s5_d"]],
                                           [(1024, F32)] * 2, acc_outs=[(1, 1024)])
    dus, g["s5"] = [], []
    side_res = {}
    for d in range(2):
        du, g_b, g_c, da, side_res[d] = _s5_bwd(nm(f"d_s5_scan{d}"), z, dysum, sv["xs"][d], P["s5"][d], reverse=(d == 0),
                                                side=side_of(d, g))
        dus.append(du)
        g["s5"].append((da[:1], g_b, g_c))
    dz, = _tile_call(nm("d_s5_u"), lambda r, c, _: ([r[0] + r[1] + r[2]], []), n, tm,
                     [_rows(du_part), _rows(dus[0]), _rows(dus[1])], [], [], alias=(dz, 1024, Z_U // 1024))

    o = sv["ys"][1]
    dq, dkv, dkr = _attn_bwd(nm("d_attn"), sv["q"], sv["kv"], sv["kr"], o, sv["lse"], dys[1])
    dqraw, = _tile_call(nm("d_q_rope"), lambda r, c, _: ([_mla_q(r[0], r[1], r[2], True)], []), n, tm,
                        [_rows(dq), _rows(cos_m), _rows(sin_m)], [], [(2048, BF16)])

    def kr_bwd(r, c, _):
        tot = r[0][:, :128]
        for h in range(1, MLA_HEADS):
            tot = tot + r[0][:, 128 * h:128 * (h + 1)]
        return [_rope_t(tot, r[1], r[2])], []

    dz, = _tile_call(nm("d_k_rope"), kr_bwd, n, tm, [_rows(dkr), _rows(cos_m), _rows(sin_m)], [], [],
                     alias=(dz, 128, Z_KR // 128))
    dcqn = _mm(nm("d_cqn"), dqraw, W["uq"], tb=True)
    g["uq"] = _mm(nm("dw_uq"), sv["cqn"], dqraw, ta=True)
    dckvn = _mm(nm("d_ckvn"), dkv, W["ukv"], tb=True)
    g["ukv"] = _mm(nm("dw_ukv"), sv["ckvn"], dkv, ta=True)

    def qn_bwd(r, c, _):
        _, vjp = jax.vjp(_f_rms, r[0][:, :MLA_Q_LORA], c[0])
        da, dg = vjp(r[1])
        return [jnp.concatenate([da, jnp.zeros((da.shape[0], 512 - MLA_Q_LORA), F32)], axis=1)], [dg]

    dz, g["mla_q_norm_g"] = _tile_call(nm("d_q_norm"), qn_bwd, n, tm, [(z, 512, Z_CQ // 512), _rows(dcqn)], [P["mla_q_norm_g"]],
                                       [], acc_outs=[(1, MLA_Q_LORA)], alias=(dz, 512, Z_CQ // 512))

    def kvn_bwd(r, c, _):
        _, vjp = jax.vjp(_f_rms, r[0], c[0])
        da, dg = vjp(r[1])
        return [da], [dg]

    dz, g["mla_kv_norm_g"] = _tile_call(nm("d_kv_norm"), kvn_bwd, n, tm, [(z, 256, Z_CKV // 256), _rows(dckvn)],
                                        [P["mla_kv_norm_g"]], [], acc_outs=[(1, MLA_KV_LORA)], alias=(dz, 256, Z_CKV // 256))

    def gn_bwd(r, c, _):
        drg, dy, dg = [], [], []
        for i in range(RET_HEADS):
            sl = slice(RET_V * i, RET_V * (i + 1))
            _, vjp = jax.vjp(_f_gn_gate, r[0][:, sl], r[1][:, sl], c[0][:, sl])
            a, b, e = vjp(r[2][:, sl])
            dy.append(a)
            drg.append(b)
            dg.append(e)
        return [jnp.concatenate(drg, axis=1), jnp.concatenate(dy, axis=1)], [jnp.concatenate(dg, axis=1)]

    dz, dyraw, g["ret_gn_g"] = _tile_call(nm("d_ret_gn"), gn_bwd, n, tm,
                                          [_rows(sv["yraw"]), (z, 1024, Z_RG // 1024), _rows(dys[0])], [P["ret_gn_g"]],
                                          [(1024, F32)], acc_outs=[(1, 1024)], alias=(dz, 1024, Z_RG // 1024))
    dz, drd_f = _ret_bwd(nm("d_ret_f"), z, cos_r, sin_r, P["rd"][0], sv["stf"], dyraw, dz, reverse=False)
    dz, drd_b = _ret_bwd(nm("d_ret_b"), z, cos_r, sin_r, P["rd"][1], sv["stb"], dyraw, dz, reverse=True)
    g["ret_decay"] = jnp.stack([drd_f[:, 0, 0], drd_b[:, 0, 0]], axis=0)

    dh = _mm(nm("d_h"), dz, W["in"], tb=True)
    g["in"] = _mm(nm("dw_in"), sv["h"], dz, ta=True)
    dx, g["norm1_g"] = _tile_call(nm("d_norm1"), _norm_bwd, n, tm, [_rows(sv["x"]), _rows(dh), _rows(dx1)], [P["norm1_g"]],
                                  [(D_MODEL, F32)], acc_outs=[(1, D_MODEL)])
    return dx, g, side_res


INPUT_NAMES = ("x",) + WEIGHTS + ("loss_target",) + tuple("m_" + n for n in WEIGHTS) + tuple("v_" + n for n in WEIGHTS)
S5_NAMES = ("s5_a_re", "s5_a_im", "s5_log_dt", "s5_b_re", "s5_b_im", "s5_c_re", "s5_c_im")


def _pack_rows(arrays, tile_rows):
    pieces = []
    for a in arrays:
        flat = a.reshape(-1)
        pad = -flat.shape[0] % LANES
        if pad:
            flat = jnp.concatenate([flat, jnp.zeros((pad,), flat.dtype)])
        pieces.append(flat.reshape(-1, LANES))
    pad = -sum(p.shape[0] for p in pieces) % tile_rows
    if pad:
        pieces.append(jnp.zeros((pad, LANES), pieces[0].dtype))
    return jnp.concatenate(pieces, axis=0)


def _unpack_rows(packed, shapes):
    out, r0 = [], 0
    for s in shapes:
        size = math.prod(s)
        rows = -(-size // LANES)
        out.append(packed[r0:r0 + rows].reshape(-1)[:size].reshape(s))
        r0 += rows
    return out


FIRST = ("w_in", "mla_w_uq", "mla_w_ukv")
REST = ("s5_w_glu", "w_branch", "w_out", "ffn_w_gu", "ffn_w_down")


def _weights_first(full):
    return {"in": _perm(full["w_in"], _in_pieces()), "uq": _perm(full["mla_w_uq"], _uq_pieces()), "ukv": full["mla_w_ukv"]}


def _weights_rest(full):
    return dict(glu_a=full["s5_w_glu"][:, :1024], glu_b=full["s5_w_glu"][:, 1024:],
                branch=[full["w_branch"][i] for i in range(3)], out=full["w_out"],
                ffn_g=full["ffn_w_gu"][:, :FFN_HIDDEN], ffn_u=full["ffn_w_gu"][:, FFN_HIDDEN:], ffn_down=full["ffn_w_down"])


def _grads_first(g):
    return dict(w_in=_unperm(g["in"], _in_pieces()), mla_w_uq=_unperm(g["uq"], _uq_pieces()), mla_w_ukv=g["ukv"])


def _grads_rest(g):
    return dict(s5_w_glu=jnp.concatenate([g["glu_a"], g["glu_b"]], axis=1), w_branch=jnp.stack(g["branch"], axis=0),
                w_out=g["out"], ffn_w_gu=jnp.concatenate([g["ffn_g"], g["ffn_u"]], axis=1), ffn_w_down=g["ffn_down"])


def _local_step(inp, x, target, plan):
    n = x.shape[0]
    tables = _rope_tables(n)
    Ps, s5_vjps = [], []
    for l in range(DEPTH):
        s5, vjps = [], []
        for d in range(2):
            (arow, b_map, c_map), vjp = jax.vjp(_s5_maps, *[inp[k][l, d] for k in S5_NAMES])
            arow = lax.stop_gradient(arow)
            ak, apow = _power_tables(arow, False, d == 1)
            ak_adj, apow_adj = _power_tables(arow, True, d == 0)
            s5.append(dict(b_map=b_map, c_map=c_map, b_map_t=b_map.transpose(0, 2, 1), c_map_t=c_map.transpose(0, 2, 1),
                           ak=ak, apow=apow, ak_adj=ak_adj, apow_adj=apow_adj))
            vjps.append(vjp)
        s5_vjps.append(vjps)
        Ps.append(dict(
            norm1_g=inp["norm1_g"][l][None], norm2_g=inp["norm2_g"][l][None], ret_gn_g=inp["ret_gn_g"][l][None],
            mla_q_norm_g=inp["mla_q_norm_g"][l][None], mla_kv_norm_g=inp["mla_kv_norm_g"][l][None], s5_d=inp["s5_d"][l][None],
            rd=[jnp.broadcast_to(inp["ret_decay"][l, d][:, None, None], (RET_HEADS, 1, LANES)) for d in range(2)], s5=s5))

    h, saved, fwd_res = x, [], {}
    for l in range(DEPTH):
        sides = {d: plan["fwd_side"](l, d) for d in range(2)}

        def rest(res, l=l):
            fwd_res.update({(l, d): r for d, r in res.items()})
            return _weights_rest(plan["rest"](l, fwd_res))

        h, sv, _ = _layer_fwd(l, h, _weights_first(plan["first"](l, fwd_res)), rest, Ps[l], tables,
                              {d: s for d, s in sides.items() if s is not None})
        saved.append(sv)

    def loss_bwd(r, c, _):
        loss, vjp = jax.vjp(lambda a, gain: _f_loss(a, gain, r[1]), r[0], c[0])
        da, dg = vjp(jnp.ones((), F32))
        return [da], [dg, jnp.broadcast_to(loss, (1, LANES))]

    dh, g_final, loss_row = _tile_call("loss", loss_bwd, n, _pick(n, 256), [_rows(h), _rows(target)], [inp["final_g"][None]],
                                       [(D_MODEL, F32)], acc_outs=[(1, D_MODEL), (1, LANES)])
    layer_g, wgrads, bwd_res = [None] * DEPTH, [None] * DEPTH, {}
    for l in reversed(range(DEPTH)):
        dh, layer_g[l], res = _layer_bwd(l, dh, saved[l], Ps[l], tables, lambda d, g, l=l: plan["bwd_side"](l, d, g, wgrads))
        wgrads[l] = {**_grads_first(layer_g[l]), **_grads_rest(layer_g[l])}
        bwd_res.update({(l, d): r for d, r in res.items()})

    def stack(f):
        return jnp.stack([f(layer_g[l], l) for l in range(DEPTH)], axis=0)

    grads = dict(
        **{k: jnp.stack([wgrads[l][k] for l in range(DEPTH)], axis=0) for k in SHARDED},
        norm1_g=stack(lambda g, l: g["norm1_g"][0]), norm2_g=stack(lambda g, l: g["norm2_g"][0]),
        ret_gn_g=stack(lambda g, l: g["ret_gn_g"][0]), mla_q_norm_g=stack(lambda g, l: g["mla_q_norm_g"][0]),
        mla_kv_norm_g=stack(lambda g, l: g["mla_kv_norm_g"][0]), s5_d=stack(lambda g, l: g["s5_d"][0]),
        ret_decay=stack(lambda g, l: g["ret_decay"]), final_g=g_final[0])
    s5_grads = [[s5_vjps[l][d](layer_g[l]["s5"][d]) for d in range(2)] for l in range(DEPTH)]
    for i, k in enumerate(S5_NAMES):
        grads[k] = jnp.stack([jnp.stack([s5_grads[l][d][i] for d in range(2)], axis=0) for l in range(DEPTH)], axis=0)
    return loss_row[0, 0], dh, grads, wgrads, bwd_res


def kernel(*args):
    inp = dict(zip(INPUT_NAMES, args))
    kinds = ("grad_", "delta_", "new_m_", "new_v_")

    def local(l, names):
        return [inp[k][l].astype(BF16) for k in names]

    def whole(names, gathered):
        return {k: _unshard(g, SHARD_AXIS[k] - 1) for k, g in zip(names, gathered)}

    def parts(names, wg):
        return ("exchange", [_shard_split(wg[k], SHARD_AXIS[k] - 1).astype(BF16) for k in names])

    first0 = _all_gather("gather_weights0", local(0, FIRST))
    fwd_sides = {(0, 0): ("gather", local(0, REST)), (0, 1): ("gather", local(1, FIRST + REST))}

    def bwd_side(l, d, g, wgrads):
        if (l, d) == (0, 0):
            return parts(FIRST, wgrads[1])
        return parts(REST, _grads_rest(g)) if (l, d) in ((1, 0), (0, 1)) else None

    plan = dict(
        first=lambda l, res: whole(FIRST, first0 if l == 0 else res[(0, 1)][:len(FIRST)]),
        rest=lambda l, res: whole(REST, res[(0, 0)] if l == 0 else res[(0, 1)][len(FIRST):]),
        fwd_side=lambda l, d: fwd_sides.get((l, d)), bwd_side=bwd_side)

    loss, dh, grads, wgrads, landed = _local_step(inp, inp["x"][0], inp["loss_target"][0], plan)
    loss = lax.psum(loss, ("x", "y", "c"))

    landed[(0, 2)] = _exchange("exchange_grads0", parts(FIRST, wgrads[0]))
    by_layer = [dict(zip(FIRST + REST, landed[(0, 2)] + landed[(0, 1)])), dict(zip(FIRST + REST, landed[(0, 0)] + landed[(1, 0)]))]
    out = {}
    for k in SHARDED:
        land = jnp.stack([by_layer[0][k], by_layer[1][k]], axis=1)
        shape = inp[k].shape
        rows, cols = math.prod(shape[:-1]), shape[-1]
        res = _adamw("adamw_" + k, land.reshape(N_DEV, rows, cols), *[inp[p + k].reshape(rows, cols) for p in ("", "m_", "v_")],
                     _row_tile(rows))
        for kind, t in zip(kinds, res):
            out[kind + k] = t.reshape(shape)

    small_tr = 512
    partial = _pack_rows([grads[k] for k in SMALL], small_tr)
    packed = [_pack_rows([inp[p + k] for k in SMALL], small_tr) for p in ("", "m_", "v_")]
    res_small = _adamw("adamw_small", _all_gather("gather_small_grads", [partial])[0], *packed, small_tr)
    for kind, b in zip(kinds, res_small):
        for k, t in zip(SMALL, _unpack_rows(b, [inp[k].shape for k in SMALL])):
            out[kind + k] = t
    return (loss, dh[None]) + tuple(out[kind + k] for kind in kinds for k in WEIGHTS)
```

```python
import functools
import math

import jax
import jax.numpy as jnp
from jax import lax
from jax.experimental import pallas as pl
from jax.experimental.pallas import tpu as pltpu

F32 = jnp.float32
BF16 = jnp.bfloat16

D_MODEL = 1024
DEPTH = 2
RMS_EPS = 1e-6
GN_EPS = 1e-5
ROPE_THETA = 10000.0
RET_HEADS, RET_QK, RET_V, RET_CHUNK = 4, 128, 256, 128
MLA_HEADS, MLA_Q_LORA, MLA_KV_LORA, MLA_NOPE, MLA_ROPE, MLA_V = 8, 384, 256, 128, 64, 128
S5_GROUPS, S5_GROUP, S5_STATE = 64, 16, 64
S5_BLOCKS = 8
FFN_HIDDEN = 2816
N_DEV = 8
ADAM_LR, ADAM_B1, ADAM_B2, ADAM_EPS, ADAM_WD, ADAM_STEP = 0.001, 0.9, 0.999, 1e-08, 0.01, 10

LANES = 128
SUBLANES = 8
VMEM_LIMIT = 48 * 1024 * 1024

ZW = 8192
Z_RET = 0
Z_RG = 2048
Z_U = 3072
Z_GATE = 4096
Z_CQ = 7168
Z_CKV = 7680
Z_KR = 7936
IN_SPLITS = (512, 512, 1024, 1024, 384, 256, 64, 1024, 3072)

SHARDED = ("w_in", "mla_w_uq", "mla_w_ukv", "s5_w_glu", "w_branch", "w_out", "ffn_w_gu", "ffn_w_down")
SHARD_AXIS = {"w_in": 2, "mla_w_uq": 2, "mla_w_ukv": 2, "s5_w_glu": 2, "w_branch": 2, "w_out": 1,
              "ffn_w_gu": 2, "ffn_w_down": 1}
SMALL = ("norm1_g", "ret_decay", "ret_gn_g", "mla_q_norm_g", "mla_kv_norm_g", "s5_a_re", "s5_a_im", "s5_log_dt",
         "s5_b_re", "s5_b_im", "s5_c_re", "s5_c_im", "s5_d", "norm2_g", "final_g")
WEIGHTS = ("norm1_g", "w_in", "ret_decay", "ret_gn_g", "mla_q_norm_g", "mla_w_uq", "mla_kv_norm_g", "mla_w_ukv",
           "s5_a_re", "s5_a_im", "s5_log_dt", "s5_b_re", "s5_b_im", "s5_c_re", "s5_c_im", "s5_d", "s5_w_glu",
           "w_branch", "w_out", "norm2_g", "ffn_w_gu", "ffn_w_down", "final_g")


def _params(sem=None):
    return pltpu.CompilerParams(dimension_semantics=sem, vmem_limit_bytes=VMEM_LIMIT)


def _pick(n, cap):
    if n <= cap:
        return n
    t = cap - cap % LANES
    while t >= LANES:
        if n % t == 0:
            return t
        t -= LANES
    return n


@functools.partial(jax.custom_vjp, nondiff_argnums=(2, 3))
def _bdot(a, b, ca, cb):
    return lax.dot_general(a.astype(BF16), b.astype(BF16), (((ca,), (cb,)), ((), ())), preferred_element_type=F32)


def _bdot_fwd(a, b, ca, cb):
    return _bdot(a, b, ca, cb), (a, b)


def _bdot_bwd(ca, cb, res, g):
    a, b = res
    da = _bdot(g, b, 1, 1 - cb) if ca == 1 else _bdot(b, g, 1 - cb, 1)
    db = _bdot(a, g, 1 - ca, 0) if cb == 0 else _bdot(g, a, 0, 1 - ca)
    return da, db


_bdot.defvjp(_bdot_fwd, _bdot_bwd)


@jax.custom_vjp
def _swap_halves(x):
    return pltpu.roll(x, LANES // 2, 1)


_swap_halves.defvjp(lambda x: (_swap_halves(x), None), lambda _, g: (_swap_halves(g),))


def _rope(x, cosf, sinf):
    return x * cosf + _swap_halves(x) * sinf


def _f_rms(x, g):
    return x * lax.rsqrt(jnp.mean(x * x, axis=-1, keepdims=True) + RMS_EPS) * g


def _rope_t(g, cosf, sinf):
    return g * cosf + _swap_halves(g * sinf)


def _f_gn_gate(yh, rgh, gh):
    mu = jnp.mean(yh, axis=-1, keepdims=True)
    var = jnp.mean(jnp.square(yh - mu), axis=-1, keepdims=True)
    return jax.nn.silu(rgh) * ((yh - mu) * lax.rsqrt(var + GN_EPS) * gh)


MLA_SCALE = (MLA_NOPE + MLA_ROPE) ** -0.5


def _mla_q(qraw, cosf, sinf, transpose):
    parts = []
    for h in range(MLA_HEADS):
        parts.append(qraw[:, 256 * h:256 * h + 128] * MLA_SCALE)
        r = qraw[:, 256 * h + 128:256 * h + 256]
        parts.append(_rope_t(r * MLA_SCALE, cosf, sinf) if transpose else _rope(r, cosf, sinf) * MLA_SCALE)
    return jnp.concatenate(parts, axis=1)


def _f_s5_act(ysum, u, d):
    return jax.nn.gelu(ysum + d * u)


def _f_glu(ga, gb):
    return ga * jax.nn.sigmoid(gb)


def _f_gate(zg, proj):
    return jax.nn.sigmoid(zg) * proj


def _f_swiglu(gp, up):
    return jax.nn.silu(gp) * up


def _f_loss(x, g, target):
    y = _f_rms(x, g)
    err = jnp.square(y - target)
    return 0.5 * jnp.sum(jnp.mean(err, axis=-1))


def _tile_call(name, fn, n_rows, tm, row_ins, consts, row_outs, acc_outs=(), alias=None):
    n_row_in, n_const = len(row_ins), len(consts)
    args = [a for a, _, _ in row_ins] + list(consts)
    in_specs = [pl.BlockSpec((tm, w), lambda i, cb=cb: (i, cb)) for _, w, cb in row_ins]
    in_specs += [pl.BlockSpec(c.shape, lambda i: (0, 0)) for c in consts]
    out_shape, out_specs, aliases = [], [], {}
    if alias is not None:
        arr, w, cb = alias
        in_specs.append(pl.BlockSpec((tm, w), lambda i, cb=cb: (i, cb)))
        aliases[len(args)] = 0
        args.append(arr)
        out_shape.append(jax.ShapeDtypeStruct(arr.shape, arr.dtype))
        out_specs.append(pl.BlockSpec((tm, w), lambda i, cb=cb: (i, cb)))
    for w, dt in row_outs:
        out_shape.append(jax.ShapeDtypeStruct((n_rows, w), dt))
        out_specs.append(pl.BlockSpec((tm, w), lambda i: (i, 0)))
    n_row_out = len(out_shape)
    for r, w in acc_outs:
        out_shape.append(jax.ShapeDtypeStruct((r, w), F32))
        out_specs.append(pl.BlockSpec((r, w), lambda i: (0, 0)))
    n_in = len(args)

    def body(*refs):
        ins, outs = refs[:n_in], refs[n_in:]
        rows = [r[...] for r in ins[:n_row_in]]
        cons = [r[...] for r in ins[n_row_in:n_row_in + n_const]]
        prev = ins[-1][...] if alias is not None else None
        res_rows, res_accs = fn(rows, cons, prev)
        for o, r in zip(outs[:n_row_out], res_rows):
            o[...] = r.astype(o.dtype)
        if acc_outs:
            @pl.when(pl.program_id(0) == 0)
            def _():
                for o in outs[n_row_out:]:
                    o[...] = jnp.zeros(o.shape, F32)
            for o, r in zip(outs[n_row_out:], res_accs):
                o[...] += r

    return pl.pallas_call(
        body, name=name, grid=(n_rows // tm,), in_specs=in_specs, out_specs=out_specs, out_shape=out_shape,
        input_output_aliases=aliases, compiler_params=_params(("arbitrary",)))(*args)


def _mm(name, a, b, *, ta=False, tb=False, add=None, out_dtype=F32):
    (K, M) = a.shape if ta else a.shape[::-1]
    (N, K2) = b.shape if tb else b.shape[::-1]
    assert K == K2, (name, a.shape, b.shape)
    tm, tn, tk = _pick(M, 1536), _pick(N, 1536), _pick(K, 1024)
    gi, gj, nk = M // tm, N // tn, K // tk
    a_bytes, b_bytes = a.size * a.dtype.itemsize, b.size * b.dtype.itemsize
    j_outer = nk == 1 and b_bytes + a_bytes * gj < a_bytes + b_bytes * gi

    def im(f):
        return (lambda g0, g1, k: f(g1, g0, k)) if j_outer else f

    a_spec = pl.BlockSpec((tk, tm), im(lambda i, j, k: (k, i))) if ta else pl.BlockSpec((tm, tk), im(lambda i, j, k: (i, k)))
    b_spec = pl.BlockSpec((tn, tk), im(lambda i, j, k: (j, k))) if tb else pl.BlockSpec((tk, tn), im(lambda i, j, k: (k, j)))
    o_spec = pl.BlockSpec((tm, tn), im(lambda i, j, k: (i, j)))
    dn = (((0 if ta else 1,), (1 if tb else 0,)), ((), ()))
    has_add = add is not None

    def body(*refs):
        if has_add:
            a_ref, b_ref, add_ref, o_ref, acc = refs
        else:
            a_ref, b_ref, o_ref, acc = refs
        k = pl.program_id(2)

        @pl.when(k == 0)
        def _():
            acc[...] = jnp.zeros(acc.shape, F32)

        acc[...] += lax.dot_general(a_ref[...].astype(BF16), b_ref[...].astype(BF16), dn, preferred_element_type=F32)

        @pl.when(k == nk - 1)
        def _():
            r = acc[...]
            if has_add:
                r = r + add_ref[...]
            o_ref[...] = r.astype(out_dtype)

    args, specs = [a, b], [a_spec, b_spec]
    if has_add:
        args.append(add)
        specs.append(o_spec)
    return pl.pallas_call(
        body, name=name, grid=(gj, gi, nk) if j_outer else (gi, gj, nk), in_specs=specs, out_specs=o_spec,
        out_shape=jax.ShapeDtypeStruct((M, N), out_dtype), scratch_shapes=[pltpu.VMEM((tm, tn), F32)],
        compiler_params=_params(("parallel", "parallel", "arbitrary")))(*args)


S5_BW = 2 * S5_STATE * (S5_GROUPS // S5_BLOCKS)


def _scan_tile(buf, ak_ref, ap_ref, carry, *, reverse, x_ref=None, da_ref=None):
    tt, bw = buf.shape
    hw = bw // 2
    ng = tt // SUBLANES
    rowid = lax.broadcasted_iota(jnp.int32, (SUBLANES, hw), 0)
    steps = [(SUBLANES - k if reverse else k, SUBLANES * n) for n, k in enumerate((1, 2, 4))]
    apr, api = ap_ref[:, :hw], ap_ref[:, hw:]
    first = (rowid == SUBLANES - 1) if reverse else (rowid == 0)

    def group(gi, c):
        cr, ci = c
        r0 = pl.multiple_of(((ng - 1 - gi) if reverse else gi) * SUBLANES, SUBLANES)
        xr, xi = buf[pl.ds(r0, SUBLANES), :hw], buf[pl.ds(r0, SUBLANES), hw:]
        for sh, a0 in steps:
            kr, ki = ak_ref[a0:a0 + SUBLANES, :hw], ak_ref[a0:a0 + SUBLANES, hw:]
            sr, si = pltpu.roll(xr, sh, 0), pltpu.roll(xi, sh, 0)
            xr, xi = xr + kr * sr - ki * si, xi + kr * si + ki * sr
        xr, xi = xr + apr * cr - api * ci, xi + apr * ci + api * cr
        buf[pl.ds(r0, SUBLANES), :hw] = xr
        buf[pl.ds(r0, SUBLANES), hw:] = xi
        if x_ref is not None:
            sh1 = SUBLANES - 1 if reverse else 1
            pr = jnp.where(first, cr, pltpu.roll(xr, sh1, 0))
            pi = jnp.where(first, ci, pltpu.roll(xi, sh1, 0))
            sr, si = x_ref[pl.ds(r0, SUBLANES), :hw], x_ref[pl.ds(r0, SUBLANES), hw:]
            da_ref[:, :hw] += pr * sr + pi * si
            da_ref[:, hw:] += pi * sr - pr * si
        last = 0 if reverse else SUBLANES - 1
        return (jnp.broadcast_to(xr[last:last + 1], (SUBLANES, hw)), jnp.broadcast_to(xi[last:last + 1], (SUBLANES, hw)))

    cr, ci = lax.fori_loop(0, ng, group, (carry[:, :hw], carry[:, hw:]))
    carry[:, :hw] = cr
    carry[:, hw:] = ci


def _s5_specs(n_rows, reverse):
    tt = _pick(n_rows, 1024)
    nt = n_rows // tt

    def rows(width, off):
        return pl.BlockSpec((tt, width), lambda j, t: ((nt - 1 - t) if reverse else t, off + j))

    def per_block(r, c):
        return pl.BlockSpec((None, r, c), lambda j, t: (j, 0, 0))

    def par(r):
        return pl.BlockSpec((r, S5_BW), lambda j, t: (0, j))

    return tt, nt, rows, per_block, par


def _s5_call(name, core, args, in_specs, out_specs, out_shape, scratch, nt, side):
    n_out = len(out_shape)
    if side is None:
        body, sem = core, ("parallel", "arbitrary")
    else:
        make, shapes, sems = _exchange_parts(side)
        n, n_in, n_sc = len(shapes), len(args), len(scratch)
        hbm = pl.BlockSpec(memory_space=pl.ANY)

        def body(*refs):
            ins, xs = refs[:n_in], refs[n_in:n_in + n]
            outs, lands = refs[n_in + n:n_in + n + n_out], refs[n_in + n + n_out:n_in + 2 * n + n_out]
            sc = refs[n_in + 2 * n + n_out:n_in + 2 * n + n_out + n_sc]
            start, forward, finish = make(xs, lands, *refs[-3:])
            j, t = pl.program_id(0), pl.program_id(1)
            pl.when((j == 0) & (t == 0))(start)
            pl.when((j == S5_BLOCKS // 2) & (t == 0))(forward)
            core(*ins, *outs, *sc)
            pl.when((j == S5_BLOCKS - 1) & (t == nt - 1))(finish)

        args, in_specs = list(args) + list(side[1]), list(in_specs) + [hbm] * n
        out_specs, out_shape = list(out_specs) + [hbm] * n, list(out_shape) + shapes
        scratch, sem = list(scratch) + sems, ("arbitrary", "arbitrary")
    res = pl.pallas_call(body, name=name, grid=(S5_BLOCKS, nt), in_specs=in_specs, out_specs=out_specs, out_shape=out_shape,
                         scratch_shapes=scratch, compiler_params=_params(sem))(*args)
    return list(res[:n_out]) + [None if side is None else list(res[n_out:])]


def _s5_fwd(name, z, m, *, reverse, side=None):
    n_rows = z.shape[0]
    tt, nt, rows, per_block, par = _s5_specs(n_rows, reverse)

    def body(u_ref, b_ref, c_ref, ak_ref, ap_ref, x_ref, y_ref, carry):
        @pl.when(pl.program_id(1) == 0)
        def _():
            carry[...] = jnp.zeros(carry.shape, F32)

        x_ref[...] = jnp.dot(u_ref[...].astype(BF16), b_ref[...].astype(BF16), preferred_element_type=F32)
        _scan_tile(x_ref, ak_ref, ap_ref, carry, reverse=reverse)
        y_ref[...] = jnp.dot(x_ref[...].astype(BF16), c_ref[...].astype(BF16), preferred_element_type=F32)

    return _s5_call(
        name, body, [z, m["b_map"], m["c_map"], m["ak"], m["apow"]],
        [rows(LANES, Z_U // LANES), per_block(LANES, S5_BW), per_block(S5_BW, LANES), par(3 * SUBLANES), par(SUBLANES)],
        [rows(S5_BW, 0), rows(LANES, 0)],
        [jax.ShapeDtypeStruct((n_rows, S5_BLOCKS * S5_BW), F32), jax.ShapeDtypeStruct((n_rows, S5_BLOCKS * LANES), F32)],
        [pltpu.VMEM((SUBLANES, S5_BW), F32)], nt, side)


def _s5_bwd(name, z, dy, xs, m, *, reverse, side=None):
    n_rows = z.shape[0]
    tt, nt, rows, per_block, par = _s5_specs(n_rows, reverse)

    def body(u_ref, dy_ref, x_ref, bt_ref, ct_ref, ak_ref, ap_ref, du_ref, db_ref, dc_ref, da_ref, lam, carry):
        t = pl.program_id(1)

        @pl.when(t == 0)
        def _():
            carry[...] = jnp.zeros(carry.shape, F32)
            db_ref[...] = jnp.zeros(db_ref.shape, F32)
            dc_ref[...] = jnp.zeros(dc_ref.shape, F32)
            da_ref[...] = jnp.zeros(da_ref.shape, F32)

        dy_b = dy_ref[...].astype(BF16)
        lam[...] = jnp.dot(dy_b, ct_ref[...].astype(BF16), preferred_element_type=F32)
        _scan_tile(lam, ak_ref, ap_ref, carry, reverse=reverse, x_ref=x_ref, da_ref=da_ref)
        lam_b = lam[...].astype(BF16)
        du_ref[...] = jnp.dot(lam_b, bt_ref[...].astype(BF16), preferred_element_type=F32)
        db_ref[...] += lax.dot_general(u_ref[...].astype(BF16), lam_b, _TN, preferred_element_type=F32)
        dc_ref[...] += lax.dot_general(x_ref[...].astype(BF16), dy_b, _TN, preferred_element_type=F32)

        @pl.when(t == nt - 1)
        def _():
            da_ref[...] = jnp.broadcast_to(jnp.sum(da_ref[...], axis=0, keepdims=True), da_ref.shape)

    return _s5_call(
        name, body, [z, dy, xs, m["b_map_t"], m["c_map_t"], m["ak_adj"], m["apow_adj"]],
        [rows(LANES, Z_U // LANES), rows(LANES, 0), rows(S5_BW, 0), per_block(S5_BW, LANES), per_block(LANES, S5_BW),
         par(3 * SUBLANES), par(SUBLANES)],
        [rows(LANES, 0), per_block(LANES, S5_BW), per_block(S5_BW, LANES), par(SUBLANES)],
        [jax.ShapeDtypeStruct((n_rows, S5_BLOCKS * LANES), F32), jax.ShapeDtypeStruct((S5_BLOCKS, LANES, S5_BW), F32),
         jax.ShapeDtypeStruct((S5_BLOCKS, S5_BW, LANES), F32), jax.ShapeDtypeStruct((SUBLANES, S5_BLOCKS * S5_BW), F32)],
        [pltpu.VMEM((tt, S5_BW), F32), pltpu.VMEM((SUBLANES, S5_BW), F32)], nt, side)


def _ret_chunk(zq, zk, v, cosf, sinf, state, rd, reverse):
    c = RET_CHUNK
    lg = jax.nn.log_sigmoid(rd)
    lg1 = jnp.max(lg, axis=1, keepdims=True)
    q = _rope(zq, cosf, sinf) * (RET_QK ** -0.5)
    k = _rope(zk, cosf, sinf)
    pi = lax.broadcasted_iota(jnp.int32, (c, c), 0).astype(F32)
    pj = lax.broadcasted_iota(jnp.int32, (c, c), 1).astype(F32)
    pcol = lax.broadcasted_iota(jnp.int32, (c, 1), 0).astype(F32)
    if reverse:
        diff, mask, pos = pj - pi, pj > pi, (c - 1) - pcol
    else:
        diff, mask, pos = pi - pj, pi >= pj, pcol
    decay_in = jnp.where(mask, jnp.exp(jnp.where(mask, diff, 0.0) * lg), 0.0)
    scores = _bdot(q, k, 1, 1) * decay_in
    inner = _bdot(scores, v, 1, 0)
    k_w = jnp.exp((c - 1 - pos) * lg1)
    kv = _bdot(k * k_w, v, 0, 0)
    q_w = jnp.exp((pos + 1) * lg1)
    cross = _bdot(q, state, 1, 0) * q_w
    new_state = jnp.exp(c * lg1) * state + kv
    return inner + cross, new_state


RET_ZW = RET_HEADS * (2 * RET_QK + RET_V)


def _ret_specs(n_chunks, reverse_order):
    cmap = (lambda n: n_chunks - 1 - n) if reverse_order else (lambda n: n)
    z_spec = pl.BlockSpec((RET_CHUNK, RET_ZW), lambda n: (cmap(n), 0))
    t_spec = pl.BlockSpec((RET_CHUNK, LANES), lambda n: (cmap(n), 0))
    rd_spec = pl.BlockSpec((RET_HEADS, 1, LANES), lambda n: (0, 0, 0))
    o_spec = pl.BlockSpec((RET_CHUNK, RET_HEADS * RET_V), lambda n: (cmap(n), 0))
    st_spec = pl.BlockSpec((RET_HEADS, None, RET_QK, RET_V), lambda n: (0, cmap(n), 0, 0))
    return z_spec, t_spec, rd_spec, o_spec, st_spec


def _ret_head(zt, h):
    b = h * (2 * RET_QK + RET_V)
    return zt[:, b:b + RET_QK], zt[:, b + RET_QK:b + 2 * RET_QK], zt[:, b + 2 * RET_QK:b + 2 * RET_QK + RET_V]


def _ret_fwd(name, z, cosf, sinf, rd, *, reverse):
    n_rows = z.shape[0]
    n_chunks = n_rows // RET_CHUNK
    z_spec, t_spec, rd_spec, o_spec, st_spec = _ret_specs(n_chunks, reverse)

    def body(z_ref, cos_ref, sin_ref, rd_ref, o_ref, st_ref, state):
        @pl.when(pl.program_id(0) == 0)
        def _():
            state[...] = jnp.zeros(state.shape, F32)

        zt = z_ref[...]
        cosv, sinv = cos_ref[...], sin_ref[...]
        for h in range(RET_HEADS):
            st = state[h]
            st_ref[h] = st
            out, new = _ret_chunk(*_ret_head(zt, h), cosv, sinv, st, rd_ref[h], reverse)
            o_ref[:, RET_V * h:RET_V * (h + 1)] = out
            state[h] = new

    return pl.pallas_call(
        body, name=name, grid=(n_chunks,), in_specs=[z_spec, t_spec, t_spec, rd_spec],
        out_specs=[o_spec, st_spec],
        out_shape=[jax.ShapeDtypeStruct((n_rows, RET_HEADS * RET_V), F32),
                   jax.ShapeDtypeStruct((RET_HEADS, n_chunks, RET_QK, RET_V), F32)],
        scratch_shapes=[pltpu.VMEM((RET_HEADS, RET_QK, RET_V), F32)], compiler_params=_params(("arbitrary",)))(z, cosf, sinf, rd)


def _ret_bwd(name, z, cosf, sinf, rd, states, dout, dz, *, reverse):
    n_rows = z.shape[0]
    n_chunks = n_rows // RET_CHUNK
    z_spec, t_spec, rd_spec, o_spec, st_spec = _ret_specs(n_chunks, not reverse)

    def body(z_ref, cos_ref, sin_ref, rd_ref, st_ref, do_ref, dzin_ref, dz_ref, drd_ref, dstate):
        @pl.when(pl.program_id(0) == 0)
        def _():
            dstate[...] = jnp.zeros(dstate.shape, F32)
            drd_ref[...] = jnp.zeros(drd_ref.shape, F32)

        zt = z_ref[...]
        cosv, sinv = cos_ref[...], sin_ref[...]
        parts = []
        for h in range(RET_HEADS):
            _, vjp = jax.vjp(lambda a, b, c, s, r: _ret_chunk(a, b, c, cosv, sinv, s, r, reverse),
                             *_ret_head(zt, h), st_ref[h], rd_ref[h])
            dq, dk, dv, dst, drd = vjp((do_ref[:, RET_V * h:RET_V * (h + 1)], dstate[h]))
            parts += [dq, dk, dv]
            dstate[h] = dst
            drd_ref[h] += jnp.sum(drd, axis=1, keepdims=True)
        dz_ref[...] = (dzin_ref[...].astype(F32) + jnp.concatenate(parts, axis=1)).astype(dz_ref.dtype)

    return pl.pallas_call(
        body, name=name, grid=(n_chunks,),
        in_specs=[z_spec, t_spec, t_spec, rd_spec, st_spec, o_spec, z_spec],
        out_specs=[z_spec, rd_spec],
        out_shape=[jax.ShapeDtypeStruct(dz.shape, dz.dtype), jax.ShapeDtypeStruct((RET_HEADS, 1, LANES), F32)],
        input_output_aliases={6: 0}, scratch_shapes=[pltpu.VMEM((RET_HEADS, RET_QK, RET_V), F32)],
        compiler_params=_params(("arbitrary",)))(z, cosf, sinf, rd, states, dout, dz)


_NT = (((1,), (1,)), ((), ()))
_TN = (((0,), (0,)), ((), ()))


def _attn_tiles(n_rows, tq_cap):
    tq, tk = _pick(n_rows, tq_cap), _pick(n_rows, 2048)
    return tq, max(tq // 2, LANES), tk, min(tk, 1024)


def _attn_fwd(name, q, kv, kr):
    n_rows = q.shape[0]
    tq, hq, tk, sub = _attn_tiles(n_rows, 512)
    nk = n_rows // tk

    def body(q_ref, kn_ref, v_ref, kr_ref, o_ref, lse_ref, m_sc, acc):
        j = pl.program_id(2)

        @pl.when(j == 0)
        def _():
            m_sc[...] = jnp.full(m_sc.shape, -jnp.inf, F32)
            acc[...] = jnp.zeros(acc.shape, F32)

        for c in range(tk // sub):
            rows = slice(c * sub, (c + 1) * sub)
            k = jnp.concatenate([kn_ref[rows, :], kr_ref[rows, :]], axis=1)
            v1 = jnp.concatenate([v_ref[rows, :], jnp.ones((sub, LANES), BF16)], axis=1)
            for part in range(tq // hq):
                qr = slice(part * hq, (part + 1) * hq)
                s = lax.dot_general(q_ref[qr, :], k, _NT, preferred_element_type=F32)
                m_prev = m_sc[qr, :]
                m_new = jnp.maximum(m_prev, jnp.max(s, axis=1, keepdims=True))
                p = jnp.exp(s - m_new)
                acc[qr, :] = jnp.exp(m_prev - m_new) * acc[qr, :] + jnp.dot(p.astype(BF16), v1, preferred_element_type=F32)
                m_sc[qr, :] = m_new

        @pl.when(j == nk - 1)
        def _():
            l = acc[:, LANES:]
            o_ref[...] = acc[:, :LANES] / l
            lse_ref[...] = m_sc[...] + jnp.log(l)

    return pl.pallas_call(
        body, name=name, grid=(MLA_HEADS, n_rows // tq, nk),
        in_specs=[pl.BlockSpec((tq, 256), lambda h, i, j: (i, h)),
                  pl.BlockSpec((tk, 128), lambda h, i, j: (j, 2 * h)),
                  pl.BlockSpec((tk, 128), lambda h, i, j: (j, 2 * h + 1)),
                  pl.BlockSpec((tk, 128), lambda h, i, j: (j, 0))],
        out_specs=[pl.BlockSpec((tq, 128), lambda h, i, j: (i, h)),
                   pl.BlockSpec((None, tq, 128), lambda h, i, j: (h, i, 0))],
        out_shape=[jax.ShapeDtypeStruct((n_rows, MLA_HEADS * MLA_V), F32),
                   jax.ShapeDtypeStruct((MLA_HEADS, n_rows, LANES), F32)],
        scratch_shapes=[pltpu.VMEM((tq, 1), F32), pltpu.VMEM((tq, 2 * LANES), F32)],
        compiler_params=_params(("parallel", "parallel", "arbitrary")))(q, kv, kv, kr)


def _attn_bwd(name, q, kv, kr, o, lse, do):
    n_rows = q.shape[0]
    tq, hq, tk, sub = _attn_tiles(n_rows, 1024)
    nq = n_rows // tq

    def body(q_ref, kn_ref, v_ref, kr_ref, o_ref, lse_ref, do_ref, dq_ref, dkv_ref, dkr_ref, dk_acc, dv_acc):
        j, i = pl.program_id(1), pl.program_id(2)

        @pl.when(i == 0)
        def _():
            dk_acc[...] = jnp.zeros(dk_acc.shape, F32)
            dv_acc[...] = jnp.zeros(dv_acc.shape, F32)

        @pl.when((i == 0) & (j == 0))
        def _():
            dq_ref[...] = jnp.zeros(dq_ref.shape, F32)

        for part in range(tq // hq):
            qr = slice(part * hq, (part + 1) * hq)
            qv = q_ref[qr, :]
            do = do_ref[qr, :]
            do_b = do.astype(BF16)
            delta = jnp.sum(do * o_ref[qr, :], axis=1, keepdims=True)
            lse_col = lse_ref[qr, :][:, :1]
            dq = None
            for c in range(tk // sub):
                rows = slice(c * sub, (c + 1) * sub)
                k = jnp.concatenate([kn_ref[rows, :], kr_ref[rows, :]], axis=1)
                s = lax.dot_general(qv, k, _NT, preferred_element_type=F32)
                p = jnp.exp(s - lse_col)
                dp = lax.dot_general(do_b, v_ref[rows, :], _NT, preferred_element_type=F32)
                ds = (p * (dp - delta)).astype(BF16)
                dv_acc[rows, :] += lax.dot_general(p.astype(BF16), do_b, _TN, preferred_element_type=F32)
                dk_acc[rows, :] += lax.dot_general(ds, qv, _TN, preferred_element_type=F32)
                t = jnp.dot(ds, k, preferred_element_type=F32)
                dq = t if dq is None else dq + t
            r0 = pl.multiple_of(i * tq + part * hq, hq)
            dq_ref[pl.ds(r0, hq), :] += dq

        @pl.when(i == nq - 1)
        def _():
            dkv_ref[...] = jnp.concatenate([dk_acc[:, :128], dv_acc[...]], axis=1).astype(dkv_ref.dtype)
            dkr_ref[...] = dk_acc[:, 128:]

    return pl.pallas_call(
        body, name=name, grid=(MLA_HEADS, n_rows // tk, nq),
        in_specs=[pl.BlockSpec((tq, 256), lambda h, j, i: (i, h)),
                  pl.BlockSpec((tk, 128), lambda h, j, i: (j, 2 * h)),
                  pl.BlockSpec((tk, 128), lambda h, j, i: (j, 2 * h + 1)),
                  pl.BlockSpec((tk, 128), lambda h, j, i: (j, 0)),
                  pl.BlockSpec((tq, 128), lambda h, j, i: (i, h)),
                  pl.BlockSpec((None, tq, 128), lambda h, j, i: (h, i, 0)),
                  pl.BlockSpec((tq, 128), lambda h, j, i: (i, h))],
        out_specs=[pl.BlockSpec((n_rows, 256), lambda h, j, i: (0, h)),
                   pl.BlockSpec((tk, 256), lambda h, j, i: (j, h)),
                   pl.BlockSpec((tk, 128), lambda h, j, i: (j, h))],
        out_shape=[jax.ShapeDtypeStruct((n_rows, MLA_HEADS * 256), F32), jax.ShapeDtypeStruct((n_rows, MLA_HEADS * 256), BF16),
                   jax.ShapeDtypeStruct((n_rows, MLA_HEADS * 128), F32)],
        scratch_shapes=[pltpu.VMEM((tk, 256), F32), pltpu.VMEM((tk, 128), F32)],
        compiler_params=_params(("parallel", "arbitrary", "arbitrary")))(q, kv, kv, kr, o, lse, do)


_MESH = pl.DeviceIdType.MESH


def _gather_stages(x_refs, out_refs, send_sems, recv_sems, local_sems):
    n = len(x_refs)
    mx, my, mc = lax.axis_index("x"), lax.axis_index("y"), lax.axis_index("c")
    me, sibling = (mx, my, mc), (mx, my, 1 - mc)
    chips = [(1 - mx, my), (mx, 1 - my), (1 - mx, 1 - my)]

    def copy(a, k, block, to, src=None):
        dst = out_refs[a].at[4 * block[0] + 2 * block[1] + block[2]]
        return pltpu.make_async_remote_copy(
            src_ref=dst if src is None else src, dst_ref=dst, send_sem=send_sems.at[7 * a + k],
            recv_sem=recv_sems.at[7 * a + k], device_id=to, device_id_type=_MESH)

    def mine():
        return [pltpu.make_async_copy(x_refs[a], out_refs[a].at[4 * mx + 2 * my + mc], local_sems.at[a]) for a in range(n)]

    def first():
        out = []
        for a in range(n):
            out.append(copy(a, 0, me, sibling, src=x_refs[a]))
            out += [copy(a, 1 + j, me, (*chip, mc), src=x_refs[a]) for j, chip in enumerate(chips)]
        return out

    def passed():
        return [copy(a, 4 + j, (*chip, mc), sibling) for j, chip in enumerate(chips) for a in range(n)]

    def start():
        for cp in mine() + first():
            cp.start()

    def forward():
        for j, chip in enumerate(chips):
            for a in range(n):
                copy(a, 1 + j, (*chip, mc), me).wait_recv()
        for cp in passed():
            cp.start()

    def finish():
        for a in range(n):
            copy(a, 0, sibling, me).wait_recv()
            for j, chip in enumerate(chips):
                copy(a, 4 + j, (*chip, 1 - mc), me).wait_recv()
        for cp in first() + passed():
            cp.wait_send()
        for cp in mine():
            cp.wait()

    return start, forward, finish


def _exchange_stages(g_refs, land_refs, send_sems, recv_sems, local_sems):
    n = len(g_refs)
    mx, my, mc = lax.axis_index("x"), lax.axis_index("y"), lax.axis_index("c")
    me = 4 * mx + 2 * my + mc

    def mine():
        return [pltpu.make_async_copy(g_refs[a].at[me], land_refs[a].at[me], local_sems.at[a]) for a in range(n)]

    def copies():
        out = []
        for k in range(1, N_DEV):
            px = 1 - mx if k & 4 else mx
            py = 1 - my if k & 2 else my
            pc = 1 - mc if k & 1 else mc
            peer = 4 * px + 2 * py + pc
            for a in range(n):
                sems = dict(send_sem=send_sems.at[7 * a + k - 1], recv_sem=recv_sems.at[7 * a + k - 1],
                            device_id=(px, py, pc), device_id_type=_MESH)
                out.append((pltpu.make_async_remote_copy(src_ref=g_refs[a].at[peer], dst_ref=land_refs[a].at[me], **sems),
                            pltpu.make_async_remote_copy(src_ref=g_refs[a].at[peer], dst_ref=land_refs[a].at[peer], **sems)))
        return out

    def start():
        for cp in mine():
            cp.start()
        for send, _ in copies():
            send.start()

    def finish():
        both = copies()
        for _, recv in both:
            recv.wait_recv()
        for send, _ in both:
            send.wait_send()
        for cp in mine():
            cp.wait()

    return start, lambda: None, finish


def _exchange_parts(side):
    kind, arrays = side
    n = len(arrays)
    if kind == "gather":
        make, shapes = _gather_stages, [jax.ShapeDtypeStruct((N_DEV,) + a.shape, a.dtype) for a in arrays]
    else:
        make, shapes = _exchange_stages, [jax.ShapeDtypeStruct(a.shape, a.dtype) for a in arrays]
    sems = [pltpu.SemaphoreType.DMA((7 * n,)), pltpu.SemaphoreType.DMA((7 * n,)), pltpu.SemaphoreType.DMA((n,))]
    return make, shapes, sems


def _exchange(name, side):
    make, shapes, sems = _exchange_parts(side)
    n = len(shapes)

    def body(*refs):
        start, forward, finish = make(refs[:n], refs[n:2 * n], *refs[2 * n:])
        start()
        forward()
        finish()

    hbm = pl.BlockSpec(memory_space=pl.ANY)
    return list(pl.pallas_call(body, name=name, out_shape=shapes, in_specs=[hbm] * n, out_specs=[hbm] * n,
                               scratch_shapes=sems)(*side[1]))


def _all_gather(name, xs):
    return _exchange(name, ("gather", xs))


def _all_to_all(name, gs):
    return _exchange(name, ("exchange", gs))


def _adamw(name, parts, w, m, v, tr):
    rows, cols = w.shape

    def body(p_ref, w_ref, m_ref, v_ref, g_ref, d_ref, nm_ref, nv_ref):
        g = p_ref[0].astype(F32)
        for d in range(1, N_DEV):
            g = g + p_ref[d].astype(F32)
        nm = ADAM_B1 * m_ref[...] + (1.0 - ADAM_B1) * g
        nv = ADAM_B2 * v_ref[...] + (1.0 - ADAM_B2) * jnp.square(g)
        m_hat = nm / (1.0 - ADAM_B1 ** ADAM_STEP)
        v_hat = nv / (1.0 - ADAM_B2 ** ADAM_STEP)
        g_ref[...] = g
        d_ref[...] = -ADAM_LR * (m_hat / (jnp.sqrt(v_hat) + ADAM_EPS) + ADAM_WD * w_ref[...])
        nm_ref[...] = nm
        nv_ref[...] = nv

    spec = pl.BlockSpec((tr, cols), lambda i: (i, 0))
    return pl.pallas_call(
        body, name=name, grid=(rows // tr,),
        in_specs=[pl.BlockSpec((N_DEV, tr, cols), lambda i: (0, i, 0)), spec, spec, spec],
        out_specs=[spec] * 4, out_shape=[jax.ShapeDtypeStruct((rows, cols), F32)] * 4,
        compiler_params=_params(("parallel",)))(parts, w, m, v)


def _in_pieces():
    p = []
    for h in range(RET_HEADS):
        p += [(128 * h, 128 * h + 128), (512 + 128 * h, 512 + 128 * h + 128), (1024 + 256 * h, 1024 + 256 * h + 256)]
    p += [(2048, 3072), (3776, 4800), (4800, 7872), (3072, 3456), 128, (3456, 3712),
          (3712, 3744), 32, (3744, 3776), 32, 128]
    return p


def _uq_pieces():
    p = []
    for h in range(MLA_HEADS):
        b = 192 * h
        p += [(b, b + 128), (b + 128, b + 160), 32, (b + 160, b + 192), 32]
    return p


def _perm(w, pieces):
    cols = [jnp.zeros(w.shape[:-1] + (p,), w.dtype) if isinstance(p, int) else w[..., p[0]:p[1]] for p in pieces]
    return jnp.concatenate(cols, axis=-1)


def _unperm(dw, pieces):
    found, off = [], 0
    for p in pieces:
        if isinstance(p, int):
            off += p
        else:
            found.append((p[0], dw[..., off:off + p[1] - p[0]]))
            off += p[1] - p[0]
    return jnp.concatenate([t for _, t in sorted(found, key=lambda s: s[0])], axis=-1)


def _unshard(blocks, axis):
    return jnp.concatenate([blocks[p] for p in range(N_DEV)], axis=axis)


def _shard_split(full, axis):
    return jnp.stack(jnp.split(full, N_DEV, axis=axis), axis=0)


def _row_tile(rows, cap=256, unit=16):
    return max(t for t in range(unit, cap + 1, unit) if rows % t == 0)


def _rope_tables(seq):
    pos = jnp.arange(seq, dtype=F32)[:, None]

    def table(dim):
        inv = 1.0 / (ROPE_THETA ** (jnp.arange(0, dim, 2, dtype=F32) / dim))
        ang = pos * inv[None, :]
        return jnp.cos(ang), jnp.sin(ang)

    cr, sr = table(RET_QK)
    cm, sm = table(MLA_ROPE)
    z = jnp.zeros_like(cm)
    return (jnp.concatenate([cr, cr], 1), jnp.concatenate([-sr, sr], 1),
            jnp.concatenate([cm, z, cm, z], 1), jnp.concatenate([-sm, z, sm, z], 1))


def _s5_maps(a_re, a_im, log_dt, b_re, b_im, c_re, c_im):
    dt = jnp.exp(log_dt)[:, None]
    ar = jnp.minimum(a_re, -1e-4)
    mag = jnp.exp(dt * ar)
    abar_re = mag * jnp.cos(dt * a_im)
    abar_im = mag * jnp.sin(dt * a_im)
    den = ar * ar + a_im * a_im
    nr = abar_re - 1.0
    ni = abar_im
    coef_re = (nr * ar + ni * a_im) / den
    coef_im = (ni * ar - nr * a_im) / den
    bb_re = coef_re[..., None] * b_re - coef_im[..., None] * b_im
    bb_im = coef_re[..., None] * b_im + coef_im[..., None] * b_re
    eye = jnp.eye(S5_BLOCKS, dtype=F32)

    def in_blocks(bb):
        t = bb.transpose(0, 2, 1).reshape(S5_BLOCKS, 8, S5_GROUP, S5_STATE)
        return jnp.einsum('jgcp,gh->jgchp', t, eye).reshape(S5_BLOCKS, 128, 512)

    def out_blocks(cc):
        t = cc.transpose(0, 2, 1).reshape(S5_BLOCKS, 8, S5_STATE, S5_GROUP)
        return jnp.einsum('jgpc,gh->jgphc', t, eye).reshape(S5_BLOCKS, 512, 128)

    arow = jnp.concatenate([abar_re.reshape(S5_BLOCKS, 512), abar_im.reshape(S5_BLOCKS, 512)], axis=1).reshape(1, -1)
    b_map = jnp.concatenate([in_blocks(bb_re), in_blocks(bb_im)], axis=2)
    c_map = jnp.concatenate([out_blocks(c_re), -out_blocks(c_im)], axis=1)
    return arow, b_map, c_map


def _power_tables(arow, conj, reverse):
    a = arow.reshape(S5_BLOCKS, 2, 512)
    ar, ai = a[:, 0], (-a[:, 1] if conj else a[:, 1])
    pw = [(ar, ai)]
    for _ in range(SUBLANES - 1):
        pr, pi = pw[-1]
        pw.append((pr * ar - pi * ai, pr * ai + pi * ar))

    def rows(sel):
        return jnp.stack([jnp.stack(list(pw[i]), axis=1) for i in sel], axis=0).reshape(len(sel), -1)

    rowid = jnp.arange(SUBLANES)[:, None]
    ak = jnp.concatenate([jnp.where((rowid < SUBLANES - k) if reverse else (rowid >= k), rows([k - 1]), 0.0)
                          for k in (1, 2, 4)], axis=0)
    order = list(range(SUBLANES))
    apow = rows(order[::-1] if reverse else order)
    return ak, apow


def _rows(arr):
    return (arr, arr.shape[1], 0)


def _vjp_rows(f, n_prim):
    def fn(r, c, _):
        _, vjp = jax.vjp(f, *r[:n_prim])
        return list(vjp(r[n_prim])), []
    return fn


def _norm_bwd(r, c, _):
    _, vjp = jax.vjp(_f_rms, r[0], c[0])
    dx, dg = vjp(r[1])
    return [dx + r[2]], [dg]


def _layer_fwd(l, x, W_first, W_rest, P, T, sides):
    W = dict(W_first)
    n = x.shape[0]
    tm, tmw = _pick(n, 512), _pick(n, 256)
    cos_r, sin_r, cos_m, sin_m = T

    def nm(s):
        return f"l{l}_{s}"

    def one(name, f, rows, consts, width, dtype=BF16, tile=tm):
        return _tile_call(nm(name), lambda r, c, _: ([f(r, c)], []), n, tile, rows, consts, [(width, dtype)])[0]

    h = one("norm1", lambda r, c: _f_rms(r[0], c[0]), [_rows(x)], [P["norm1_g"]], D_MODEL)
    z = _mm(nm("in_proj"), h, W["in"])
    of, stf = _ret_fwd(nm("ret_f"), z, cos_r, sin_r, P["rd"][0], reverse=False)
    ob, stb = _ret_fwd(nm("ret_b"), z, cos_r, sin_r, P["rd"][1], reverse=True)

    def gn(r, c, _):
        yraw = r[0] + r[1]
        ys = [_f_gn_gate(yraw[:, RET_V * i:RET_V * (i + 1)], r[2][:, RET_V * i:RET_V * (i + 1)],
                         c[0][:, RET_V * i:RET_V * (i + 1)]) for i in range(RET_HEADS)]
        return [yraw, jnp.concatenate(ys, axis=1)], []

    yraw, yret = _tile_call(nm("ret_gn"), gn, n, tm, [_rows(of), _rows(ob), (z, 1024, Z_RG // 1024)], [P["ret_gn_g"]],
                            [(1024, F32), (1024, BF16)])

    cqn = one("q_norm", lambda r, c: _f_rms(r[0][:, :MLA_Q_LORA], c[0]), [(z, 512, Z_CQ // 512)], [P["mla_q_norm_g"]], MLA_Q_LORA)
    ckvn = one("kv_norm", lambda r, c: _f_rms(r[0], c[0]), [(z, 256, Z_CKV // 256)], [P["mla_kv_norm_g"]], MLA_KV_LORA)
    qraw = _mm(nm("q_up"), cqn, W["uq"])
    kv = _mm(nm("kv_up"), ckvn, W["ukv"], out_dtype=BF16)
    q = one("q_rope", lambda r, c: _mla_q(r[0], r[1], r[2], False), [_rows(qraw), _rows(cos_m), _rows(sin_m)], [], 2048)
    kr = one("k_rope", lambda r, c: _rope(r[0], r[1], r[2]), [(z, 128, Z_KR // 128), _rows(cos_m), _rows(sin_m)], [], 128)
    o, lse = _attn_fwd(nm("attn"), q, kv, kr)

    xs, y_dir, side_res = [], [], {}
    for d in range(2):
        x_d, y_d, side_res[d] = _s5_fwd(nm(f"s5_scan{d}"), z, P["s5"][d], reverse=(d == 1), side=sides.get(d))
        xs.append(x_d)
        y_dir.append(y_d)
    W.update(W_rest(side_res))

    def s5_act(r, c, _):
        ysum = r[0] + r[1]
        return [ysum, _f_s5_act(ysum, r[2], c[0])], []

    ysum, gact = _tile_call(nm("s5_act"), s5_act, n, tm, [_rows(y_dir[0]), _rows(y_dir[1]), (z, 1024, Z_U // 1024)], [P["s5_d"]],
                            [(1024, F32), (1024, BF16)])
    ga = _mm(nm("glu_a"), gact, W["glu_a"])
    gb = _mm(nm("glu_b"), gact, W["glu_b"])
    ys5 = one("glu", lambda r, c: _f_glu(r[0], r[1]), [_rows(ga), _rows(gb)], [], 1024)

    ys = (yret, o, ys5)
    ps = [_mm(nm(f"branch{i}"), ys[i], W["branch"][i]) for i in range(3)]
    mix = one("mix", lambda r, c: _f_gate(r[0], r[3]) + _f_gate(r[1], r[4]) + _f_gate(r[2], r[5]),
              [(z, 1024, Z_GATE // 1024 + i) for i in range(3)] + [_rows(p) for p in ps], [], 1024)
    x1 = _mm(nm("out_proj"), mix, W["out"], add=x)
    h2 = one("norm2", lambda r, c: _f_rms(r[0], c[0]), [_rows(x1)], [P["norm2_g"]], D_MODEL)
    gp = _mm(nm("ffn_g"), h2, W["ffn_g"])
    up = _mm(nm("ffn_u"), h2, W["ffn_u"])
    act = one("swiglu", lambda r, c: _f_swiglu(r[0], r[1]), [_rows(gp), _rows(up)], [], FFN_HIDDEN, tile=tmw)
    x2 = _mm(nm("ffn_down"), act, W["ffn_down"], add=x1)
    saved = dict(x=x, h=h, z=z, stf=stf, stb=stb, yraw=yraw, ys=ys, cqn=cqn, ckvn=ckvn, qraw=qraw, q=q, kv=kv, kr=kr,
                 lse=lse, xs=xs, ysum=ysum, gact=gact, ga=ga, gb=gb, ps=ps, mix=mix, x1=x1, h2=h2, gp=gp, up=up, act=act, W=W)
    return x2, saved, side_res


def _layer_bwd(l, dx2, sv, P, T, side_of):
    n = dx2.shape[0]
    tm, tmw = _pick(n, 512), _pick(n, 256)
    cos_r, sin_r, cos_m, sin_m = T
    z, W = sv["z"], sv["W"]
    g = {}

    def nm(s):
        return f"l{l}_{s}"

    dact = _mm(nm("d_act"), dx2, W["ffn_down"], tb=True)
    g["ffn_down"] = _mm(nm("dw_ffn_down"), sv["act"], dx2, ta=True)
    dgp, dup = _tile_call(nm("d_swiglu"), _vjp_rows(_f_swiglu, 2), n, tmw, [_rows(sv["gp"]), _rows(sv["up"]), _rows(dact)], [],
                          [(FFN_HIDDEN, BF16)] * 2)
    dh2 = _mm(nm("d_h2_g"), dgp, W["ffn_g"], tb=True)
    dh2 = _mm(nm("d_h2_u"), dup, W["ffn_u"], tb=True, add=dh2)
    g["ffn_g"] = _mm(nm("dw_ffn_g"), sv["h2"], dgp, ta=True)
    g["ffn_u"] = _mm(nm("dw_ffn_u"), sv["h2"], dup, ta=True)
    dx1, g["norm2_g"] = _tile_call(nm("d_norm2"), _norm_bwd, n, tm, [_rows(sv["x1"]), _rows(dh2), _rows(dx2)], [P["norm2_g"]],
                                   [(D_MODEL, F32)], acc_outs=[(1, D_MODEL)])

    dmix = _mm(nm("d_mix"), dx1, W["out"], tb=True)
    g["out"] = _mm(nm("dw_out"), sv["mix"], dx1, ta=True)
    dz = jnp.zeros((n, ZW), BF16)
    dys, g["branch"] = [], []
    for i in range(3):
        dz, dp = _tile_call(nm(f"d_gate{i}"), _vjp_rows(_f_gate, 2), n, tm,
                            [(z, 1024, Z_GATE // 1024 + i), _rows(sv["ps"][i]), _rows(dmix)], [], [(1024, BF16)],
                            alias=(dz, 1024, Z_GATE // 1024 + i))
        dys.append(_mm(nm(f"d_branch{i}"), dp, W["branch"][i], tb=True))
        g["branch"].append(_mm(nm(f"dw_branch{i}"), sv["ys"][i], dp, ta=True))

    dga, dgb = _tile_call(nm("d_glu"), _vjp_rows(_f_glu, 2), n, tm, [_rows(sv["ga"]), _rows(sv["gb"]), _rows(dys[2])], [],
                          [(1024, BF16)] * 2)
    dgact = _mm(nm("d_gact_a"), dga, W["glu_a"], tb=True)
    dgact = _mm(nm("d_gact_b"), dgb, W["glu_b"], tb=True, add=dgact)
    g["glu_a"] = _mm(nm("dw_glu_a"), sv["gact"], dga, ta=True)
    g["glu_b"] = _mm(nm("dw_glu_b"), sv["gact"], dgb, ta=True)

    def act_bwd(r, c, _):
        _, vjp = jax.vjp(_f_s5_act, r[0], r[1], c[0])
        dy, du, dd = vjp(r[2])
        return [dy, du], [dd]

    dysum, du_part, g["s5_d"] = _tile_call(nm("d_s5_act"), act_bwd, n, tm,
                                           [_rows(sv["ysum"]), (z, 1024, Z_U // 1024), _rows(dgact)], [P["s5_d"]],
                                           [(1024, F32)] * 2, acc_outs=[(1, 1024)])
    dus, g["s5"] = [], []
    side_res = {}
    for d in range(2):
        du, g_b, g_c, da, side_res[d] = _s5_bwd(nm(f"d_s5_scan{d}"), z, dysum, sv["xs"][d], P["s5"][d], reverse=(d == 0),
                                                side=side_of(d, g))
        dus.append(du)
        g["s5"].append((da[:1], g_b, g_c))
    dz, = _tile_call(nm("d_s5_u"), lambda r, c, _: ([r[0] + r[1] + r[2]], []), n, tm,
                     [_rows(du_part), _rows(dus[0]), _rows(dus[1])], [], [], alias=(dz, 1024, Z_U // 1024))

    o = sv["ys"][1]
    dq, dkv, dkr = _attn_bwd(nm("d_attn"), sv["q"], sv["kv"], sv["kr"], o, sv["lse"], dys[1])
    dqraw, = _tile_call(nm("d_q_rope"), lambda r, c, _: ([_mla_q(r[0], r[1], r[2], True)], []), n, tm,
                        [_rows(dq), _rows(cos_m), _rows(sin_m)], [], [(2048, BF16)])

    def kr_bwd(r, c, _):
        tot = r[0][:, :128]
        for h in range(1, MLA_HEADS):
            tot = tot + r[0][:, 128 * h:128 * (h + 1)]
        return [_rope_t(tot, r[1], r[2])], []

    dz, = _tile_call(nm("d_k_rope"), kr_bwd, n, tm, [_rows(dkr), _rows(cos_m), _rows(sin_m)], [], [],
                     alias=(dz, 128, Z_KR // 128))
    dcqn = _mm(nm("d_cqn"), dqraw, W["uq"], tb=True)
    g["uq"] = _mm(nm("dw_uq"), sv["cqn"], dqraw, ta=True)
    dckvn = _mm(nm("d_ckvn"), dkv, W["ukv"], tb=True)
    g["ukv"] = _mm(nm("dw_ukv"), sv["ckvn"], dkv, ta=True)

    def qn_bwd(r, c, _):
        _, vjp = jax.vjp(_f_rms, r[0][:, :MLA_Q_LORA], c[0])
        da, dg = vjp(r[1])
        return [jnp.concatenate([da, jnp.zeros((da.shape[0], 512 - MLA_Q_LORA), F32)], axis=1)], [dg]

    dz, g["mla_q_norm_g"] = _tile_call(nm("d_q_norm"), qn_bwd, n, tm, [(z, 512, Z_CQ // 512), _rows(dcqn)], [P["mla_q_norm_g"]],
                                       [], acc_outs=[(1, MLA_Q_LORA)], alias=(dz, 512, Z_CQ // 512))

    def kvn_bwd(r, c, _):
        _, vjp = jax.vjp(_f_rms, r[0], c[0])
        da, dg = vjp(r[1])
        return [da], [dg]

    dz, g["mla_kv_norm_g"] = _tile_call(nm("d_kv_norm"), kvn_bwd, n, tm, [(z, 256, Z_CKV // 256), _rows(dckvn)],
                                        [P["mla_kv_norm_g"]], [], acc_outs=[(1, MLA_KV_LORA)], alias=(dz, 256, Z_CKV // 256))

    def gn_bwd(r, c, _):
        drg, dy, dg = [], [], []
        for i in range(RET_HEADS):
            sl = slice(RET_V * i, RET_V * (i + 1))
            _, vjp = jax.vjp(_f_gn_gate, r[0][:, sl], r[1][:, sl], c[0][:, sl])
            a, b, e = vjp(r[2][:, sl])
            dy.append(a)
            drg.append(b)
            dg.append(e)
        return [jnp.concatenate(drg, axis=1), jnp.concatenate(dy, axis=1)], [jnp.concatenate(dg, axis=1)]

    dz, dyraw, g["ret_gn_g"] = _tile_call(nm("d_ret_gn"), gn_bwd, n, tm,
                                          [_rows(sv["yraw"]), (z, 1024, Z_RG // 1024), _rows(dys[0])], [P["ret_gn_g"]],
                                          [(1024, F32)], acc_outs=[(1, 1024)], alias=(dz, 1024, Z_RG // 1024))
    dz, drd_f = _ret_bwd(nm("d_ret_f"), z, cos_r, sin_r, P["rd"][0], sv["stf"], dyraw, dz, reverse=False)
    dz, drd_b = _ret_bwd(nm("d_ret_b"), z, cos_r, sin_r, P["rd"][1], sv["stb"], dyraw, dz, reverse=True)
    g["ret_decay"] = jnp.stack([drd_f[:, 0, 0], drd_b[:, 0, 0]], axis=0)

    dh = _mm(nm("d_h"), dz, W["in"], tb=True)
    g["in"] = _mm(nm("dw_in"), sv["h"], dz, ta=True)
    dx, g["norm1_g"] = _tile_call(nm("d_norm1"), _norm_bwd, n, tm, [_rows(sv["x"]), _rows(dh), _rows(dx1)], [P["norm1_g"]],
                                  [(D_MODEL, F32)], acc_outs=[(1, D_MODEL)])
    return dx, g, side_res


INPUT_NAMES = ("x",) + WEIGHTS + ("loss_target",) + tuple("m_" + n for n in WEIGHTS) + tuple("v_" + n for n in WEIGHTS)
S5_NAMES = ("s5_a_re", "s5_a_im", "s5_log_dt", "s5_b_re", "s5_b_im", "s5_c_re", "s5_c_im")


def _pack_rows(arrays, tile_rows):
    pieces = []
    for a in arrays:
        flat = a.reshape(-1)
        pad = -flat.shape[0] % LANES
        if pad:
            flat = jnp.concatenate([flat, jnp.zeros((pad,), flat.dtype)])
        pieces.append(flat.reshape(-1, LANES))
    pad = -sum(p.shape[0] for p in pieces) % tile_rows
    if pad:
        pieces.append(jnp.zeros((pad, LANES), pieces[0].dtype))
    return jnp.concatenate(pieces, axis=0)


def _unpack_rows(packed, shapes):
    out, r0 = [], 0
    for s in shapes:
        size = math.prod(s)
        rows = -(-size // LANES)
        out.append(packed[r0:r0 + rows].reshape(-1)[:size].reshape(s))
        r0 += rows
    return out


FIRST = ("w_in", "mla_w_uq", "mla_w_ukv")
REST = ("s5_w_glu", "w_branch", "w_out", "ffn_w_gu", "ffn_w_down")


def _weights_first(full):
    return {"in": _perm(full["w_in"], _in_pieces()), "uq": _perm(full["mla_w_uq"], _uq_pieces()), "ukv": full["mla_w_ukv"]}


def _weights_rest(full):
    return dict(glu_a=full["s5_w_glu"][:, :1024], glu_b=full["s5_w_glu"][:, 1024:],
                branch=[full["w_branch"][i] for i in range(3)], out=full["w_out"],
                ffn_g=full["ffn_w_gu"][:, :FFN_HIDDEN], ffn_u=full["ffn_w_gu"][:, FFN_HIDDEN:], ffn_down=full["ffn_w_down"])


def _grads_first(g):
    return dict(w_in=_unperm(g["in"], _in_pieces()), mla_w_uq=_unperm(g["uq"], _uq_pieces()), mla_w_ukv=g["ukv"])


def _grads_rest(g):
    return dict(s5_w_glu=jnp.concatenate([g["glu_a"], g["glu_b"]], axis=1), w_branch=jnp.stack(g["branch"], axis=0),
                w_out=g["out"], ffn_w_gu=jnp.concatenate([g["ffn_g"], g["ffn_u"]], axis=1), ffn_w_down=g["ffn_down"])


def _local_step(inp, x, target, plan):
    n = x.shape[0]
    tables = _rope_tables(n)
    Ps, s5_vjps = [], []
    for l in range(DEPTH):
        s5, vjps = [], []
        for d in range(2):
            (arow, b_map, c_map), vjp = jax.vjp(_s5_maps, *[inp[k][l, d] for k in S5_NAMES])
            arow = lax.stop_gradient(arow)
            ak, apow = _power_tables(arow, False, d == 1)
            ak_adj, apow_adj = _power_tables(arow, True, d == 0)
            s5.append(dict(b_map=b_map, c_map=c_map, b_map_t=b_map.transpose(0, 2, 1), c_map_t=c_map.transpose(0, 2, 1),
                           ak=ak, apow=apow, ak_adj=ak_adj, apow_adj=apow_adj))
            vjps.append(vjp)
        s5_vjps.append(vjps)
        Ps.append(dict(
            norm1_g=inp["norm1_g"][l][None], norm2_g=inp["norm2_g"][l][None], ret_gn_g=inp["ret_gn_g"][l][None],
            mla_q_norm_g=inp["mla_q_norm_g"][l][None], mla_kv_norm_g=inp["mla_kv_norm_g"][l][None], s5_d=inp["s5_d"][l][None],
            rd=[jnp.broadcast_to(inp["ret_decay"][l, d][:, None, None], (RET_HEADS, 1, LANES)) for d in range(2)], s5=s5))

    h, saved, fwd_res = x, [], {}
    for l in range(DEPTH):
        sides = {d: plan["fwd_side"](l, d) for d in range(2)}

        def rest(res, l=l):
            fwd_res.update({(l, d): r for d, r in res.items()})
            return _weights_rest(plan["rest"](l, fwd_res))

        h, sv, _ = _layer_fwd(l, h, _weights_first(plan["first"](l, fwd_res)), rest, Ps[l], tables,
                              {d: s for d, s in sides.items() if s is not None})
        saved.append(sv)

    def loss_bwd(r, c, _):
        loss, vjp = jax.vjp(lambda a, gain: _f_loss(a, gain, r[1]), r[0], c[0])
        da, dg = vjp(jnp.ones((), F32))
        return [da], [dg, jnp.broadcast_to(loss, (1, LANES))]

    dh, g_final, loss_row = _tile_call("loss", loss_bwd, n, _pick(n, 256), [_rows(h), _rows(target)], [inp["final_g"][None]],
                                       [(D_MODEL, F32)], acc_outs=[(1, D_MODEL), (1, LANES)])
    layer_g, wgrads, bwd_res = [None] * DEPTH, [None] * DEPTH, {}
    for l in reversed(range(DEPTH)):
        dh, layer_g[l], res = _layer_bwd(l, dh, saved[l], Ps[l], tables, lambda d, g, l=l: plan["bwd_side"](l, d, g, wgrads))
        wgrads[l] = {**_grads_first(layer_g[l]), **_grads_rest(layer_g[l])}
        bwd_res.update({(l, d): r for d, r in res.items()})

    def stack(f):
        return jnp.stack([f(layer_g[l], l) for l in range(DEPTH)], axis=0)

    grads = dict(
        **{k: jnp.stack([wgrads[l][k] for l in range(DEPTH)], axis=0) for k in SHARDED},
        norm1_g=stack(lambda g, l: g["norm1_g"][0]), norm2_g=stack(lambda g, l: g["norm2_g"][0]),
        ret_gn_g=stack(lambda g, l: g["ret_gn_g"][0]), mla_q_norm_g=stack(lambda g, l: g["mla_q_norm_g"][0]),
        mla_kv_norm_g=stack(lambda g, l: g["mla_kv_norm_g"][0]), s5_d=stack(lambda g, l: g["s5_d"][0]),
        ret_decay=stack(lambda g, l: g["ret_decay"]), final_g=g_final[0])
    s5_grads = [[s5_vjps[l][d](layer_g[l]["s5"][d]) for d in range(2)] for l in range(DEPTH)]
    for i, k in enumerate(S5_NAMES):
        grads[k] = jnp.stack([jnp.stack([s5_grads[l][d][i] for d in range(2)], axis=0) for l in range(DEPTH)], axis=0)
    return loss_row[0, 0], dh, grads, wgrads, bwd_res


def kernel(*args):
    inp = dict(zip(INPUT_NAMES, args))
    kinds = ("grad_", "delta_", "new_m_", "new_v_")

    def local(l, names):
        return [inp[k][l].astype(BF16) for k in names]

    def whole(names, gathered):
        return {k: _unshard(g, SHARD_AXIS[k] - 1) for k, g in zip(names, gathered)}

    def parts(names, wg):
        return ("exchange", [_shard_split(wg[k], SHARD_AXIS[k] - 1).astype(BF16) for k in names])

    first0 = _all_gather("gather_weights0", local(0, FIRST))
    fwd_sides = {(0, 0): ("gather", local(0, REST)), (0, 1): ("gather", local(1, FIRST + REST))}

    def bwd_side(l, d, g, wgrads):
        if (l, d) == (0, 0):
            return parts(FIRST, wgrads[1])
        return parts(REST, _grads_rest(g)) if (l, d) in ((1, 0), (0, 1)) else None

    plan = dict(
        first=lambda l, res: whole(FIRST, first0 if l == 0 else res[(0, 1)][:len(FIRST)]),
        rest=lambda l, res: whole(REST, res[(0, 0)] if l == 0 else res[(0, 1)][len(FIRST):]),
        fwd_side=lambda l, d: fwd_sides.get((l, d)), bwd_side=bwd_side)

    loss, dh, grads, wgrads, landed = _local_step(inp, inp["x"][0], inp["loss_target"][0], plan)
    loss = lax.psum(loss, ("x", "y", "c"))

    landed[(0, 2)] = _exchange("exchange_grads0", parts(FIRST, wgrads[0]))
    by_layer = [dict(zip(FIRST + REST, landed[(0, 2)] + landed[(0, 1)])), dict(zip(FIRST + REST, landed[(0, 0)] + landed[(1, 0)]))]
    out = {}
    for k in SHARDED:
        land = jnp.stack([by_layer[0][k], by_layer[1][k]], axis=1)
        shape = inp[k].shape
        rows, cols = math.prod(shape[:-1]), shape[-1]
        res = _adamw("adamw_" + k, land.reshape(N_DEV, rows, cols), *[inp[p + k].reshape(rows, cols) for p in ("", "m_", "v_")],
                     _row_tile(rows))
        for kind, t in zip(kinds, res):
            out[kind + k] = t.reshape(shape)

    small_tr = 512
    partial = _pack_rows([grads[k] for k in SMALL], small_tr)
    packed = [_pack_rows([inp[p + k] for k in SMALL], small_tr) for p in ("", "m_", "v_")]
    res_small = _adamw("adamw_small", _all_gather("gather_small_grads", [partial])[0], *packed, small_tr)
    for kind, b in zip(kinds, res_small):
        for k, t in zip(SMALL, _unpack_rows(b, [inp[k].shape for k in SMALL])):
            out[kind + k] = t
    return (loss, dh[None]) + tuple(out[kind + k] for kind in kinds for k in WEIGHTS)
```
